```python
import jax, jax.numpy as jnp
from jax import lax
import numpy as np

D_MODEL = 2048
BATCH = 4
SEQ = 4096
DEPTH = 2

N_MIXERS = 2
N_HEADS = 16
HEAD_DIM = D_MODEL // N_HEADS
DILATED_GROUPS = ((128, 1), (512, 4), (2048, 16))
N_ATT_GROUPS = len(DILATED_GROUPS)
ATT_BLOCK = 128
ROPE_THETA = 10000.0
POOL_WINDOWS = (2, 4, 8, 16)
N_POOL_GROUPS = len(POOL_WINDOWS)
POOL_DIM = D_MODEL // N_POOL_GROUPS
D_FF = 256 * ((8 * D_MODEL // 3 + 255) // 256)
N_EXPERTS = 8
TOP_K = 2
PLE_DIM = 256
DN_ALPHA = (2 * DEPTH) ** 0.25
DN_BETA = (8 * DEPTH) ** -0.25
LN_EPS = 1e-5
NEG_INF = -1e30
N_EVEN = (DEPTH + 1) // 2
N_ODD = DEPTH // 2

kernel_name = "dilated_attn_pool_moe_hybrid"


def layer_norm(x, g, b):
    xf = x.astype(jnp.float32)
    mu = jnp.mean(xf, axis=-1, keepdims=True)
    xc = xf - mu
    var = jnp.mean(xc * xc, axis=-1, keepdims=True)
    return (xc * lax.rsqrt(var + LN_EPS) * g + b).astype(x.dtype)


def rope(t, pos):
    half = HEAD_DIM // 2
    inv = ROPE_THETA ** (-jnp.arange(half, dtype=jnp.float32) / half)
    ang = pos.astype(jnp.float32)[:, None] * inv[None, :]
    cos = jnp.cos(ang)[None, :, None, :].astype(t.dtype)
    sin = jnp.sin(ang)[None, :, None, :].astype(t.dtype)
    t1, t2 = t[..., :half], t[..., half:]
    return jnp.concatenate([t1 * cos - t2 * sin, t2 * cos + t1 * sin], axis=-1)


def dilated_window_attention(q, k, v, window, dilation):
    B, S, H, Dh = q.shape
    span = dilation * ATT_BLOCK
    Sp = -(-S // span) * span
    pad = Sp - S
    n_sub = Sp // dilation
    nb = n_sub // ATT_BLOCK
    n_back = window // dilation

    def to_blocks(t):
        t = jnp.pad(t, ((0, 0), (0, pad), (0, 0), (0, 0)))
        t = t.reshape(B, n_sub, dilation, H, Dh).transpose(0, 2, 1, 3, 4)
        return t.reshape(B, dilation, nb, ATT_BLOCK, H, Dh)

    def with_prev(t):
        prev = jnp.pad(t, ((0, 0), (0, 0), (1, 0), (0, 0), (0, 0), (0, 0)))[:, :, :-1]
        return jnp.concatenate([prev, t], axis=3)

    qb = to_blocks(q)
    kw = with_prev(to_blocks(k))
    vw = with_prev(to_blocks(v))
    scores = jnp.einsum('brnqhd,brnkhd->brnhqk', qb, kw).astype(jnp.float32) * (Dh ** -0.5)
    qi = jnp.arange(ATT_BLOCK)[:, None] + ATT_BLOCK
    ki = jnp.arange(2 * ATT_BLOCK)[None, :]
    dist = qi - ki
    band = (dist >= 0) & (dist <= n_back)
    has_prev = (jnp.arange(nb)[:, None, None] > 0) | (ki[None] >= ATT_BLOCK)
    mask = band[None] & has_prev
    scores = jnp.where(mask[None, None, :, None], scores, NEG_INF)
    m = jnp.max(scores, axis=-1, keepdims=True)
    e = jnp.exp(scores - m)
    l = jnp.sum(e, axis=-1, keepdims=True)
    o = jnp.einsum('brnhqk,brnkhd->brnqhd', (e / l).astype(v.dtype), vw)
    lse = (m + jnp.log(l))[..., 0]
    o = o.reshape(B, dilation, n_sub, H, Dh).transpose(0, 2, 1, 3, 4).reshape(B, Sp, H, Dh)[:, :S]
    lse = lse.transpose(0, 1, 2, 4, 3).reshape(B, dilation, n_sub, H)
    lse = lse.transpose(0, 2, 1, 3).reshape(B, Sp, H)[:, :S]
    return o.astype(jnp.float32), lse


def dilated_attention_mixer(x, w_qkv, w_o):
    B, S, _ = x.shape
    qkv = (x @ w_qkv).reshape(B, S, N_ATT_GROUPS, 3, N_HEADS, HEAD_DIM)
    pos = jnp.arange(S)
    outs, lses = [], []
    for g, (win, dil) in enumerate(DILATED_GROUPS):
        q = rope(qkv[:, :, g, 0], pos)
        k = rope(qkv[:, :, g, 1], pos)
        o, lse = dilated_window_attention(q, k, qkv[:, :, g, 2], win, dil)
        outs.append(o)
        lses.append(lse)
    wts = jax.nn.softmax(jnp.stack(lses, axis=0), axis=0)
    o = jnp.einsum('gbsh,gbshd->bshd', wts, jnp.stack(outs, axis=0)).astype(x.dtype)
    return o.reshape(B, S, N_HEADS * HEAD_DIM) @ w_o


def multiscale_pool_mixer(x, w_in, w_group, scale, w_o):
    B, S, _ = x.shape
    u = (x @ w_in).reshape(B, S, N_POOL_GROUPS, POOL_DIM)
    c = jnp.cumsum(u.astype(jnp.float32), axis=1)
    t = jnp.arange(S)
    pooled = []
    for g, w in enumerate(POOL_WINDOWS):
        cg = jnp.pad(c[:, :, g], ((0, 0), (w, 0), (0, 0)))
        s = cg[:, w:] - cg[:, :S]
        cnt = jnp.minimum(t + 1, w).astype(jnp.float32)
        pooled.append(s / cnt[None, :, None])
    pooled = jnp.stack(pooled, axis=2)
    mixed = (pooled - u.astype(jnp.float32)).astype(x.dtype)
    y = jnp.einsum('bsgc,gcd->bsgd', mixed, w_group) * scale
    return y.reshape(B, S, D_MODEL) @ w_o


def swiglu(x, w1, w3, w2):
    return (jax.nn.silu(x @ w1) * (x @ w3)) @ w2


def moe_swiglu(x, w_router, w1, w3, w2):
    logits = (x @ w_router).astype(jnp.float32)
    top_v, top_i = lax.top_k(logits, TOP_K)
    gates = jax.nn.softmax(top_v, axis=-1)
    dense_gate = jnp.sum(jax.nn.one_hot(top_i, N_EXPERTS, dtype=jnp.float32) * gates[..., None], axis=-2)
    dense_gate = dense_gate.astype(x.dtype)
    y = jnp.zeros_like(x)
    for e in range(N_EXPERTS):
        y = y + dense_gate[..., e:e + 1] * swiglu(x, w1[e], w3[e], w2[e])
    return y


def setup_inputs(seed: int = 0) -> dict:
    key = jax.random.key(seed)
    ks = jax.random.split(key, 24)
    f32 = jnp.float32
    nrm = lambda k, shape, s: jax.random.normal(k, shape, f32) * s
    att_w = N_HEADS * HEAD_DIM
    return {
        "x": nrm(ks[0], (BATCH, SEQ, D_MODEL), 1.0),
        "p": nrm(ks[1], (DEPTH, BATCH, SEQ, PLE_DIM), 1.0),
        "attn_w_qkv": nrm(ks[2], (N_EVEN, D_MODEL, N_ATT_GROUPS * 3 * att_w), D_MODEL ** -0.5),
        "attn_w_o": nrm(ks[3], (N_EVEN, att_w, D_MODEL), DN_BETA * att_w ** -0.5),
        "pool_w_in": nrm(ks[4], (N_ODD, D_MODEL, D_MODEL), D_MODEL ** -0.5),
        "pool_w_group": nrm(ks[5], (N_ODD, N_POOL_GROUPS, POOL_DIM, POOL_DIM), POOL_DIM ** -0.5),
        "pool_scale": 1.0 + nrm(ks[6], (N_ODD, N_POOL_GROUPS, POOL_DIM), 0.02),
        "pool_w_o": nrm(ks[7], (N_ODD, D_MODEL, D_MODEL), DN_BETA * D_MODEL ** -0.5),
        "ln_mix_g": 1.0 + nrm(ks[8], (DEPTH, D_MODEL), 0.02),
        "ln_mix_b": nrm(ks[9], (DEPTH, D_MODEL), 0.02),
        "ln_ffn_g": 1.0 + nrm(ks[10], (DEPTH, D_MODEL), 0.02),
        "ln_ffn_b": nrm(ks[11], (DEPTH, D_MODEL), 0.02),
        "ffn_w1": nrm(ks[12], (N_EVEN, D_MODEL, D_FF), D_MODEL ** -0.5),
        "ffn_w3": nrm(ks[13], (N_EVEN, D_MODEL, D_FF), D_MODEL ** -0.5),
        "ffn_w2": nrm(ks[14], (N_EVEN, D_FF, D_MODEL), DN_BETA * D_FF ** -0.5),
        "moe_router": nrm(ks[15], (N_ODD, D_MODEL, N_EXPERTS), D_MODEL ** -0.5),
        "moe_w1": nrm(ks[16], (N_ODD, N_EXPERTS, D_MODEL, D_FF), D_MODEL ** -0.5),
        "moe_w3": nrm(ks[17], (N_ODD, N_EXPERTS, D_MODEL, D_FF), D_MODEL ** -0.5),
        "moe_w2": nrm(ks[18], (N_ODD, N_EXPERTS, D_FF, D_MODEL), DN_BETA * D_FF ** -0.5),
        "ple_w_proj": nrm(ks[19], (DEPTH, PLE_DIM, D_MODEL), PLE_DIM ** -0.5),
        "ple_w_gate": nrm(ks[20], (DEPTH, D_MODEL, D_MODEL), D_MODEL ** -0.5),
        "ple_b_gate": nrm(ks[21], (DEPTH, D_MODEL), 0.01),
    }


def reference(x, p, attn_w_qkv, attn_w_o, pool_w_in, pool_w_group, pool_scale, pool_w_o,
              ln_mix_g, ln_mix_b, ln_ffn_g, ln_ffn_b, ffn_w1, ffn_w3, ffn_w2,
              moe_router, moe_w1, moe_w3, moe_w2, ple_w_proj, ple_w_gate, ple_b_gate):
    for i in range(DEPTH):
        j = i // N_MIXERS
        if i % N_MIXERS == 0:
            h = dilated_attention_mixer(x, attn_w_qkv[j], attn_w_o[j])
        else:
            h = multiscale_pool_mixer(x, pool_w_in[j], pool_w_group[j], pool_scale[j], pool_w_o[j])
        x = layer_norm(DN_ALPHA * x + h, ln_mix_g[i], ln_mix_b[i])
        if i % 2 == 0:
            f = swiglu(x, ffn_w1[j], ffn_w3[j], ffn_w2[j])
        else:
            f = moe_swiglu(x, moe_router[j], moe_w1[j], moe_w3[j], moe_w2[j])
        x = layer_norm(DN_ALPHA * x + f, ln_ffn_g[i], ln_ffn_b[i])
        gate = jax.nn.sigmoid((x @ ple_w_gate[i]).astype(jnp.float32) + ple_b_gate[i]).astype(x.dtype)
        x = x + gate * (p[i] @ ple_w_proj[i])
    return x
```

```python
import functools

import jax
import jax.numpy as jnp
from jax import lax
from jax.experimental import pallas as pl
from jax.experimental.pallas import tpu as pltpu

F32 = jnp.float32
BF16 = jnp.bfloat16

N_HEADS = 16
DILATED_GROUPS = ((128, 1), (512, 4), (2048, 16))
ATT_BLOCK = 128
ROPE_THETA = 10000.0
POOL_WINDOWS = (2, 4, 8, 16)
POOL_HALO = 16
TOP_K = 2
LN_EPS = 1e-5
NEG_INF = -1e30

LANES = 128
MIB = 1024 * 1024


def _tile(dim, pref):
    t = min(dim, pref)
    while dim % t:
        t //= 2
    return t


def _params(semantics, vmem_mib):
    return pltpu.CompilerParams(dimension_semantics=semantics, vmem_limit_bytes=vmem_mib * MIB)


def _layer_norm(y, g, b):
    mu = jnp.mean(y, axis=-1, keepdims=True)
    yc = y - mu
    var = jnp.mean(yc * yc, axis=-1, keepdims=True)
    return yc * lax.rsqrt(var + LN_EPS) * g + b


def _qkv_rope_kernel(x_ref, w_ref, cos_ref, sin_ref, o_ref):
    acc = jnp.dot(x_ref[...], w_ref[...], preferred_element_type=F32)
    cos = cos_ref[...]
    sin = sin_ref[...]
    for h in range(acc.shape[1] // LANES):
        t = acc[:, h * LANES:(h + 1) * LANES]
        rot = pltpu.roll(t, LANES // 2, 1)
        o_ref[:, h * LANES:(h + 1) * LANES] = (t * cos + rot * sin).astype(o_ref.dtype)


def _qkv_rope(xb, w, cos_tab, sin_tab, seq, att_w):
    m, k = xb.shape
    n = w.shape[1]
    tm = _tile(seq, 1024)
    tn = _tile(att_w, 1024)
    s_tiles = seq // tm
    per_role = att_w // tn
    return pl.pallas_call(
        _qkv_rope_kernel,
        grid=(m // tm, n // tn),
        in_specs=[
            pl.BlockSpec((tm, k), lambda i, j: (i, 0)),
            pl.BlockSpec((k, tn), lambda i, j: (0, j)),
            pl.BlockSpec((None, tm, LANES), lambda i, j: ((j // per_role) % 3, i % s_tiles, 0)),
            pl.BlockSpec((None, tm, LANES), lambda i, j: ((j // per_role) % 3, i % s_tiles, 0)),
        ],
        out_specs=pl.BlockSpec((tm, tn), lambda i, j: (i, j)),
        out_shape=jax.ShapeDtypeStruct((m, n), BF16),
        compiler_params=_params(("parallel", "parallel"), 48),
        name="qkv_rope",
    )(xb, w, cos_tab, sin_tab)


def _rope_tables(seq):
    half = LANES // 2
    inv = ROPE_THETA ** (-jnp.arange(half, dtype=F32) / half)
    ang = jnp.arange(seq).astype(F32)[:, None] * inv[None, :]
    cos, sin = jnp.cos(ang), jnp.sin(ang)
    cos_full = jnp.concatenate([cos, cos], axis=-1)
    sin_signed = jnp.concatenate([-sin, sin], axis=-1)
    ones, zeros = jnp.ones_like(cos_full), jnp.zeros_like(cos_full)
    return jnp.stack([cos_full, cos_full, ones]), jnp.stack([sin_signed, sin_signed, zeros])


def _attn_kernel(q_ref, kc_ref, kp_ref, vc_ref, vp_ref, o_ref, lse_ref, *, n_back, scale):
    n = pl.program_id(2)
    blk = q_ref.shape[0]
    qi = lax.broadcasted_iota(jnp.int32, (blk, 2 * blk), 0) + blk
    ki = lax.broadcasted_iota(jnp.int32, (blk, 2 * blk), 1)
    dist = qi - ki
    mask = (dist >= 0) & (dist <= n_back) & ((n > 0) | (ki >= blk))
    for h in range(q_ref.shape[1] // LANES):
        hs = slice(h * LANES, (h + 1) * LANES)
        k2 = jnp.concatenate([kp_ref[:, hs], kc_ref[:, hs]], axis=0)
        v2 = jnp.concatenate([vp_ref[:, hs], vc_ref[:, hs]], axis=0)
        s = lax.dot_general(q_ref[:, hs], k2, (((1,), (1,)), ((), ())),
                            preferred_element_type=F32) * scale
        s = jnp.where(mask, s, NEG_INF)
        m = jnp.max(s, axis=-1, keepdims=True)
        e = jnp.exp(s - m)
        l = jnp.sum(e, axis=-1, keepdims=True)
        p = (e / l).astype(v2.dtype)
        o_ref[:, hs] = jnp.dot(p, v2, preferred_element_type=F32)
        lse_ref[:, hs] = jnp.broadcast_to(m + jnp.log(l), (blk, LANES))


def _dilated_attention(qkv, group, batch, seq, att_w):
    window, dil = DILATED_GROUPS[group]
    assert seq % (dil * ATT_BLOCK) == 0
    n_sub = seq // dil
    nb = n_sub // ATT_BLOCK
    n_slabs = qkv.shape[1] // att_w
    view = qkv.reshape(batch, n_sub, dil * qkv.shape[1])

    def spec(role, prev):
        def index(b, r, n):
            row = jnp.maximum(n - 1, 0) if prev else n
            return (b, row, r * n_slabs + group * 3 + role)
        return pl.BlockSpec((None, ATT_BLOCK, att_w), index)

    out_spec = pl.BlockSpec((None, ATT_BLOCK, att_w), lambda b, r, n: (b, n, r))
    out_shape = jax.ShapeDtypeStruct((batch, n_sub, dil * att_w), F32)
    o, lse = pl.pallas_call(
        functools.partial(_attn_kernel, n_back=window // dil, scale=LANES ** -0.5),
        grid=(batch, dil, nb),
        in_specs=[spec(0, False), spec(1, False), spec(1, True), spec(2, False), spec(2, True)],
        out_specs=[out_spec, out_spec],
        out_shape=[out_shape, out_shape],
        compiler_params=_params(("parallel", "parallel", "arbitrary"), 32),
        name=f"dilated_attn_g{group}",
    )(view, view, view, view, view)
    return o.reshape(batch * seq, att_w), lse.reshape(batch * seq, att_w)


def _combine_groups_kernel(o0, o1, o2, l0, l1, l2, out_ref):
    a, b, c = l0[...], l1[...], l2[...]
    m = jnp.maximum(jnp.maximum(a, b), c)
    ea, eb, ec = jnp.exp(a - m), jnp.exp(b - m), jnp.exp(c - m)
    tot = ea + eb + ec
    out = (ea / tot) * o0[...] + (eb / tot) * o1[...] + (ec / tot) * o2[...]
    out_ref[...] = out.astype(out_ref.dtype)


def _combine_groups(outs, lses):
    m, n = outs[0].shape
    tm = _tile(m, 512)
    spec = pl.BlockSpec((tm, n), lambda i: (i, 0))
    return pl.pallas_call(
        _combine_groups_kernel,
        grid=(m // tm,),
        in_specs=[spec] * 6,
        out_specs=spec,
        out_shape=jax.ShapeDtypeStruct((m, n), BF16),
        compiler_params=_params(("parallel",), 64),
        name="combine_groups",
    )(*outs, *lses)


def _matmul_kernel(a_ref, w_ref, o_ref):
    o_ref[...] = jnp.dot(a_ref[...], w_ref[...], preferred_element_type=F32).astype(o_ref.dtype)


def _matmul(a, w, out_dtype):
    m, k = a.shape
    n = w.shape[1]
    tm, tn = _tile(m, 1024), _tile(n, 1024)
    return pl.pallas_call(
        _matmul_kernel,
        grid=(m // tm, n // tn),
        in_specs=[pl.BlockSpec((tm, k), lambda i, j: (i, 0)),
                  pl.BlockSpec((k, tn), lambda i, j: (0, j))],
        out_specs=pl.BlockSpec((tm, tn), lambda i, j: (i, j)),
        out_shape=jax.ShapeDtypeStruct((m, n), out_dtype),
        compiler_params=_params(("parallel", "parallel"), 48),
        name="matmul",
    )(a, w)


def _matmul_res_ln_kernel(a_ref, w_ref, x_ref, g_ref, b_ref, o32_ref, o16_ref, *, alpha):
    h = jnp.dot(a_ref[...], w_ref[...], preferred_element_type=F32)
    y = _layer_norm(alpha * x_ref[...] + h, g_ref[...], b_ref[...])
    o32_ref[...] = y
    o16_ref[...] = y.astype(o16_ref.dtype)


def _matmul_res_ln(a, w, x_res, g, b, alpha):
    m, k = a.shape
    n = w.shape[1]
    tm = _tile(m, 256)
    row = lambda i: (i, 0)
    fixed = lambda i: (0, 0)
    return pl.pallas_call(
        functools.partial(_matmul_res_ln_kernel, alpha=alpha),
        grid=(m // tm,),
        in_specs=[pl.BlockSpec((tm, k), row),
                  pl.BlockSpec((k, n), fixed, pipeline_mode=pl.Buffered(1)),
                  pl.BlockSpec((tm, n), row), pl.BlockSpec((1, n), fixed), pl.BlockSpec((1, n), fixed)],
        out_specs=[pl.BlockSpec((tm, n), row), pl.BlockSpec((tm, n), row)],
        out_shape=[jax.ShapeDtypeStruct((m, n), F32), jax.ShapeDtypeStruct((m, n), BF16)],
        compiler_params=_params(("parallel",), 60),
        name="matmul_res_ln",
    )(a, w, x_res, g.reshape(1, n), b.reshape(1, n))


def _swiglu_up_kernel(x_ref, w1_ref, w3_ref, o_ref):
    x = x_ref[...]
    a = jnp.dot(x, w1_ref[...], preferred_element_type=F32)
    b = jnp.dot(x, w3_ref[...], preferred_element_type=F32)
    o_ref[...] = (a * jax.nn.sigmoid(a) * b).astype(o_ref.dtype)


def _swiglu_up(xb, w1, w3):
    m, k = xb.shape
    n = w1.shape[1]
    tm, tn = _tile(m, 1024), _tile(n, 512)
    return pl.pallas_call(
        _swiglu_up_kernel,
        grid=(m // tm, n // tn),
        in_specs=[pl.BlockSpec((tm, k), lambda i, j: (i, 0)),
                  pl.BlockSpec((k, tn), lambda i, j: (0, j)),
                  pl.BlockSpec((k, tn), lambda i, j: (0, j))],
        out_specs=pl.BlockSpec((tm, tn), lambda i, j: (i, j)),
        out_shape=jax.ShapeDtypeStruct((m, n), BF16),
        compiler_params=_params(("parallel", "parallel"), 48),
        name="swiglu_up",
    )(xb, w1, w3)


def _ple_kernel(xb_ref, wg_ref, bg_ref, p_ref, wp_ref, x_ref, o32_ref, o16_ref):
    z = jnp.dot(xb_ref[...], wg_ref[...], preferred_element_type=F32) + bg_ref[...]
    e = jnp.dot(p_ref[...], wp_ref[...], preferred_element_type=F32)
    y = x_ref[...] + jax.nn.sigmoid(z) * e
    o32_ref[...] = y
    o16_ref[...] = y.astype(o16_ref.dtype)


def _ple(x32, xb, w_gate, b_gate, pb, w_proj):
    m, k = xb.shape
    n = w_gate.shape[1]
    kp = pb.shape[1]
    tm, tn = _tile(m, 1024), _tile(n, 1024)
    return pl.pallas_call(
        _ple_kernel,
        grid=(m // tm, n // tn),
        in_specs=[pl.BlockSpec((tm, k), lambda i, j: (i, 0)),
                  pl.BlockSpec((k, tn), lambda i, j: (0, j)),
                  pl.BlockSpec((1, tn), lambda i, j: (0, j)),
                  pl.BlockSpec((tm, kp), lambda i, j: (i, 0)),
                  pl.BlockSpec((kp, tn), lambda i, j: (0, j)),
                  pl.BlockSpec((tm, tn), lambda i, j: (i, j))],
        out_specs=[pl.BlockSpec((tm, tn), lambda i, j: (i, j))] * 2,
        out_shape=[jax.ShapeDtypeStruct((m, n), F32), jax.ShapeDtypeStruct((m, n), BF16)],
        compiler_params=_params(("parallel", "parallel"), 56),
        name="ple",
    )(xb, w_gate, b_gate.reshape(1, n), pb, w_proj, x32)


def _pool_kernel(uc_ref, up_ref, wg_ref, sc_ref, y_ref, buf_ref):
    s = pl.program_id(1)
    tm = uc_ref.shape[0]
    pd = wg_ref.shape[1]
    buf_ref[POOL_HALO:POOL_HALO + tm, :] = uc_ref[...]
    buf_ref[0:POOL_HALO, :] = jnp.where(s > 0, up_ref[...], 0.0)
    t = s * tm + lax.broadcasted_iota(jnp.int32, (tm, 1), 0)
    for g, w in enumerate(POOL_WINDOWS):
        cols = slice(g * pd, (g + 1) * pd)
        u = buf_ref[POOL_HALO:POOL_HALO + tm, cols]
        tot = u
        for i in range(1, w):
            tot = tot + buf_ref[POOL_HALO - i:POOL_HALO - i + tm, cols]
        cnt = jnp.minimum(t + 1, w).astype(F32)
        mixed = (tot / cnt - u).astype(wg_ref.dtype)
        y = jnp.dot(mixed, wg_ref[g], preferred_element_type=F32) * sc_ref[:, cols]
        y_ref[:, cols] = y.astype(y_ref.dtype)


def _pool_mix(u, w_group, scale, batch, seq):
    d = u.shape[1]
    tm = _tile(seq, 512)
    assert tm % POOL_HALO == 0 and max(POOL_WINDOWS) <= POOL_HALO
    halo_blocks = tm // POOL_HALO
    u3 = u.reshape(batch, seq, d)
    y = pl.pallas_call(
        _pool_kernel,
        grid=(batch, seq // tm),
        in_specs=[pl.BlockSpec((None, tm, d), lambda b, s: (b, s, 0)),
                  pl.BlockSpec((None, POOL_HALO, d),
                               lambda b, s: (b, jnp.maximum(s * halo_blocks - 1, 0), 0)),
                  pl.BlockSpec(w_group.shape, lambda b, s: (0, 0, 0)),
                  pl.BlockSpec((1, d), lambda b, s: (0, 0))],
        out_specs=pl.BlockSpec((None, tm, d), lambda b, s: (b, s, 0)),
        out_shape=jax.ShapeDtypeStruct((batch, seq, d), BF16),
        scratch_shapes=[pltpu.VMEM((POOL_HALO + tm, d), F32)],
        compiler_params=_params(("parallel", "arbitrary"), 48),
        name="pool_mix",
    )(u3, u3, w_group, scale.reshape(1, d))
    return y.reshape(batch * seq, d)


def _router_kernel(x_ref, wr_ref, tri_ref, e_ref, rank_ref, gate_ref, cnt_ref, run_ref):
    @pl.when(pl.program_id(0) == 0)
    def _():
        run_ref[...] = jnp.zeros_like(run_ref)

    logits = lax.dot_general(wr_ref[...], x_ref[...], (((1,), (1,)), ((), ())),
                             preferred_element_type=F32)
    n_exp = logits.shape[0]
    eidx = lax.broadcasted_iota(jnp.int32, logits.shape, 0)
    m1 = jnp.max(logits, axis=0, keepdims=True)
    i1 = jnp.min(jnp.where(logits == m1, eidx, n_exp), axis=0, keepdims=True)
    rest = jnp.where(eidx == i1, -jnp.inf, logits)
    m2 = jnp.max(rest, axis=0, keepdims=True)
    i2 = jnp.min(jnp.where(rest == m2, eidx, n_exp), axis=0, keepdims=True)
    ex = jnp.exp(m2 - m1)
    g1 = 1.0 / (1.0 + ex)
    g2 = ex / (1.0 + ex)

    oh1 = (eidx == i1).astype(F32)
    oh2 = (eidx == i2).astype(F32)
    tri = tri_ref[...]
    c1 = jnp.dot(oh1.astype(tri.dtype), tri, preferred_element_type=F32)
    c2 = jnp.dot(oh2.astype(tri.dtype), tri, preferred_element_type=F32)
    tot1 = jnp.sum(oh1, axis=1, keepdims=True)
    tot2 = jnp.sum(oh2, axis=1, keepdims=True)
    run = run_ref[:, 0:1]
    r1 = jnp.sum(oh1 * (run + c1), axis=0, keepdims=True)
    r2 = jnp.sum(oh2 * (run + tot1 + c2), axis=0, keepdims=True)
    run = run + tot1 + tot2
    run_ref[...] = jnp.broadcast_to(run, run_ref.shape)

    e_ref[...] = jnp.concatenate([i1, i2], axis=0)
    rank_ref[...] = jnp.concatenate([r1, r2], axis=0).astype(jnp.int32)
    gate_ref[...] = jnp.concatenate([g1, g2], axis=0)
    cnt_ref[...] = jnp.broadcast_to(run, cnt_ref.shape).astype(jnp.int32)


def _router(xb, w_router_t):
    m, k = xb.shape
    n_exp = w_router_t.shape[0]
    tm = _tile(m, 512)
    tri = (jnp.arange(tm)[:, None] < jnp.arange(tm)[None, :]).astype(BF16)
    pair = lambda dt: jax.ShapeDtypeStruct((TOP_K, m), dt)
    pair_spec = pl.BlockSpec((TOP_K, tm), lambda i: (0, i))
    return pl.pallas_call(
        _router_kernel,
        grid=(m // tm,),
        in_specs=[pl.BlockSpec((tm, k), lambda i: (i, 0)),
                  pl.BlockSpec((n_exp, k), lambda i: (0, 0)),
                  pl.BlockSpec((tm, tm), lambda i: (0, 0))],
        out_specs=[pair_spec, pair_spec, pair_spec, pl.BlockSpec((n_exp, LANES), lambda i: (0, 0))],
        out_shape=[pair(jnp.int32), pair(jnp.int32), pair(F32),
                   jax.ShapeDtypeStruct((n_exp, LANES), jnp.int32)],
        scratch_shapes=[pltpu.VMEM((n_exp, LANES), F32)],
        compiler_params=_params(("arbitrary",), 32),
        name="router",
    )(xb, w_router_t, tri)


def _dispatch_kernel(pos_ref, x_ref, init_ref, xs_ref, sem):
    del init_ref
    tm = x_ref.shape[0]

    def row_copy(r, k):
        return pltpu.make_async_copy(x_ref.at[pl.ds(r, 1), :],
                                     xs_ref.at[pl.ds(pos_ref[k, r], 1), :], sem)

    def start(r, carry):
        for k in range(TOP_K):
            row_copy(r, k).start()
        return carry

    def wait(r, carry):
        for k in range(TOP_K):
            row_copy(r, k).wait()
        return carry

    lax.fori_loop(0, tm, start, 0)
    lax.fori_loop(0, tm, wait, 0)


def _tile_positions(pos, tm):
    return pos.reshape(TOP_K, -1, tm).transpose(1, 0, 2)


def _dispatch(pos, x32, n_slots):
    m, d = x32.shape
    tm = _tile(m, 256)
    return pl.pallas_call(
        _dispatch_kernel,
        grid=(m // tm,),
        in_specs=[pl.BlockSpec((None, TOP_K, tm), lambda i: (i, 0, 0), memory_space=pltpu.SMEM),
                  pl.BlockSpec((tm, d), lambda i: (i, 0)),
                  pl.BlockSpec(memory_space=pl.ANY)],
        out_specs=pl.BlockSpec(memory_space=pl.ANY),
        out_shape=jax.ShapeDtypeStruct((n_slots, d), x32.dtype),
        scratch_shapes=[pltpu.SemaphoreType.DMA(())],
        input_output_aliases={2: 0},
        compiler_params=_params(("arbitrary",), 32),
        name="dispatch",
    )(_tile_positions(pos, tm), x32, jnp.zeros((n_slots, d), x32.dtype))


def _expert_up_kernel(te_ref, tv_ref, xs_ref, w1_ref, w3_ref, h_ref):
    del te_ref

    @pl.when(tv_ref[pl.program_id(1)] > 0)
    def _():
        x = xs_ref[...]
        a = jnp.dot(x, w1_ref[...], preferred_element_type=F32)
        b = jnp.dot(x, w3_ref[...], preferred_element_type=F32)
        h_ref[...] = (a * jax.nn.sigmoid(a) * b).astype(h_ref.dtype)

    @pl.when(tv_ref[pl.program_id(1)] == 0)
    def _():
        h_ref[...] = jnp.zeros_like(h_ref)


def _expert_up(tile_expert, tile_valid, xs, w1, w3, tm):
    n_slots, k = xs.shape
    n = w1.shape[2]
    tn = _tile(n, 512)
    grid_spec = pltpu.PrefetchScalarGridSpec(
        num_scalar_prefetch=2,
        grid=(n // tn, n_slots // tm),
        in_specs=[pl.BlockSpec((tm, k), lambda c, i, te, tv: (i, 0)),
                  pl.BlockSpec((None, k, tn), lambda c, i, te, tv: (te[i], 0, c)),
                  pl.BlockSpec((None, k, tn), lambda c, i, te, tv: (te[i], 0, c))],
        out_specs=pl.BlockSpec((tm, tn), lambda c, i, te, tv: (i, c)),
    )
    return pl.pallas_call(
        _expert_up_kernel,
        grid_spec=grid_spec,
        out_shape=jax.ShapeDtypeStruct((n_slots, n), BF16),
        compiler_params=_params(("parallel", "arbitrary"), 48),
        name="expert_up",
    )(tile_expert, tile_valid, xs, w1, w3)


def _expert_down_kernel(te_ref, tv_ref, h_ref, w2_ref, y_ref):
    del te_ref

    @pl.when(tv_ref[pl.program_id(1)] > 0)
    def _():
        y_ref[...] = jnp.dot(h_ref[...], w2_ref[...], preferred_element_type=F32)

    @pl.when(tv_ref[pl.program_id(1)] == 0)
    def _():
        y_ref[...] = jnp.zeros_like(y_ref)


def _expert_down(tile_expert, tile_valid, h, w2, tm):
    n_slots, k = h.shape
    n = w2.shape[2]
    tn = _tile(n, 1024)
    grid_spec = pltpu.PrefetchScalarGridSpec(
        num_scalar_prefetch=2,
        grid=(n // tn, n_slots // tm),
        in_specs=[pl.BlockSpec((tm, k), lambda c, i, te, tv: (i, 0)),
                  pl.BlockSpec((None, k, tn), lambda c, i, te, tv: (te[i], 0, c))],
        out_specs=pl.BlockSpec((tm, tn), lambda c, i, te, tv: (i, c)),
    )
    return pl.pallas_call(
        _expert_down_kernel,
        grid_spec=grid_spec,
        out_shape=jax.ShapeDtypeStruct((n_slots, n), F32),
        compiler_params=_params(("parallel", "arbitrary"), 56),
        name="expert_down",
    )(tile_expert, tile_valid, h, w2)


def _gather_mix_ln_kernel(pos_ref, ys_ref, gate_ref, x_ref, g_ref, b_ref, o32_ref, o16_ref,
                          buf_ref, sem, *, alpha):
    tm = x_ref.shape[0]

    def row_copy(r, k):
        return pltpu.make_async_copy(ys_ref.at[pl.ds(pos_ref[k, r], 1), :],
                                     buf_ref.at[k, pl.ds(r, 1), :], sem)

    def start(r, carry):
        for k in range(TOP_K):
            row_copy(r, k).start()
        return carry

    def wait(r, carry):
        for k in range(TOP_K):
            row_copy(r, k).wait()
        return carry

    lax.fori_loop(0, tm, start, 0)
    lax.fori_loop(0, tm, wait, 0)
    gates = gate_ref[...]
    f = gates[:, 0:1] * buf_ref[0] + gates[:, 1:2] * buf_ref[1]
    y = _layer_norm(alpha * x_ref[...] + f, g_ref[...], b_ref[...])
    o32_ref[...] = y
    o16_ref[...] = y.astype(o16_ref.dtype)


def _gather_mix_ln(pos, ys, gates, x_res, g, b, alpha):
    m, d = x_res.shape
    tm = _tile(m, 256)
    row = lambda i: (i, 0)
    fixed = lambda i: (0, 0)
    return pl.pallas_call(
        functools.partial(_gather_mix_ln_kernel, alpha=alpha),
        grid=(m // tm,),
        in_specs=[pl.BlockSpec((None, TOP_K, tm), lambda i: (i, 0, 0), memory_space=pltpu.SMEM),
                  pl.BlockSpec(memory_space=pl.ANY),
                  pl.BlockSpec((tm, TOP_K), row),
                  pl.BlockSpec((tm, d), row),
                  pl.BlockSpec((1, d), fixed), pl.BlockSpec((1, d), fixed)],
        out_specs=[pl.BlockSpec((tm, d), row), pl.BlockSpec((tm, d), row)],
        out_shape=[jax.ShapeDtypeStruct((m, d), F32), jax.ShapeDtypeStruct((m, d), BF16)],
        scratch_shapes=[pltpu.VMEM((TOP_K, tm, d), F32), pltpu.SemaphoreType.DMA(())],
        compiler_params=_params(("arbitrary",), 32),
        name="gather_mix_ln",
    )(_tile_positions(pos, tm), ys, gates, x_res, g.reshape(1, d), b.reshape(1, d))


def _moe_ffn_ln(x32, xb, w_router, w1, w3, w2, g, b, alpha):
    m, _ = xb.shape
    n_exp = w1.shape[0]
    tm = _tile(m, 512)
    experts, ranks, gates, counts = _router(xb, w_router.T.astype(BF16))

    counts = counts[:, 0]
    padded = (counts + tm - 1) // tm * tm
    ends = jnp.cumsum(padded)
    starts = ends - padded
    pos = starts[experts] + ranks
    n_slots = TOP_K * m + n_exp * tm
    tile_start = jnp.arange(n_slots // tm, dtype=jnp.int32) * tm
    tile_valid = (tile_start < ends[-1]).astype(jnp.int32)
    last_tile = jnp.maximum(ends[-1] // tm - 1, 0)
    tile_expert = jnp.searchsorted(ends, jnp.minimum(tile_start, last_tile * tm), side="right")
    tile_expert = jnp.minimum(tile_expert, n_exp - 1).astype(jnp.int32)

    xs = _dispatch(pos, x32, n_slots).astype(BF16)
    h = _expert_up(tile_expert, tile_valid, xs, w1, w3, tm)
    ys = _expert_down(tile_expert, tile_valid, h, w2, tm)
    return _gather_mix_ln(pos, ys, gates.T, x32, g, b, alpha)


def kernel(x, p, attn_w_qkv, attn_w_o, pool_w_in, pool_w_group, pool_scale, pool_w_o,
           ln_mix_g, ln_mix_b, ln_ffn_g, ln_ffn_b, ffn_w1, ffn_w3, ffn_w2,
           moe_router, moe_w1, moe_w3, moe_w2, ple_w_proj, ple_w_gate, ple_b_gate):
    batch, seq, d = x.shape
    depth = p.shape[0]
    att_w = N_HEADS * LANES
    assert attn_w_qkv.shape[2] == len(DILATED_GROUPS) * 3 * att_w
    alpha = (2 * depth) ** 0.25
    m = batch * seq
    bf = lambda t: t.astype(BF16)

    x32 = x.reshape(m, d)
    xb = bf(x32)
    cos_tab, sin_tab = _rope_tables(seq)
    for i in range(depth):
        j = i // 2
        if i % 2 == 0:
            qkv = _qkv_rope(xb, bf(attn_w_qkv[j]), cos_tab, sin_tab, seq, att_w)
            outs, lses = zip(*[_dilated_attention(qkv, grp, batch, seq, att_w)
                               for grp in range(len(DILATED_GROUPS))])
            mixed = _combine_groups(outs, lses)
            x32, xb = _matmul_res_ln(mixed, bf(attn_w_o[j]), x32, ln_mix_g[i], ln_mix_b[i], alpha)
            h = _swiglu_up(xb, bf(ffn_w1[j]), bf(ffn_w3[j]))
            x32, xb = _matmul_res_ln(h, bf(ffn_w2[j]), x32, ln_ffn_g[i], ln_ffn_b[i], alpha)
        else:
            u = _matmul(xb, bf(pool_w_in[j]), F32)
            y = _pool_mix(u, bf(pool_w_group[j]), pool_scale[j].reshape(-1), batch, seq)
            x32, xb = _matmul_res_ln(y, bf(pool_w_o[j]), x32, ln_mix_g[i], ln_mix_b[i], alpha)
            x32, xb = _moe_ffn_ln(x32, xb, moe_router[j], bf(moe_w1[j]), bf(moe_w3[j]),
                                  bf(moe_w2[j]), ln_ffn_g[i], ln_ffn_b[i], alpha)
        x32, xb = _ple(x32, xb, bf(ple_w_gate[i]), ple_b_gate[i], bf(p[i].reshape(m, -1)),
                       bf(ple_w_proj[i]))
    return x32.reshape(batch, seq, d)
```

```python
import functools

import jax
import jax.numpy as jnp
from jax import lax
from jax.experimental import pallas as pl
from jax.experimental.pallas import tpu as pltpu

F32 = jnp.float32
BF16 = jnp.bfloat16

N_HEADS = 16
DILATED_GROUPS = ((128, 1), (512, 4), (2048, 16))
N_GROUPS = len(DILATED_GROUPS)
ATT_BLOCK = 128
ROPE_THETA = 10000.0
POOL_WINDOWS = (2, 4, 8, 16)
POOL_HALO = 16
TOP_K = 2
LN_EPS = 1e-5
NEG_INF = -1e30

LANES = 128
MIB = 1024 * 1024


def _tile(dim, pref):
    t = min(dim, pref)
    while dim % t:
        t //= 2
    return t


def _params(semantics, vmem_mib, **kw):
    return pltpu.CompilerParams(dimension_semantics=semantics, vmem_limit_bytes=vmem_mib * MIB, **kw)


def _layer_norm(y, g, b):
    mu = jnp.mean(y, axis=-1, keepdims=True)
    yc = y - mu
    var = jnp.mean(yc * yc, axis=-1, keepdims=True)
    return yc * lax.rsqrt(var + LN_EPS) * g + b


def _residue_major(t, batch, seq, dil):
    d = t.shape[1]
    return t.reshape(batch, seq // dil, dil, d).transpose(0, 2, 1, 3).reshape(batch * seq, d)


def _qkv_rope_kernel(x_ref, w_ref, cos_ref, sin_ref, o_ref, wb_ref):
    @pl.when(pl.program_id(2) == 0)
    def _():
        wb_ref[...] = w_ref[...].astype(wb_ref.dtype)

    acc = jnp.dot(x_ref[...], wb_ref[...], preferred_element_type=F32)
    cos = cos_ref[...]
    sin = sin_ref[...]
    for h in range(acc.shape[1] // LANES):
        t = acc[:, h * LANES:(h + 1) * LANES]
        rot = pltpu.roll(t, LANES // 2, 1)
        o_ref[:, h * LANES:(h + 1) * LANES] = (t * cos + rot * sin).astype(o_ref.dtype)


def _qkv_rope(xg, w, cos_tab, sin_tab, seq, att_w):
    n_groups, m, k = xg.shape
    tm = _tile(seq, 1024)
    tn = _tile(att_w, 1024)
    s_tiles = seq // tm
    per_role = att_w // tn
    per_group = 3 * per_role
    return pl.pallas_call(
        _qkv_rope_kernel,
        grid=(n_groups, per_group, m // tm),
        in_specs=[
            pl.BlockSpec((None, tm, k), lambda g, j, i: (g, i, 0)),
            pl.BlockSpec((k, tn), lambda g, j, i: (0, g * per_group + j)),
            pl.BlockSpec((None, None, tm, LANES), lambda g, j, i: (g, j // per_role, i % s_tiles, 0)),
            pl.BlockSpec((None, None, tm, LANES), lambda g, j, i: (g, j // per_role, i % s_tiles, 0)),
        ],
        out_specs=pl.BlockSpec((None, tm, tn), lambda g, j, i: (g, i, j)),
        out_shape=jax.ShapeDtypeStruct((n_groups, m, 3 * att_w), BF16),
        scratch_shapes=[pltpu.VMEM((k, tn), BF16)],
        compiler_params=_params(("parallel", "parallel", "arbitrary"), 56),
        name="qkv_rope",
    )(xg, w, cos_tab, sin_tab)


def _rope_tables(seq):
    half = LANES // 2
    inv = ROPE_THETA ** (-jnp.arange(half, dtype=F32) / half)
    row = jnp.arange(seq)
    cos_tabs, sin_tabs = [], []
    for _, dil in DILATED_GROUPS:
        n_sub = seq // dil
        pos = (row % n_sub) * dil + row // n_sub
        ang = pos.astype(F32)[:, None] * inv[None, :]
        cos, sin = jnp.cos(ang), jnp.sin(ang)
        cos_full = jnp.concatenate([cos, cos], axis=-1)
        sin_signed = jnp.concatenate([-sin, sin], axis=-1)
        cos_tabs.append(jnp.stack([cos_full, cos_full, jnp.ones_like(cos_full)]))
        sin_tabs.append(jnp.stack([sin_signed, sin_signed, jnp.zeros_like(cos_full)]))
    return jnp.stack(cos_tabs), jnp.stack(sin_tabs)


def _attn_kernel(nb_ref, q_ref, kc_ref, kp_ref, vc_ref, vp_ref, o_ref, lse_ref, *, n_back, scale):
    blk = q_ref.shape[0]
    nb = nb_ref[pl.program_id(0)]
    has_prev = lax.rem(pl.program_id(1), nb) > 0
    qi = lax.broadcasted_iota(jnp.int32, (blk, 2 * blk), 0) + blk
    ki = lax.broadcasted_iota(jnp.int32, (blk, 2 * blk), 1)
    dist = qi - ki
    mask = (dist >= 0) & (dist <= n_back) & (has_prev | (ki >= blk))
    bias = jnp.where(mask, 0.0, NEG_INF)
    lane = lax.broadcasted_iota(jnp.int32, (blk, LANES), 1)
    lse_tile = jnp.zeros((blk, LANES), F32)
    for h in range(q_ref.shape[1] // LANES):
        hs = slice(h * LANES, (h + 1) * LANES)
        k2 = jnp.concatenate([kp_ref[:, hs], kc_ref[:, hs]], axis=0)
        v2 = jnp.concatenate([vp_ref[:, hs], vc_ref[:, hs]], axis=0)
        s = lax.dot_general(q_ref[:, hs], k2, (((1,), (1,)), ((), ())),
                            preferred_element_type=F32) * scale + bias
        m = jnp.max(s, axis=-1, keepdims=True)
        e = jnp.exp(s - m)
        l = jnp.sum(e, axis=-1, keepdims=True)
        p = (e * (1.0 / l)).astype(v2.dtype)
        o_ref[:, hs] = jnp.dot(p, v2, preferred_element_type=F32)
        lse_tile = jnp.where(lane == h, m + jnp.log(l), lse_tile)
    lse_ref[...] = lse_tile


def _dilated_attention(qkv, seq, att_w):
    n_groups, m, _ = qkv.shape
    n_backs = {w // d for w, d in DILATED_GROUPS}
    assert n_backs == {ATT_BLOCK} and all(seq % (d * ATT_BLOCK) == 0 for _, d in DILATED_GROUPS)
    nbs = jnp.array([seq // (d * ATT_BLOCK) for _, d in DILATED_GROUPS], jnp.int32)

    def spec(role, prev):
        def index(g, r, nb):
            return (g, jnp.maximum(r - 1, 0) if prev else r, role)
        return pl.BlockSpec((None, ATT_BLOCK, att_w), index)

    grid_spec = pltpu.PrefetchScalarGridSpec(
        num_scalar_prefetch=1,
        grid=(n_groups, m // ATT_BLOCK),
        in_specs=[spec(0, False), spec(1, False), spec(1, True), spec(2, False), spec(2, True)],
        out_specs=[pl.BlockSpec((None, ATT_BLOCK, att_w), lambda g, r, nb: (g, r, 0)),
                   pl.BlockSpec((None, ATT_BLOCK, LANES), lambda g, r, nb: (g, r, 0))],
    )
    return pl.pallas_call(
        functools.partial(_attn_kernel, n_back=ATT_BLOCK, scale=LANES ** -0.5),
        grid_spec=grid_spec,
        out_shape=[jax.ShapeDtypeStruct((n_groups, m, att_w), F32),
                   jax.ShapeDtypeStruct((n_groups, m, LANES), F32)],
        compiler_params=_params(("parallel", "arbitrary"), 32),
        name="dilated_attn",
    )(nbs, qkv, qkv, qkv, qkv, qkv)


def _token_order(ref, scr_ref):
    dil, n, w = ref.shape
    heads = range(w // LANES)
    if dil == 1:
        return [ref[0, :, h * LANES:(h + 1) * LANES] for h in heads]
    for r in range(dil):
        for h in heads:
            scr_ref[h, pl.ds(r, n, stride=dil), :] = ref[r, :, h * LANES:(h + 1) * LANES]
    return [scr_ref[h] for h in heads]


def _attn_out_kernel(*refs, alpha):
    o_refs, l_refs = refs[:N_GROUPS], refs[N_GROUPS:2 * N_GROUPS]
    w_ref, x_ref, g_ref, b_ref, o32_ref, o16_ref = refs[2 * N_GROUPS:2 * N_GROUPS + 6]
    scratch = refs[2 * N_GROUPS + 6:]
    o_scr, l_scr = scratch[:N_GROUPS], scratch[N_GROUPS:]
    outs = [_token_order(o_refs[g], o_scr[g]) for g in range(N_GROUPS)]
    lses = [_token_order(l_refs[g], l_scr[g])[0] for g in range(N_GROUPS)]
    m = functools.reduce(jnp.maximum, lses)
    es = [jnp.exp(l - m) for l in lses]
    inv = 1.0 / functools.reduce(jnp.add, es)
    wts = [e * inv for e in es]
    tm = x_ref.shape[0]
    heads = []
    for h in range(len(outs[0])):
        acc = None
        for g in range(N_GROUPS):
            term = jnp.broadcast_to(wts[g][:, h:h + 1], (tm, LANES)) * outs[g][h]
            acc = term if acc is None else acc + term
        heads.append(acc.astype(w_ref.dtype))
    mixed = jnp.concatenate(heads, axis=1)
    hid = jnp.dot(mixed, w_ref[...], preferred_element_type=F32)
    y = _layer_norm(alpha * x_ref[...] + hid, g_ref[...], b_ref[...])
    o32_ref[...] = y
    o16_ref[...] = y.astype(o16_ref.dtype)


def _attn_out(o_all, lse_all, w_o, x_res, g, b, alpha, batch, seq):
    n_groups, m, att_w = o_all.shape
    d = w_o.shape[1]
    dils = [dil for _, dil in DILATED_GROUPS]
    tm = _tile(seq, 256)
    s_tiles = seq // tm
    assert all(tm % (8 * dil) == 0 for dil in dils)

    def group_view(t, g):
        return t.reshape(n_groups, batch, dils[g], seq // dils[g], t.shape[-1])

    def group_spec(g, width):
        return pl.BlockSpec((None, None, dils[g], tm // dils[g], width),
                            lambda i: (g, i // s_tiles, 0, i % s_tiles, 0))

    row = lambda i: (i, 0)
    fixed = lambda i: (0, 0)
    return pl.pallas_call(
        functools.partial(_attn_out_kernel, alpha=alpha),
        grid=(m // tm,),
        in_specs=[group_spec(g, att_w) for g in range(n_groups)]
        + [group_spec(g, LANES) for g in range(n_groups)]
        + [pl.BlockSpec((att_w, d), fixed, pipeline_mode=pl.Buffered(1)),
           pl.BlockSpec((tm, d), row), pl.BlockSpec((1, d), fixed), pl.BlockSpec((1, d), fixed)],
        out_specs=[pl.BlockSpec((tm, d), row), pl.BlockSpec((tm, d), row)],
        out_shape=[jax.ShapeDtypeStruct((m, d), F32), jax.ShapeDtypeStruct((m, d), BF16)],
        scratch_shapes=[pltpu.VMEM((att_w // LANES, tm, LANES), F32) for _ in range(n_groups)]
        + [pltpu.VMEM((1, tm, LANES), F32) for _ in range(n_groups)],
        compiler_params=_params(("parallel",), 56),
        name="attn_out_ln",
    )(*[group_view(o_all, g) for g in range(n_groups)],
      *[group_view(lse_all, g) for g in range(n_groups)],
      w_o, x_res, g.reshape(1, d), b.reshape(1, d))


def _matmul_kernel(a_ref, w_ref, o_ref):
    o_ref[...] = jnp.dot(a_ref[...], w_ref[...], preferred_element_type=F32).astype(o_ref.dtype)


def _matmul(a, w, out_dtype):
    m, k = a.shape
    n = w.shape[1]
    tm, tn = _tile(m, 1024), _tile(n, 1024)
    return pl.pallas_call(
        _matmul_kernel,
        grid=(m // tm, n // tn),
        in_specs=[pl.BlockSpec((tm, k), lambda i, j: (i, 0)),
                  pl.BlockSpec((k, tn), lambda i, j: (0, j))],
        out_specs=pl.BlockSpec((tm, tn), lambda i, j: (i, j)),
        out_shape=jax.ShapeDtypeStruct((m, n), out_dtype),
        compiler_params=_params(("parallel", "parallel"), 48),
        name="matmul",
    )(a, w)


def _matmul_res_ln_kernel(a_ref, w_ref, x_ref, g_ref, b_ref, o32_ref, o16_ref, *, alpha):
    h = jnp.dot(a_ref[...], w_ref[...], preferred_element_type=F32)
    y = _layer_norm(alpha * x_ref[...] + h, g_ref[...], b_ref[...])
    o32_ref[...] = y
    o16_ref[...] = y.astype(o16_ref.dtype)


def _matmul_res_ln(a, w, x_res, g, b, alpha):
    m, k = a.shape
    n = w.shape[1]
    tm = _tile(m, 256)
    row = lambda i: (i, 0)
    fixed = lambda i: (0, 0)
    return pl.pallas_call(
        functools.partial(_matmul_res_ln_kernel, alpha=alpha),
        grid=(m // tm,),
        in_specs=[pl.BlockSpec((tm, k), row),
                  pl.BlockSpec((k, n), fixed, pipeline_mode=pl.Buffered(1)),
                  pl.BlockSpec((tm, n), row), pl.BlockSpec((1, n), fixed), pl.BlockSpec((1, n), fixed)],
        out_specs=[pl.BlockSpec((tm, n), row), pl.BlockSpec((tm, n), row)],
        out_shape=[jax.ShapeDtypeStruct((m, n), F32), jax.ShapeDtypeStruct((m, n), BF16)],
        compiler_params=_params(("parallel",), 60),
        name="matmul_res_ln",
    )(a, w, x_res, g.reshape(1, n), b.reshape(1, n))


def _swiglu_up_kernel(x_ref, w1_ref, w3_ref, o_ref, w1b_ref, w3b_ref):
    @pl.when(pl.program_id(1) == 0)
    def _():
        w1b_ref[...] = w1_ref[...].astype(w1b_ref.dtype)
        w3b_ref[...] = w3_ref[...].astype(w3b_ref.dtype)

    x = x_ref[...]
    a = jnp.dot(x, w1b_ref[...], preferred_element_type=F32)
    b = jnp.dot(x, w3b_ref[...], preferred_element_type=F32)
    o_ref[...] = (a * jax.nn.sigmoid(a) * b).astype(o_ref.dtype)


def _swiglu_up(xb, w1, w3):
    m, k = xb.shape
    n = w1.shape[1]
    tm, tn = _tile(m, 1024), _tile(n, 512)
    return pl.pallas_call(
        _swiglu_up_kernel,
        grid=(n // tn, m // tm),
        in_specs=[pl.BlockSpec((tm, k), lambda j, i: (i, 0)),
                  pl.BlockSpec((k, tn), lambda j, i: (0, j)),
                  pl.BlockSpec((k, tn), lambda j, i: (0, j))],
        out_specs=pl.BlockSpec((tm, tn), lambda j, i: (i, j)),
        out_shape=jax.ShapeDtypeStruct((m, n), BF16),
        scratch_shapes=[pltpu.VMEM((k, tn), BF16), pltpu.VMEM((k, tn), BF16)],
        compiler_params=_params(("parallel", "arbitrary"), 56),
        name="swiglu_up",
    )(xb, w1, w3)


def _ple_kernel(xb_ref, wg_ref, bg_ref, p_ref, wp_ref, x_ref, o32_ref, o16_ref):
    z = jnp.dot(xb_ref[...], wg_ref[...], preferred_element_type=F32) + bg_ref[...]
    e = jnp.dot(p_ref[...], wp_ref[...], preferred_element_type=F32)
    y = x_ref[...] + jax.nn.sigmoid(z) * e
    o32_ref[...] = y
    o16_ref[...] = y.astype(o16_ref.dtype)


def _ple(x32, xb, w_gate, b_gate, pb, w_proj):
    m, k = xb.shape
    n = w_gate.shape[1]
    kp = pb.shape[1]
    tm, tn = _tile(m, 1024), _tile(n, 1024)
    return pl.pallas_call(
        _ple_kernel,
        grid=(m // tm, n // tn),
        in_specs=[pl.BlockSpec((tm, k), lambda i, j: (i, 0)),
                  pl.BlockSpec((k, tn), lambda i, j: (0, j)),
                  pl.BlockSpec((1, tn), lambda i, j: (0, j)),
                  pl.BlockSpec((tm, kp), lambda i, j: (i, 0)),
                  pl.BlockSpec((kp, tn), lambda i, j: (0, j)),
                  pl.BlockSpec((tm, tn), lambda i, j: (i, j))],
        out_specs=[pl.BlockSpec((tm, tn), lambda i, j: (i, j))] * 2,
        out_shape=[jax.ShapeDtypeStruct((m, n), F32), jax.ShapeDtypeStruct((m, n), BF16)],
        compiler_params=_params(("parallel", "parallel"), 56),
        name="ple",
    )(xb, w_gate, b_gate.reshape(1, n), pb, w_proj, x32)


def _pool_kernel(uc_ref, up_ref, wg_ref, sc_ref, y_ref, buf_ref):
    s = pl.program_id(1)
    tm = uc_ref.shape[0]
    pd = wg_ref.shape[1]
    buf_ref[POOL_HALO:POOL_HALO + tm, :] = uc_ref[...]
    buf_ref[0:POOL_HALO, :] = jnp.where(s > 0, up_ref[...], 0.0)
    t = s * tm + lax.broadcasted_iota(jnp.int32, (tm, 1), 0)
    for g, w in enumerate(POOL_WINDOWS):
        cols = slice(g * pd, (g + 1) * pd)
        u = buf_ref[POOL_HALO:POOL_HALO + tm, cols]
        tot = u
        for i in range(1, w):
            tot = tot + buf_ref[POOL_HALO - i:POOL_HALO - i + tm, cols]
        cnt = jnp.minimum(t + 1, w).astype(F32)
        mixed = (tot / cnt - u).astype(wg_ref.dtype)
        y = jnp.dot(mixed, wg_ref[g], preferred_element_type=F32) * sc_ref[:, cols]
        y_ref[:, cols] = y.astype(y_ref.dtype)


def _pool_mix(u, w_group, scale, batch, seq):
    d = u.shape[1]
    tm = _tile(seq, 512)
    assert tm % POOL_HALO == 0 and max(POOL_WINDOWS) <= POOL_HALO
    halo_blocks = tm // POOL_HALO
    u3 = u.reshape(batch, seq, d)
    y = pl.pallas_call(
        _pool_kernel,
        grid=(batch, seq // tm),
        in_specs=[pl.BlockSpec((None, tm, d), lambda b, s: (b, s, 0)),
                  pl.BlockSpec((None, POOL_HALO, d),
                               lambda b, s: (b, jnp.maximum(s * halo_blocks - 1, 0), 0)),
                  pl.BlockSpec(w_group.shape, lambda b, s: (0, 0, 0)),
                  pl.BlockSpec((1, d), lambda b, s: (0, 0))],
        out_specs=pl.BlockSpec((None, tm, d), lambda b, s: (b, s, 0)),
        out_shape=jax.ShapeDtypeStruct((batch, seq, d), BF16),
        scratch_shapes=[pltpu.VMEM((POOL_HALO + tm, d), F32)],
        compiler_params=_params(("parallel", "arbitrary"), 48),
        name="pool_mix",
    )(u3, u3, w_group, scale.reshape(1, d))
    return y.reshape(batch * seq, d)


def _router_kernel(x_ref, wr_ref, tri_ref, e_ref, rank_ref, gate_ref, cnt_ref, run_ref):
    @pl.when(pl.program_id(0) == 0)
    def _():
        run_ref[...] = jnp.zeros_like(run_ref)

    logits = lax.dot_general(wr_ref[...], x_ref[...], (((1,), (1,)), ((), ())),
                             preferred_element_type=F32)
    n_exp = logits.shape[0]
    eidx = lax.broadcasted_iota(jnp.int32, logits.shape, 0)
    m1 = jnp.max(logits, axis=0, keepdims=True)
    i1 = jnp.min(jnp.where(logits == m1, eidx, n_exp), axis=0, keepdims=True)
    rest = jnp.where(eidx == i1, -jnp.inf, logits)
    m2 = jnp.max(rest, axis=0, keepdims=True)
    i2 = jnp.min(jnp.where(rest == m2, eidx, n_exp), axis=0, keepdims=True)
    ex = jnp.exp(m2 - m1)
    g1 = 1.0 / (1.0 + ex)
    g2 = ex / (1.0 + ex)

    oh1 = (eidx == i1).astype(F32)
    oh2 = (eidx == i2).astype(F32)
    tri = tri_ref[...]
    c1 = jnp.dot(oh1.astype(tri.dtype), tri, preferred_element_type=F32)
    c2 = jnp.dot(oh2.astype(tri.dtype), tri, preferred_element_type=F32)
    tot1 = jnp.sum(oh1, axis=1, keepdims=True)
    tot2 = jnp.sum(oh2, axis=1, keepdims=True)
    run = run_ref[:, 0:1]
    r1 = jnp.sum(oh1 * (run + c1), axis=0, keepdims=True)
    r2 = jnp.sum(oh2 * (run + tot1 + c2), axis=0, keepdims=True)
    run = run + tot1 + tot2
    run_ref[...] = jnp.broadcast_to(run, run_ref.shape)

    e_ref[...] = jnp.concatenate([i1, i2], axis=0)
    rank_ref[...] = jnp.concatenate([r1, r2], axis=0).astype(jnp.int32)
    gate_ref[...] = jnp.concatenate([g1, g2], axis=0)
    cnt_ref[...] = jnp.broadcast_to(run, cnt_ref.shape).astype(jnp.int32)


def _router(xb, w_router_t):
    m, k = xb.shape
    n_exp = w_router_t.shape[0]
    tm = _tile(m, 512)
    tri = (jnp.arange(tm)[:, None] < jnp.arange(tm)[None, :]).astype(BF16)
    pair = lambda dt: jax.ShapeDtypeStruct((TOP_K, m), dt)
    pair_spec = pl.BlockSpec((TOP_K, tm), lambda i: (0, i))
    return pl.pallas_call(
        _router_kernel,
        grid=(m // tm,),
        in_specs=[pl.BlockSpec((tm, k), lambda i: (i, 0)),
                  pl.BlockSpec((n_exp, k), lambda i: (0, 0)),
                  pl.BlockSpec((tm, tm), lambda i: (0, 0))],
        out_specs=[pair_spec, pair_spec, pair_spec, pl.BlockSpec((n_exp, LANES), lambda i: (0, 0))],
        out_shape=[pair(jnp.int32), pair(jnp.int32), pair(F32),
                   jax.ShapeDtypeStruct((n_exp, LANES), jnp.int32)],
        scratch_shapes=[pltpu.VMEM((n_exp, LANES), F32)],
        compiler_params=_params(("arbitrary",), 32),
        name="router",
    )(xb, w_router_t, tri)


def _tile_positions(pos, tm):
    return pos.reshape(TOP_K, -1, tm).transpose(1, 0, 2)


def _dispatch_kernel(pos_ref, x_ref, init_ref, xs_ref, sem):
    del init_ref
    tm = x_ref.shape[0]

    def row_copy(r, k):
        return pltpu.make_async_copy(x_ref.at[pl.ds(r, 1), :],
                                     xs_ref.at[pl.ds(pos_ref[k, r], 1), :], sem)

    def start(r, carry):
        for k in range(TOP_K):
            row_copy(r, k).start()
        return carry

    def wait(r, carry):
        for k in range(TOP_K):
            row_copy(r, k).wait()
        return carry

    lax.fori_loop(0, tm, start, 0)
    lax.fori_loop(0, tm, wait, 0)


def _dispatch(pos, x32, n_slots):
    m, d = x32.shape
    tm = _tile(m, 256)
    return pl.pallas_call(
        _dispatch_kernel,
        grid=(m // tm,),
        in_specs=[pl.BlockSpec((None, TOP_K, tm), lambda i: (i, 0, 0), memory_space=pltpu.SMEM),
                  pl.BlockSpec((tm, d), lambda i: (i, 0)),
                  pl.BlockSpec(memory_space=pl.ANY)],
        out_specs=pl.BlockSpec(memory_space=pl.ANY),
        out_shape=jax.ShapeDtypeStruct((n_slots, d), x32.dtype),
        scratch_shapes=[pltpu.SemaphoreType.DMA(())],
        input_output_aliases={2: 0},
        compiler_params=_params(("arbitrary",), 32),
        name="dispatch",
    )(_tile_positions(pos, tm), x32, jnp.zeros((n_slots, d), x32.dtype))


def _cast_expert_weights(first_ref, pairs):
    @pl.when(first_ref[pl.program_id(1)] > 0)
    def _():
        for src, dst in pairs:
            dst[...] = src[...].astype(dst.dtype)


def _expert_up_kernel(te_ref, tv_ref, tf_ref, xs_ref, w1_ref, w3_ref, h_ref, w1b_ref, w3b_ref):
    del te_ref
    _cast_expert_weights(tf_ref, [(w1_ref, w1b_ref), (w3_ref, w3b_ref)])

    @pl.when(tv_ref[pl.program_id(1)] > 0)
    def _():
        x = xs_ref[...]
        a = jnp.dot(x, w1b_ref[...], preferred_element_type=F32)
        b = jnp.dot(x, w3b_ref[...], preferred_element_type=F32)
        h_ref[...] = (a * jax.nn.sigmoid(a) * b).astype(h_ref.dtype)

    @pl.when(tv_ref[pl.program_id(1)] == 0)
    def _():
        h_ref[...] = jnp.zeros_like(h_ref)


def _expert_up(tiles, xs, w1, w3, tm):
    n_slots, k = xs.shape
    n = w1.shape[2]
    tn = _tile(n, 512)
    grid_spec = pltpu.PrefetchScalarGridSpec(
        num_scalar_prefetch=3,
        grid=(n // tn, n_slots // tm),
        in_specs=[pl.BlockSpec((tm, k), lambda c, i, te, tv, tf: (i, 0)),
                  pl.BlockSpec((None, k, tn), lambda c, i, te, tv, tf: (te[i], 0, c)),
                  pl.BlockSpec((None, k, tn), lambda c, i, te, tv, tf: (te[i], 0, c))],
        out_specs=pl.BlockSpec((tm, tn), lambda c, i, te, tv, tf: (i, c)),
        scratch_shapes=[pltpu.VMEM((k, tn), BF16), pltpu.VMEM((k, tn), BF16)],
    )
    return pl.pallas_call(
        _expert_up_kernel,
        grid_spec=grid_spec,
        out_shape=jax.ShapeDtypeStruct((n_slots, n), BF16),
        compiler_params=_params(("parallel", "arbitrary"), 48),
        name="expert_up",
    )(*tiles, xs, w1, w3)


def _expert_down_kernel(te_ref, tv_ref, tf_ref, h_ref, w2_ref, y_ref, w2b_ref):
    del te_ref
    _cast_expert_weights(tf_ref, [(w2_ref, w2b_ref)])

    @pl.when(tv_ref[pl.program_id(1)] > 0)
    def _():
        y_ref[...] = jnp.dot(h_ref[...], w2b_ref[...], preferred_element_type=F32)

    @pl.when(tv_ref[pl.program_id(1)] == 0)
    def _():
        y_ref[...] = jnp.zeros_like(y_ref)


def _expert_down(tiles, h, w2, tm):
    n_slots, k = h.shape
    n = w2.shape[2]
    tn = _tile(n, 512)
    grid_spec = pltpu.PrefetchScalarGridSpec(
        num_scalar_prefetch=3,
        grid=(n // tn, n_slots // tm),
        in_specs=[pl.BlockSpec((tm, k), lambda c, i, te, tv, tf: (i, 0)),
                  pl.BlockSpec((None, k, tn), lambda c, i, te, tv, tf: (te[i], 0, c))],
        out_specs=pl.BlockSpec((tm, tn), lambda c, i, te, tv, tf: (i, c)),
        scratch_shapes=[pltpu.VMEM((k, tn), BF16)],
    )
    return pl.pallas_call(
        _expert_down_kernel,
        grid_spec=grid_spec,
        out_shape=jax.ShapeDtypeStruct((n_slots, n), F32),
        compiler_params=_params(("parallel", "arbitrary"), 56),
        name="expert_down",
    )(*tiles, h, w2)


def _gather_mix_ln_kernel(pos_ref, ys_ref, gate_ref, x_ref, g_ref, b_ref, o32_ref, o16_ref,
                          buf_ref, sem, *, alpha):
    tm = x_ref.shape[0]

    def row_copy(r, k):
        return pltpu.make_async_copy(ys_ref.at[pl.ds(pos_ref[k, r], 1), :],
                                     buf_ref.at[k, pl.ds(r, 1), :], sem)

    def start(r, carry):
        for k in range(TOP_K):
            row_copy(r, k).start()
        return carry

    def wait(r, carry):
        for k in range(TOP_K):
            row_copy(r, k).wait()
        return carry

    lax.fori_loop(0, tm, start, 0)
    lax.fori_loop(0, tm, wait, 0)
    gates = gate_ref[...]
    f = gates[:, 0:1] * buf_ref[0] + gates[:, 1:2] * buf_ref[1]
    y = _layer_norm(alpha * x_ref[...] + f, g_ref[...], b_ref[...])
    o32_ref[...] = y
    o16_ref[...] = y.astype(o16_ref.dtype)


def _gather_mix_ln(pos, ys, gates, x_res, g, b, alpha):
    m, d = x_res.shape
    tm = _tile(m, 256)
    row = lambda i: (i, 0)
    fixed = lambda i: (0, 0)
    return pl.pallas_call(
        functools.partial(_gather_mix_ln_kernel, alpha=alpha),
        grid=(m // tm,),
        in_specs=[pl.BlockSpec((None, TOP_K, tm), lambda i: (i, 0, 0), memory_space=pltpu.SMEM),
                  pl.BlockSpec(memory_space=pl.ANY),
                  pl.BlockSpec((tm, TOP_K), row),
                  pl.BlockSpec((tm, d), row),
                  pl.BlockSpec((1, d), fixed), pl.BlockSpec((1, d), fixed)],
        out_specs=[pl.BlockSpec((tm, d), row), pl.BlockSpec((tm, d), row)],
        out_shape=[jax.ShapeDtypeStruct((m, d), F32), jax.ShapeDtypeStruct((m, d), BF16)],
        scratch_shapes=[pltpu.VMEM((TOP_K, tm, d), F32), pltpu.SemaphoreType.DMA(())],
        compiler_params=_params(("arbitrary",), 32),
        name="gather_mix_ln",
    )(_tile_positions(pos, tm), ys, gates, x_res, g.reshape(1, d), b.reshape(1, d))


def _moe_ffn_ln(x32, xb, w_router, w1, w3, w2, g, b, alpha):
    m, _ = xb.shape
    n_exp = w1.shape[0]
    tm = _tile(m, 512)
    experts, ranks, gates, counts = _router(xb, w_router.T.astype(BF16))

    counts = counts[:, 0]
    padded = (counts + tm - 1) // tm * tm
    ends = jnp.cumsum(padded)
    starts = ends - padded
    expert_ids = jnp.arange(n_exp, dtype=jnp.int32)[:, None, None]
    pos = jnp.sum(jnp.where(experts[None] == expert_ids, starts[:, None, None], 0), axis=0) + ranks
    n_slots = TOP_K * m + n_exp * tm
    tile_start = jnp.arange(n_slots // tm, dtype=jnp.int32) * tm
    tile_valid = (tile_start < ends[-1]).astype(jnp.int32)
    clamped = jnp.minimum(tile_start, jnp.maximum(ends[-1] - tm, 0))
    tile_expert = jnp.sum((ends[None, :] <= clamped[:, None]).astype(jnp.int32), axis=1)
    tile_expert = jnp.minimum(tile_expert, n_exp - 1)
    prev_expert = jnp.concatenate([jnp.full((1,), -1, jnp.int32), tile_expert[:-1]])
    tile_first = (tile_expert != prev_expert).astype(jnp.int32)
    tiles = (tile_expert, tile_valid, tile_first)

    xs = _dispatch(pos, x32, n_slots).astype(BF16)
    h = _expert_up(tiles, xs, w1, w3, tm)
    ys = _expert_down(tiles, h, w2, tm)
    return _gather_mix_ln(pos, ys, gates.T, x32, g, b, alpha)


def kernel(x, p, attn_w_qkv, attn_w_o, pool_w_in, pool_w_group, pool_scale, pool_w_o,
           ln_mix_g, ln_mix_b, ln_ffn_g, ln_ffn_b, ffn_w1, ffn_w3, ffn_w2,
           moe_router, moe_w1, moe_w3, moe_w2, ple_w_proj, ple_w_gate, ple_b_gate):
    batch, seq, d = x.shape
    depth = p.shape[0]
    att_w = N_HEADS * LANES
    assert attn_w_qkv.shape[2] == N_GROUPS * 3 * att_w
    alpha = (2 * depth) ** 0.25
    m = batch * seq
    bf = lambda t: t.astype(BF16)

    x32 = x.reshape(m, d)
    xb = bf(x32)
    cos_tab, sin_tab = _rope_tables(seq)
    for i in range(depth):
        j = i // 2
        if i % 2 == 0:
            xg = jnp.stack([_residue_major(xb, batch, seq, dil) for _, dil in DILATED_GROUPS])
            qkv = _qkv_rope(xg, attn_w_qkv[j], cos_tab, sin_tab, seq, att_w)
            o_all, lse_all = _dilated_attention(qkv, seq, att_w)
            x32, xb = _attn_out(o_all, lse_all, bf(attn_w_o[j]), x32, ln_mix_g[i], ln_mix_b[i],
                                alpha, batch, seq)
            h = _swiglu_up(xb, ffn_w1[j], ffn_w3[j])
            x32, xb = _matmul_res_ln(h, bf(ffn_w2[j]), x32, ln_ffn_g[i], ln_ffn_b[i], alpha)
        else:
            u = _matmul(xb, bf(pool_w_in[j]), F32)
            y = _pool_mix(u, bf(pool_w_group[j]), pool_scale[j].reshape(-1), batch, seq)
            x32, xb = _matmul_res_ln(y, bf(pool_w_o[j]), x32, ln_mix_g[i], ln_mix_b[i], alpha)
            x32, xb = _moe_ffn_ln(x32, xb, moe_router[j], moe_w1[j], moe_w3[j], moe_w2[j],
                                  ln_ffn_g[i], ln_ffn_b[i], alpha)
        x32, xb = _ple(x32, xb, bf(ple_w_gate[i]), ple_b_gate[i], bf(p[i].reshape(m, -1)),
                       bf(ple_w_proj[i]))
    return x32.reshape(batch, seq, d)
```

```python
import functools

import jax
import jax.numpy as jnp
from jax import lax
from jax.experimental import pallas as pl
from jax.experimental.pallas import tpu as pltpu

F32 = jnp.float32
BF16 = jnp.bfloat16

N_HEADS = 16
DILATED_GROUPS = ((128, 1), (512, 4), (2048, 16))
N_GROUPS = len(DILATED_GROUPS)
ATT_BLOCK = 128
ROPE_THETA = 10000.0
POOL_WINDOWS = (2, 4, 8, 16)
POOL_HALO = 16
TOP_K = 2
ROW_COPY_UNROLL = 4
LN_EPS = 1e-5
NEG_INF = -1e30

LANES = 128
MIB = 1024 * 1024


def _tile(dim, pref):
    t = min(dim, pref)
    while dim % t:
        t //= 2
    return t


def _params(semantics, vmem_mib, **kw):
    return pltpu.CompilerParams(dimension_semantics=semantics, vmem_limit_bytes=vmem_mib * MIB, **kw)


def _layer_norm(y, g, b):
    mu = jnp.mean(y, axis=-1, keepdims=True)
    yc = y - mu
    var = jnp.mean(yc * yc, axis=-1, keepdims=True)
    return yc * lax.rsqrt(var + LN_EPS) * g + b


def _residue_major(t, batch, seq, dil):
    d = t.shape[1]
    return t.reshape(batch, seq // dil, dil, d).transpose(0, 2, 1, 3).reshape(batch * seq, d)


def _qkv_rope_kernel(x_ref, w_ref, cos_ref, sin_ref, o_ref, wb_ref):
    @pl.when(pl.program_id(1) == 0)
    def _():
        wb_ref[...] = w_ref[...].astype(wb_ref.dtype)

    acc = jnp.dot(x_ref[...], wb_ref[...], preferred_element_type=F32)
    cos = cos_ref[...]
    sin = sin_ref[...]
    for h in range(acc.shape[1] // LANES):
        t = acc[:, h * LANES:(h + 1) * LANES]
        rot = pltpu.roll(t, LANES // 2, 1)
        o_ref[:, h * LANES:(h + 1) * LANES] = (t * cos + rot * sin).astype(o_ref.dtype)


def _qkv_rope(x_g, w, tables, group, seq, att_w):
    m, k = x_g.shape
    cos_tab, sin_tab = tables
    tm = _tile(seq, 1024)
    tn = _tile(att_w, 1024)
    s_tiles = seq // tm
    per_role = att_w // tn
    per_group = 3 * per_role
    table_spec = pl.BlockSpec((None, tm, LANES), lambda j, i: (j // per_role, i % s_tiles, 0))
    return pl.pallas_call(
        _qkv_rope_kernel,
        grid=(per_group, m // tm),
        in_specs=[pl.BlockSpec((tm, k), lambda j, i: (i, 0)),
                  pl.BlockSpec((k, tn), lambda j, i: (0, group * per_group + j)),
                  table_spec, table_spec],
        out_specs=pl.BlockSpec((tm, tn), lambda j, i: (i, j)),
        out_shape=jax.ShapeDtypeStruct((m, 3 * att_w), BF16),
        scratch_shapes=[pltpu.VMEM((k, tn), BF16)],
        compiler_params=_params(("parallel", "arbitrary"), 56),
        name=f"qkv_rope_g{group}",
    )(x_g, w, cos_tab, sin_tab)


def _rope_tables(seq, dil):
    half = LANES // 2
    inv = ROPE_THETA ** (-jnp.arange(half, dtype=F32) / half)
    row = jnp.arange(seq)
    n_sub = seq // dil
    pos = (row % n_sub) * dil + row // n_sub
    ang = pos.astype(F32)[:, None] * inv[None, :]
    cos, sin = jnp.cos(ang), jnp.sin(ang)
    cos_full = jnp.concatenate([cos, cos], axis=-1)
    sin_signed = jnp.concatenate([-sin, sin], axis=-1)
    return (jnp.stack([cos_full, cos_full, jnp.ones_like(cos_full)]),
            jnp.stack([sin_signed, sin_signed, jnp.zeros_like(cos_full)]))


def _attn_kernel(q_ref, kc_ref, kp_ref, vc_ref, vp_ref, o_ref, lse_ref, *, nb, n_back, scale):
    blk = q_ref.shape[0]
    has_prev = lax.rem(pl.program_id(0), nb) > 0
    qi = lax.broadcasted_iota(jnp.int32, (blk, 2 * blk), 0) + blk
    ki = lax.broadcasted_iota(jnp.int32, (blk, 2 * blk), 1)
    dist = qi - ki
    mask = (dist >= 0) & (dist <= n_back) & (has_prev | (ki >= blk))
    bias = jnp.where(mask, 0.0, NEG_INF)
    lane = lax.broadcasted_iota(jnp.int32, (blk, LANES), 1)
    lse_tile = jnp.zeros((blk, LANES), F32)
    for h in range(q_ref.shape[1] // LANES):
        hs = slice(h * LANES, (h + 1) * LANES)
        k2 = jnp.concatenate([kp_ref[:, hs], kc_ref[:, hs]], axis=0)
        v2 = jnp.concatenate([vp_ref[:, hs], vc_ref[:, hs]], axis=0)
        s = lax.dot_general(q_ref[:, hs], k2, (((1,), (1,)), ((), ())),
                            preferred_element_type=F32) * scale + bias
        m = jnp.max(s, axis=-1, keepdims=True)
        e = jnp.exp(s - m)
        l = jnp.sum(e, axis=-1, keepdims=True)
        p = (e * (1.0 / l)).astype(v2.dtype)
        o_ref[:, hs] = jnp.dot(p, v2, preferred_element_type=F32)
        lse_tile = jnp.where(lane == h, m + jnp.log(l), lse_tile)
    lse_ref[...] = lse_tile


def _dilated_attention(qkv, group, seq, att_w):
    m = qkv.shape[0]
    window, dil = DILATED_GROUPS[group]
    assert seq % (dil * ATT_BLOCK) == 0

    def spec(role, prev):
        return pl.BlockSpec((ATT_BLOCK, att_w),
                            lambda r: (jnp.maximum(r - 1, 0) if prev else r, role))

    return pl.pallas_call(
        functools.partial(_attn_kernel, nb=seq // (dil * ATT_BLOCK), n_back=window // dil,
                          scale=LANES ** -0.5),
        grid=(m // ATT_BLOCK,),
        in_specs=[spec(0, False), spec(1, False), spec(1, True), spec(2, False), spec(2, True)],
        out_specs=[pl.BlockSpec((ATT_BLOCK, att_w), lambda r: (r, 0)),
                   pl.BlockSpec((ATT_BLOCK, LANES), lambda r: (r, 0))],
        out_shape=[jax.ShapeDtypeStruct((m, att_w), F32), jax.ShapeDtypeStruct((m, LANES), F32)],
        compiler_params=_params(("arbitrary",), 32),
        name=f"dilated_attn_g{group}",
    )(qkv, qkv, qkv, qkv, qkv)


def _token_order(ref, scr_ref):
    dil, n, w = ref.shape
    heads = range(w // LANES)
    if dil == 1:
        return [ref[0, :, h * LANES:(h + 1) * LANES] for h in heads]
    for r in range(dil):
        for h in heads:
            scr_ref[h, pl.ds(r, n, stride=dil), :] = ref[r, :, h * LANES:(h + 1) * LANES]
    return [scr_ref[h] for h in heads]


def _attn_out_kernel(*refs, alpha):
    o_refs, l_refs = refs[:N_GROUPS], refs[N_GROUPS:2 * N_GROUPS]
    w_ref, x_ref, g_ref, b_ref, o32_ref, o16_ref = refs[2 * N_GROUPS:2 * N_GROUPS + 6]
    scratch = refs[2 * N_GROUPS + 6:]
    o_scr, l_scr = scratch[:N_GROUPS], scratch[N_GROUPS:]
    outs = [_token_order(o_refs[g], o_scr[g]) for g in range(N_GROUPS)]
    lses = [_token_order(l_refs[g], l_scr[g])[0] for g in range(N_GROUPS)]
    m = functools.reduce(jnp.maximum, lses)
    es = [jnp.exp(l - m) for l in lses]
    inv = 1.0 / functools.reduce(jnp.add, es)
    wts = [e * inv for e in es]
    tm = x_ref.shape[0]
    heads = []
    for h in range(len(outs[0])):
        acc = None
        for g in range(N_GROUPS):
            term = jnp.broadcast_to(wts[g][:, h:h + 1], (tm, LANES)) * outs[g][h]
            acc = term if acc is None else acc + term
        heads.append(acc.astype(w_ref.dtype))
    mixed = jnp.concatenate(heads, axis=1)
    hid = jnp.dot(mixed, w_ref[...], preferred_element_type=F32)
    y = _layer_norm(alpha * x_ref[...] + hid, g_ref[...], b_ref[...])
    o32_ref[...] = y
    o16_ref[...] = y.astype(o16_ref.dtype)


def _attn_out(outs, lses, w_o, x_res, g, b, alpha, batch, seq):
    n_groups = len(outs)
    m, att_w = outs[0].shape
    d = w_o.shape[1]
    dils = [dil for _, dil in DILATED_GROUPS]
    tm = _tile(seq, 256)
    s_tiles = seq // tm
    assert all(tm % (8 * dil) == 0 for dil in dils)

    def group_view(t, g):
        return t.reshape(batch, dils[g], seq // dils[g], t.shape[-1])

    def group_spec(g, width):
        return pl.BlockSpec((None, dils[g], tm // dils[g], width),
                            lambda i: (i // s_tiles, 0, i % s_tiles, 0))

    row = lambda i: (i, 0)
    fixed = lambda i: (0, 0)
    return pl.pallas_call(
        functools.partial(_attn_out_kernel, alpha=alpha),
        grid=(m // tm,),
        in_specs=[group_spec(g, att_w) for g in range(n_groups)]
        + [group_spec(g, LANES) for g in range(n_groups)]
        + [pl.BlockSpec((att_w, d), fixed, pipeline_mode=pl.Buffered(1)),
           pl.BlockSpec((tm, d), row), pl.BlockSpec((1, d), fixed), pl.BlockSpec((1, d), fixed)],
        out_specs=[pl.BlockSpec((tm, d), row), pl.BlockSpec((tm, d), row)],
        out_shape=[jax.ShapeDtypeStruct((m, d), F32), jax.ShapeDtypeStruct((m, d), BF16)],
        scratch_shapes=[pltpu.VMEM((att_w // LANES, tm, LANES), F32) for _ in range(n_groups)]
        + [pltpu.VMEM((1, tm, LANES), F32) for _ in range(n_groups)],
        compiler_params=_params(("parallel",), 56),
        name="attn_out_ln",
    )(*[group_view(outs[g], g) for g in range(n_groups)],
      *[group_view(lses[g], g) for g in range(n_groups)],
      w_o, x_res, g.reshape(1, d), b.reshape(1, d))


def _matmul_kernel(a_ref, w_ref, o_ref):
    o_ref[...] = jnp.dot(a_ref[...], w_ref[...], preferred_element_type=F32).astype(o_ref.dtype)


def _matmul(a, w, out_dtype):
    m, k = a.shape
    n = w.shape[1]
    tm, tn = _tile(m, 1024), _tile(n, 1024)
    return pl.pallas_call(
        _matmul_kernel,
        grid=(m // tm, n // tn),
        in_specs=[pl.BlockSpec((tm, k), lambda i, j: (i, 0)),
                  pl.BlockSpec((k, tn), lambda i, j: (0, j))],
        out_specs=pl.BlockSpec((tm, tn), lambda i, j: (i, j)),
        out_shape=jax.ShapeDtypeStruct((m, n), out_dtype),
        compiler_params=_params(("parallel", "parallel"), 48),
        name="matmul",
    )(a, w)


def _matmul_res_ln_kernel(a_ref, w_ref, x_ref, g_ref, b_ref, o32_ref, o16_ref, *, alpha):
    h = jnp.dot(a_ref[...], w_ref[...], preferred_element_type=F32)
    y = _layer_norm(alpha * x_ref[...] + h, g_ref[...], b_ref[...])
    o32_ref[...] = y
    o16_ref[...] = y.astype(o16_ref.dtype)


def _gated_embedding(y, wg_ref, bg_ref, p_ref, wp_ref):
    z = jnp.dot(y.astype(wg_ref.dtype), wg_ref[...], preferred_element_type=F32) + bg_ref[...]
    e = jnp.dot(p_ref[...], wp_ref[...], preferred_element_type=F32)
    return y + jax.nn.sigmoid(z) * e


def _ple_specs(tm, d, kp, row, fixed):
    return [pl.BlockSpec((d, d), fixed, pipeline_mode=pl.Buffered(1)),
            pl.BlockSpec((1, d), fixed),
            pl.BlockSpec((tm, kp), row),
            pl.BlockSpec((kp, d), fixed, pipeline_mode=pl.Buffered(1))]


def _matmul_res_ln_ple_kernel(a_ref, w_ref, x_ref, g_ref, b_ref, wg_ref, bg_ref, p_ref, wp_ref,
                              o32_ref, o16_ref, *, alpha):
    h = jnp.dot(a_ref[...], w_ref[...], preferred_element_type=F32)
    y = _layer_norm(alpha * x_ref[...] + h, g_ref[...], b_ref[...])
    y = _gated_embedding(y, wg_ref, bg_ref, p_ref, wp_ref)
    o32_ref[...] = y
    o16_ref[...] = y.astype(o16_ref.dtype)


def _matmul_res_ln_ple(a, w, x_res, g, b, alpha, w_gate, b_gate, pb, w_proj):
    m, k = a.shape
    n = w.shape[1]
    kp = pb.shape[1]
    tm = _tile(m, 256)
    row = lambda i: (i, 0)
    fixed = lambda i: (0, 0)
    return pl.pallas_call(
        functools.partial(_matmul_res_ln_ple_kernel, alpha=alpha),
        grid=(m // tm,),
        in_specs=[pl.BlockSpec((tm, k), row),
                  pl.BlockSpec((k, n), fixed, pipeline_mode=pl.Buffered(1)),
                  pl.BlockSpec((tm, n), row), pl.BlockSpec((1, n), fixed), pl.BlockSpec((1, n), fixed)]
        + _ple_specs(tm, n, kp, row, fixed),
        out_specs=[pl.BlockSpec((tm, n), row), pl.BlockSpec((tm, n), row)],
        out_shape=[jax.ShapeDtypeStruct((m, n), F32), jax.ShapeDtypeStruct((m, n), BF16)],
        compiler_params=_params(("parallel",), 60),
        name="matmul_res_ln_ple",
    )(a, w, x_res, g.reshape(1, n), b.reshape(1, n), w_gate, b_gate.reshape(1, n), pb, w_proj)


def _matmul_res_ln(a, w, x_res, g, b, alpha):
    m, k = a.shape
    n = w.shape[1]
    tm = _tile(m, 256)
    row = lambda i: (i, 0)
    fixed = lambda i: (0, 0)
    return pl.pallas_call(
        functools.partial(_matmul_res_ln_kernel, alpha=alpha),
        grid=(m // tm,),
        in_specs=[pl.BlockSpec((tm, k), row),
                  pl.BlockSpec((k, n), fixed, pipeline_mode=pl.Buffered(1)),
                  pl.BlockSpec((tm, n), row), pl.BlockSpec((1, n), fixed), pl.BlockSpec((1, n), fixed)],
        out_specs=[pl.BlockSpec((tm, n), row), pl.BlockSpec((tm, n), row)],
        out_shape=[jax.ShapeDtypeStruct((m, n), F32), jax.ShapeDtypeStruct((m, n), BF16)],
        compiler_params=_params(("parallel",), 60),
        name="matmul_res_ln",
    )(a, w, x_res, g.reshape(1, n), b.reshape(1, n))


def _swiglu_up_kernel(x_ref, w1_ref, w3_ref, o_ref, w1b_ref, w3b_ref):
    @pl.when(pl.program_id(1) == 0)
    def _():
        w1b_ref[...] = w1_ref[...].astype(w1b_ref.dtype)
        w3b_ref[...] = w3_ref[...].astype(w3b_ref.dtype)

    x = x_ref[...]
    a = jnp.dot(x, w1b_ref[...], preferred_element_type=F32)
    b = jnp.dot(x, w3b_ref[...], preferred_element_type=F32)
    o_ref[...] = (a * jax.nn.sigmoid(a) * b).astype(o_ref.dtype)


def _swiglu_up(xb, w1, w3):
    m, k = xb.shape
    n = w1.shape[1]
    tm, tn = _tile(m, 1024), _tile(n, 512)
    return pl.pallas_call(
        _swiglu_up_kernel,
        grid=(n // tn, m // tm),
        in_specs=[pl.BlockSpec((tm, k), lambda j, i: (i, 0)),
                  pl.BlockSpec((k, tn), lambda j, i: (0, j)),
                  pl.BlockSpec((k, tn), lambda j, i: (0, j))],
        out_specs=pl.BlockSpec((tm, tn), lambda j, i: (i, j)),
        out_shape=jax.ShapeDtypeStruct((m, n), BF16),
        scratch_shapes=[pltpu.VMEM((k, tn), BF16), pltpu.VMEM((k, tn), BF16)],
        compiler_params=_params(("parallel", "arbitrary"), 56),
        name="swiglu_up",
    )(xb, w1, w3)


def _pool_kernel(uc_ref, up_ref, wg_ref, sc_ref, y_ref, buf_ref):
    s = pl.program_id(1)
    tm = uc_ref.shape[0]
    pd = wg_ref.shape[1]
    buf_ref[POOL_HALO:POOL_HALO + tm, :] = uc_ref[...]
    buf_ref[0:POOL_HALO, :] = jnp.where(s > 0, up_ref[...], 0.0)
    t = s * tm + lax.broadcasted_iota(jnp.int32, (tm, 1), 0)
    for g, w in enumerate(POOL_WINDOWS):
        cols = slice(g * pd, (g + 1) * pd)
        u = buf_ref[POOL_HALO:POOL_HALO + tm, cols]
        tot = u
        for i in range(1, w):
            tot = tot + buf_ref[POOL_HALO - i:POOL_HALO - i + tm, cols]
        cnt = jnp.minimum(t + 1, w).astype(F32)
        mixed = (tot / cnt - u).astype(wg_ref.dtype)
        y = jnp.dot(mixed, wg_ref[g], preferred_element_type=F32) * sc_ref[:, cols]
        y_ref[:, cols] = y.astype(y_ref.dtype)


def _pool_mix(u, w_group, scale, batch, seq):
    d = u.shape[1]
    tm = _tile(seq, 512)
    assert tm % POOL_HALO == 0 and max(POOL_WINDOWS) <= POOL_HALO
    halo_blocks = tm // POOL_HALO
    u3 = u.reshape(batch, seq, d)
    y = pl.pallas_call(
        _pool_kernel,
        grid=(batch, seq // tm),
        in_specs=[pl.BlockSpec((None, tm, d), lambda b, s: (b, s, 0)),
                  pl.BlockSpec((None, POOL_HALO, d),
                               lambda b, s: (b, jnp.maximum(s * halo_blocks - 1, 0), 0)),
                  pl.BlockSpec(w_group.shape, lambda b, s: (0, 0, 0)),
                  pl.BlockSpec((1, d), lambda b, s: (0, 0))],
        out_specs=pl.BlockSpec((None, tm, d), lambda b, s: (b, s, 0)),
        out_shape=jax.ShapeDtypeStruct((batch, seq, d), BF16),
        scratch_shapes=[pltpu.VMEM((POOL_HALO + tm, d), F32)],
        compiler_params=_params(("parallel", "arbitrary"), 48),
        name="pool_mix",
    )(u3, u3, w_group, scale.reshape(1, d))
    return y.reshape(batch * seq, d)


def _router_kernel(x_ref, wr_ref, tri_ref, e_ref, rank_ref, gate_ref, cnt_ref, run_ref):
    @pl.when(pl.program_id(0) == 0)
    def _():
        run_ref[...] = jnp.zeros_like(run_ref)

    logits = lax.dot_general(wr_ref[...], x_ref[...], (((1,), (1,)), ((), ())),
                             preferred_element_type=F32)
    n_exp = logits.shape[0]
    eidx = lax.broadcasted_iota(jnp.int32, logits.shape, 0)
    m1 = jnp.max(logits, axis=0, keepdims=True)
    i1 = jnp.min(jnp.where(logits == m1, eidx, n_exp), axis=0, keepdims=True)
    rest = jnp.where(eidx == i1, -jnp.inf, logits)
    m2 = jnp.max(rest, axis=0, keepdims=True)
    i2 = jnp.min(jnp.where(rest == m2, eidx, n_exp), axis=0, keepdims=True)
    ex = jnp.exp(m2 - m1)
    g1 = 1.0 / (1.0 + ex)
    g2 = ex / (1.0 + ex)

    oh1 = (eidx == i1).astype(F32)
    oh2 = (eidx == i2).astype(F32)
    tri = tri_ref[...]
    c1 = jnp.dot(oh1.astype(tri.dtype), tri, preferred_element_type=F32)
    c2 = jnp.dot(oh2.astype(tri.dtype), tri, preferred_element_type=F32)
    tot1 = jnp.sum(oh1, axis=1, keepdims=True)
    tot2 = jnp.sum(oh2, axis=1, keepdims=True)
    run = run_ref[:, 0:1]
    r1 = jnp.sum(oh1 * (run + c1), axis=0, keepdims=True)
    r2 = jnp.sum(oh2 * (run + tot1 + c2), axis=0, keepdims=True)
    run = run + tot1 + tot2
    run_ref[...] = jnp.broadcast_to(run, run_ref.shape)

    e_ref[...] = jnp.concatenate([i1, i2], axis=0)
    rank_ref[...] = jnp.concatenate([r1, r2], axis=0).astype(jnp.int32)
    gate_ref[...] = jnp.concatenate([g1, g2], axis=0)
    cnt_ref[...] = jnp.broadcast_to(run, cnt_ref.shape).astype(jnp.int32)


def _router(xb, w_router_t):
    m, k = xb.shape
    n_exp = w_router_t.shape[0]
    tm = _tile(m, 512)
    tri = (jnp.arange(tm)[:, None] < jnp.arange(tm)[None, :]).astype(BF16)
    pair = lambda dt: jax.ShapeDtypeStruct((TOP_K, m), dt)
    pair_spec = pl.BlockSpec((TOP_K, tm), lambda i: (0, i))
    return pl.pallas_call(
        _router_kernel,
        grid=(m // tm,),
        in_specs=[pl.BlockSpec((tm, k), lambda i: (i, 0)),
                  pl.BlockSpec((n_exp, k), lambda i: (0, 0)),
                  pl.BlockSpec((tm, tm), lambda i: (0, 0))],
        out_specs=[pair_spec, pair_spec, pair_spec, pl.BlockSpec((n_exp, LANES), lambda i: (0, 0))],
        out_shape=[pair(jnp.int32), pair(jnp.int32), pair(F32),
                   jax.ShapeDtypeStruct((n_exp, LANES), jnp.int32)],
        scratch_shapes=[pltpu.VMEM((n_exp, LANES), F32)],
        compiler_params=_params(("arbitrary",), 32),
        name="router",
    )(xb, w_router_t, tri)


def _tile_positions(pos, tm):
    return pos.reshape(TOP_K, -1, tm).transpose(1, 0, 2)


def _dispatch_kernel(pos_ref, x_ref, init_ref, xs_ref, sem):
    del init_ref
    tm = x_ref.shape[0]

    def row_copy(r, k):
        return pltpu.make_async_copy(x_ref.at[pl.ds(r, 1), :],
                                     xs_ref.at[pl.ds(pos_ref[k, r], 1), :], sem)

    def start(r, carry):
        for k in range(TOP_K):
            row_copy(r, k).start()
        return carry

    def wait(r, carry):
        for k in range(TOP_K):
            row_copy(r, k).wait()
        return carry

    lax.fori_loop(0, tm, start, 0, unroll=ROW_COPY_UNROLL)
    lax.fori_loop(0, tm, wait, 0, unroll=ROW_COPY_UNROLL)


def _dispatch(pos, x32, n_slots):
    m, d = x32.shape
    tm = _tile(m, 256)
    return pl.pallas_call(
        _dispatch_kernel,
        grid=(m // tm,),
        in_specs=[pl.BlockSpec((None, TOP_K, tm), lambda i: (i, 0, 0), memory_space=pltpu.SMEM),
                  pl.BlockSpec((tm, d), lambda i: (i, 0)),
                  pl.BlockSpec(memory_space=pl.ANY)],
        out_specs=pl.BlockSpec(memory_space=pl.ANY),
        out_shape=jax.ShapeDtypeStruct((n_slots, d), x32.dtype),
        scratch_shapes=[pltpu.SemaphoreType.DMA(())],
        input_output_aliases={2: 0},
        compiler_params=_params(("arbitrary",), 32, disable_bounds_checks=True),
        name="dispatch",
    )(_tile_positions(pos, tm), x32, jnp.zeros((n_slots, d), x32.dtype))


def _cast_expert_weights(first_ref, pairs):
    @pl.when(first_ref[pl.program_id(1)] > 0)
    def _():
        for src, dst in pairs:
            dst[...] = src[...].astype(dst.dtype)


def _expert_up_kernel(te_ref, tv_ref, tf_ref, xs_ref, w1_ref, w3_ref, h_ref, w1b_ref, w3b_ref):
    del te_ref
    _cast_expert_weights(tf_ref, [(w1_ref, w1b_ref), (w3_ref, w3b_ref)])

    @pl.when(tv_ref[pl.program_id(1)] > 0)
    def _():
        x = xs_ref[...]
        a = jnp.dot(x, w1b_ref[...], preferred_element_type=F32)
        b = jnp.dot(x, w3b_ref[...], preferred_element_type=F32)
        h_ref[...] = (a * jax.nn.sigmoid(a) * b).astype(h_ref.dtype)

    @pl.when(tv_ref[pl.program_id(1)] == 0)
    def _():
        h_ref[...] = jnp.zeros_like(h_ref)


def _expert_up(tiles, xs, w1, w3, tm):
    n_slots, k = xs.shape
    n = w1.shape[2]
    tn = _tile(n, 512)
    grid_spec = pltpu.PrefetchScalarGridSpec(
        num_scalar_prefetch=3,
        grid=(n // tn, n_slots // tm),
        in_specs=[pl.BlockSpec((tm, k), lambda c, i, te, tv, tf: (i, 0)),
                  pl.BlockSpec((None, k, tn), lambda c, i, te, tv, tf: (te[i], 0, c)),
                  pl.BlockSpec((None, k, tn), lambda c, i, te, tv, tf: (te[i], 0, c))],
        out_specs=pl.BlockSpec((tm, tn), lambda c, i, te, tv, tf: (i, c)),
        scratch_shapes=[pltpu.VMEM((k, tn), BF16), pltpu.VMEM((k, tn), BF16)],
    )
    return pl.pallas_call(
        _expert_up_kernel,
        grid_spec=grid_spec,
        out_shape=jax.ShapeDtypeStruct((n_slots, n), BF16),
        compiler_params=_params(("parallel", "arbitrary"), 48),
        name="expert_up",
    )(*tiles, xs, w1, w3)


def _expert_down_kernel(te_ref, tv_ref, tf_ref, h_ref, w2_ref, y_ref, w2b_ref):
    del te_ref
    _cast_expert_weights(tf_ref, [(w2_ref, w2b_ref)])

    @pl.when(tv_ref[pl.program_id(1)] > 0)
    def _():
        y_ref[...] = jnp.dot(h_ref[...], w2b_ref[...], preferred_element_type=F32)

    @pl.when(tv_ref[pl.program_id(1)] == 0)
    def _():
        y_ref[...] = jnp.zeros_like(y_ref)


def _expert_down(tiles, h, w2, tm):
    n_slots, k = h.shape
    n = w2.shape[2]
    tn = _tile(n, 512)
    grid_spec = pltpu.PrefetchScalarGridSpec(
        num_scalar_prefetch=3,
        grid=(n // tn, n_slots // tm),
        in_specs=[pl.BlockSpec((tm, k), lambda c, i, te, tv, tf: (i, 0)),
                  pl.BlockSpec((None, k, tn), lambda c, i, te, tv, tf: (te[i], 0, c),
                               pipeline_mode=pl.Buffered(1))],
        out_specs=pl.BlockSpec((tm, tn), lambda c, i, te, tv, tf: (i, c)),
        scratch_shapes=[pltpu.VMEM((k, tn), BF16)],
    )
    return pl.pallas_call(
        _expert_down_kernel,
        grid_spec=grid_spec,
        out_shape=jax.ShapeDtypeStruct((n_slots, n), F32),
        compiler_params=_params(("parallel", "arbitrary"), 56),
        name="expert_down",
    )(*tiles, h, w2)


def _gather_mix_ln_kernel(pos_ref, ys_ref, gate_ref, x_ref, g_ref, b_ref, wg_ref, bg_ref, p_ref,
                          wp_ref, o32_ref, o16_ref, buf_ref, sem, *, alpha):
    tm = x_ref.shape[0]

    def row_copy(r, k):
        return pltpu.make_async_copy(ys_ref.at[pl.ds(pos_ref[k, r], 1), :],
                                     buf_ref.at[k, pl.ds(r, 1), :], sem)

    def start(r, carry):
        for k in range(TOP_K):
            row_copy(r, k).start()
        return carry

    def wait(r, carry):
        for k in range(TOP_K):
            row_copy(r, k).wait()
        return carry

    lax.fori_loop(0, tm, start, 0, unroll=ROW_COPY_UNROLL)
    lax.fori_loop(0, tm, wait, 0, unroll=ROW_COPY_UNROLL)
    gates = gate_ref[...]
    f = gates[:, 0:1] * buf_ref[0] + gates[:, 1:2] * buf_ref[1]
    y = _layer_norm(alpha * x_ref[...] + f, g_ref[...], b_ref[...])
    y = _gated_embedding(y, wg_ref, bg_ref, p_ref, wp_ref)
    o32_ref[...] = y
    o16_ref[...] = y.astype(o16_ref.dtype)


def _gather_mix_ln(pos, ys, gates, x_res, g, b, alpha, w_gate, b_gate, pb, w_proj):
    m, d = x_res.shape
    kp = pb.shape[1]
    tm = _tile(m, 256)
    row = lambda i: (i, 0)
    fixed = lambda i: (0, 0)
    return pl.pallas_call(
        functools.partial(_gather_mix_ln_kernel, alpha=alpha),
        grid=(m // tm,),
        in_specs=[pl.BlockSpec((None, TOP_K, tm), lambda i: (i, 0, 0), memory_space=pltpu.SMEM),
                  pl.BlockSpec(memory_space=pl.ANY),
                  pl.BlockSpec((tm, TOP_K), row),
                  pl.BlockSpec((tm, d), row),
                  pl.BlockSpec((1, d), fixed), pl.BlockSpec((1, d), fixed)]
        + _ple_specs(tm, d, kp, row, fixed),
        out_specs=[pl.BlockSpec((tm, d), row), pl.BlockSpec((tm, d), row)],
        out_shape=[jax.ShapeDtypeStruct((m, d), F32), jax.ShapeDtypeStruct((m, d), BF16)],
        scratch_shapes=[pltpu.VMEM((TOP_K, tm, d), F32), pltpu.SemaphoreType.DMA(())],
        compiler_params=_params(("arbitrary",), 48, disable_bounds_checks=True),
        name="gather_mix_ln",
    )(_tile_positions(pos, tm), ys, gates, x_res, g.reshape(1, d), b.reshape(1, d),
      w_gate, b_gate.reshape(1, d), pb, w_proj)


def _moe_ffn_ln(x32, xb, w_router, w1, w3, w2, g, b, alpha, ple):
    m, _ = xb.shape
    n_exp = w1.shape[0]
    tm = _tile(m, 1024)
    experts, ranks, gates, counts = _router(xb, w_router.T.astype(BF16))

    counts = counts[:, 0]
    padded = (counts + tm - 1) // tm * tm
    ends = jnp.cumsum(padded)
    starts = ends - padded
    expert_ids = jnp.arange(n_exp, dtype=jnp.int32)[:, None, None]
    pos = jnp.sum(jnp.where(experts[None] == expert_ids, starts[:, None, None], 0), axis=0) + ranks
    n_slots = TOP_K * m + n_exp * tm
    tile_start = jnp.arange(n_slots // tm, dtype=jnp.int32) * tm
    tile_valid = (tile_start < ends[-1]).astype(jnp.int32)
    clamped = jnp.minimum(tile_start, jnp.maximum(ends[-1] - tm, 0))
    tile_expert = jnp.sum((ends[None, :] <= clamped[:, None]).astype(jnp.int32), axis=1)
    tile_expert = jnp.minimum(tile_expert, n_exp - 1)
    prev_expert = jnp.concatenate([jnp.full((1,), -1, jnp.int32), tile_expert[:-1]])
    tile_first = (tile_expert != prev_expert).astype(jnp.int32)
    tiles = (tile_expert, tile_valid, tile_first)

    xs = _dispatch(pos, x32, n_slots).astype(BF16)
    h = _expert_up(tiles, xs, w1, w3, tm)
    ys = _expert_down(tiles, h, w2, tm)
    return _gather_mix_ln(pos, ys, gates.T, x32, g, b, alpha, *ple)


def kernel(x, p, attn_w_qkv, attn_w_o, pool_w_in, pool_w_group, pool_scale, pool_w_o,
           ln_mix_g, ln_mix_b, ln_ffn_g, ln_ffn_b, ffn_w1, ffn_w3, ffn_w2,
           moe_router, moe_w1, moe_w3, moe_w2, ple_w_proj, ple_w_gate, ple_b_gate):
    batch, seq, d = x.shape
    depth = p.shape[0]
    att_w = N_HEADS * LANES
    assert attn_w_qkv.shape[2] == N_GROUPS * 3 * att_w
    alpha = (2 * depth) ** 0.25
    m = batch * seq
    bf = lambda t: t.astype(BF16)

    x32 = x.reshape(m, d)
    xb = bf(x32)
    for i in range(depth):
        j = i // 2
        ple = (bf(ple_w_gate[i]), ple_b_gate[i], bf(p[i].reshape(m, -1)), bf(ple_w_proj[i]))
        if i % 2 == 0:
            outs, lses = [], []
            for grp, (_, dil) in enumerate(DILATED_GROUPS):
                x_g = _residue_major(xb, batch, seq, dil) if dil > 1 else xb
                qkv = _qkv_rope(x_g, attn_w_qkv[j], _rope_tables(seq, dil), grp, seq, att_w)
                o_g, lse_g = _dilated_attention(qkv, grp, seq, att_w)
                outs.append(o_g)
                lses.append(lse_g)
            x32, xb = _attn_out(outs, lses, bf(attn_w_o[j]), x32, ln_mix_g[i], ln_mix_b[i],
                                alpha, batch, seq)
            h = _swiglu_up(xb, ffn_w1[j], ffn_w3[j])
            x32, xb = _matmul_res_ln_ple(h, bf(ffn_w2[j]), x32, ln_ffn_g[i], ln_ffn_b[i], alpha, *ple)
        else:
            u = _matmul(xb, bf(pool_w_in[j]), F32)
            y = _pool_mix(u, bf(pool_w_group[j]), pool_scale[j].reshape(-1), batch, seq)
            x32, xb = _matmul_res_ln(y, bf(pool_w_o[j]), x32, ln_mix_g[i], ln_mix_b[i], alpha)
            x32, xb = _moe_ffn_ln(x32, xb, moe_router[j], moe_w1[j], moe_w3[j], moe_w2[j],
                                  ln_ffn_g[i], ln_ffn_b[i], alpha, ple)
    return x32.reshape(batch, seq, d)
```

```python
import functools

import jax
import jax.numpy as jnp
from jax import lax
from jax.experimental import pallas as pl
from jax.experimental.pallas import tpu as pltpu

F32 = jnp.float32
BF16 = jnp.bfloat16

N_HEADS = 16
DILATED_GROUPS = ((128, 1), (512, 4), (2048, 16))
N_GROUPS = len(DILATED_GROUPS)
ATT_BLOCK = 128
ROPE_THETA = 10000.0
POOL_WINDOWS = (2, 4, 8, 16)
POOL_HALO = 16
TOP_K = 2
ROW_COPY_UNROLL = 4
LN_EPS = 1e-5
NEG_INF = -1e30

LANES = 128
SUBLANES = 8
MIB = 1024 * 1024


def _tile(dim, pref):
    t = min(dim, pref)
    while dim % t:
        t //= 2
    return t


def _params(semantics, vmem_mib, **kw):
    return pltpu.CompilerParams(dimension_semantics=semantics, vmem_limit_bytes=vmem_mib * MIB, **kw)


def _layer_norm(y, g, b):
    mu = jnp.mean(y, axis=-1, keepdims=True)
    yc = y - mu
    var = jnp.mean(yc * yc, axis=-1, keepdims=True)
    return yc * lax.rsqrt(var + LN_EPS) * g + b


def _pack_bf16_pairs(y):
    half = y.shape[1] // 2
    bits = lambda t: lax.bitcast_convert_type(t.astype(BF16).astype(F32), jnp.uint32)
    return (bits(y[:, :half]) >> 16) | bits(y[:, half:])


def _unpack_bf16_pairs(words):
    lo = lax.bitcast_convert_type(words << 16, F32)
    hi = lax.bitcast_convert_type(words & jnp.uint32(0xFFFF0000), F32)
    return jnp.concatenate([lo, hi], axis=1).astype(BF16)


def _residue_major(t, batch, seq, dil):
    d = t.shape[1]
    return t.reshape(batch, seq // dil, dil, d).transpose(0, 2, 1, 3).reshape(batch * seq, d)


def _qkv_rope_kernel(x_ref, w_ref, cos_ref, sin_ref, o_ref, wb_ref):
    @pl.when(pl.program_id(1) == 0)
    def _():
        wb_ref[...] = w_ref[...].astype(wb_ref.dtype)

    acc = jnp.dot(x_ref[...], wb_ref[...], preferred_element_type=F32)
    cos = cos_ref[...]
    sin = sin_ref[...]
    for h in range(acc.shape[1] // LANES):
        t = acc[:, h * LANES:(h + 1) * LANES]
        rot = pltpu.roll(t, LANES // 2, 1)
        o_ref[:, h * LANES:(h + 1) * LANES] = (t * cos + rot * sin).astype(o_ref.dtype)


def _qkv_rope(x_g, w, tables, group, seq, att_w):
    m, k = x_g.shape
    cos_tab, sin_tab = tables
    tm = _tile(seq, 1024)
    tn = _tile(att_w, 1024)
    s_tiles = seq // tm
    per_role = att_w // tn
    per_group = 3 * per_role
    table_spec = pl.BlockSpec((None, tm, LANES), lambda j, i: (j // per_role, i % s_tiles, 0))
    return pl.pallas_call(
        _qkv_rope_kernel,
        grid=(per_group, m // tm),
        in_specs=[pl.BlockSpec((tm, k), lambda j, i: (i, 0)),
                  pl.BlockSpec((k, tn), lambda j, i: (0, group * per_group + j)),
                  table_spec, table_spec],
        out_specs=pl.BlockSpec((tm, tn), lambda j, i: (i, j)),
        out_shape=jax.ShapeDtypeStruct((m, 3 * att_w), BF16),
        scratch_shapes=[pltpu.VMEM((k, tn), BF16)],
        compiler_params=_params(("parallel", "arbitrary"), 56),
        name=f"qkv_rope_g{group}",
    )(x_g, w, cos_tab, sin_tab)


def _rope_tables(seq, dil):
    half = LANES // 2
    inv = ROPE_THETA ** (-jnp.arange(half, dtype=F32) / half)
    row = jnp.arange(seq)
    n_sub = seq // dil
    pos = (row % n_sub) * dil + row // n_sub
    ang = pos.astype(F32)[:, None] * inv[None, :]
    cos, sin = jnp.cos(ang), jnp.sin(ang)
    cos_full = jnp.concatenate([cos, cos], axis=-1)
    sin_signed = jnp.concatenate([-sin, sin], axis=-1)
    return (jnp.stack([cos_full, cos_full, jnp.ones_like(cos_full)]),
            jnp.stack([sin_signed, sin_signed, jnp.zeros_like(cos_full)]))


def _attn_kernel(q_ref, kc_ref, kp_ref, vc_ref, vp_ref, o_ref, lse_ref, *, nb, n_back, scale):
    blk = q_ref.shape[0]
    has_prev = lax.rem(pl.program_id(0), nb) > 0
    qi = lax.broadcasted_iota(jnp.int32, (blk, 2 * blk), 0) + blk
    ki = lax.broadcasted_iota(jnp.int32, (blk, 2 * blk), 1)
    dist = qi - ki
    mask = (dist >= 0) & (dist <= n_back) & (has_prev | (ki >= blk))
    bias = jnp.where(mask, 0.0, NEG_INF)
    lane = lax.broadcasted_iota(jnp.int32, (blk, LANES), 1)
    lse_tile = jnp.zeros((blk, LANES), F32)
    for h in range(q_ref.shape[1] // LANES):
        hs = slice(h * LANES, (h + 1) * LANES)
        k2 = jnp.concatenate([kp_ref[:, hs], kc_ref[:, hs]], axis=0)
        v2 = jnp.concatenate([vp_ref[:, hs], vc_ref[:, hs]], axis=0)
        s = lax.dot_general(q_ref[:, hs], k2, (((1,), (1,)), ((), ())),
                            preferred_element_type=F32) * scale + bias
        m = jnp.max(s, axis=-1, keepdims=True)
        e = jnp.exp(s - m)
        l = jnp.sum(e, axis=-1, keepdims=True)
        p = (e * (1.0 / l)).astype(v2.dtype)
        o_ref[:, hs] = jnp.dot(p, v2, preferred_element_type=F32)
        lse_tile = jnp.where(lane == h, m + jnp.log(l), lse_tile)
    lse_ref[...] = lse_tile


def _dilated_attention(qkv, group, seq, att_w):
    m = qkv.shape[0]
    window, dil = DILATED_GROUPS[group]
    assert seq % (dil * ATT_BLOCK) == 0

    def spec(role, prev):
        return pl.BlockSpec((ATT_BLOCK, att_w),
                            lambda r: (jnp.maximum(r - 1, 0) if prev else r, role))

    return pl.pallas_call(
        functools.partial(_attn_kernel, nb=seq // (dil * ATT_BLOCK), n_back=window // dil,
                          scale=LANES ** -0.5),
        grid=(m // ATT_BLOCK,),
        in_specs=[spec(0, False), spec(1, False), spec(1, True), spec(2, False), spec(2, True)],
        out_specs=[pl.BlockSpec((ATT_BLOCK, att_w), lambda r: (r, 0)),
                   pl.BlockSpec((ATT_BLOCK, LANES), lambda r: (r, 0))],
        out_shape=[jax.ShapeDtypeStruct((m, att_w), F32), jax.ShapeDtypeStruct((m, LANES), F32)],
        compiler_params=_params(("arbitrary",), 32),
        name=f"dilated_attn_g{group}",
    )(qkv, qkv, qkv, qkv, qkv)


def _token_order(ref, scr_ref):
    dil, n, w = ref.shape
    heads = range(w // LANES)
    if dil == 1:
        return [ref[0, :, h * LANES:(h + 1) * LANES] for h in heads]
    for r in range(dil):
        for h in heads:
            scr_ref[h, pl.ds(r, n, stride=dil), :] = ref[r, :, h * LANES:(h + 1) * LANES]
    return [scr_ref[h] for h in heads]


def _attn_out_kernel(*refs, alpha):
    o_refs, l_refs = refs[:N_GROUPS], refs[N_GROUPS:2 * N_GROUPS]
    w_ref, x_ref, g_ref, b_ref, o32_ref, o16_ref = refs[2 * N_GROUPS:2 * N_GROUPS + 6]
    scratch = refs[2 * N_GROUPS + 6:]
    o_scr, l_scr = scratch[:N_GROUPS], scratch[N_GROUPS:]
    outs = [_token_order(o_refs[g], o_scr[g]) for g in range(N_GROUPS)]
    lses = [_token_order(l_refs[g], l_scr[g])[0] for g in range(N_GROUPS)]
    m = functools.reduce(jnp.maximum, lses)
    es = [jnp.exp(l - m) for l in lses]
    inv = 1.0 / functools.reduce(jnp.add, es)
    wts = [e * inv for e in es]
    tm = x_ref.shape[0]
    heads = []
    for h in range(len(outs[0])):
        acc = None
        for g in range(N_GROUPS):
            term = jnp.broadcast_to(wts[g][:, h:h + 1], (tm, LANES)) * outs[g][h]
            acc = term if acc is None else acc + term
        heads.append(acc.astype(w_ref.dtype))
    mixed = jnp.concatenate(heads, axis=1)
    hid = jnp.dot(mixed, w_ref[...], preferred_element_type=F32)
    y = _layer_norm(alpha * x_ref[...] + hid, g_ref[...], b_ref[...])
    o32_ref[...] = y
    o16_ref[...] = y.astype(o16_ref.dtype)


def _attn_out(outs, lses, w_o, x_res, g, b, alpha, batch, seq):
    n_groups = len(outs)
    m, att_w = outs[0].shape
    d = w_o.shape[1]
    dils = [dil for _, dil in DILATED_GROUPS]
    tm = _tile(seq, 256)
    s_tiles = seq // tm
    assert all(tm % (8 * dil) == 0 for dil in dils)

    def group_view(t, g):
        return t.reshape(batch, dils[g], seq // dils[g], t.shape[-1])

    def group_spec(g, width):
        return pl.BlockSpec((None, dils[g], tm // dils[g], width),
                            lambda i: (i // s_tiles, 0, i % s_tiles, 0))

    row = lambda i: (i, 0)
    fixed = lambda i: (0, 0)
    return pl.pallas_call(
        functools.partial(_attn_out_kernel, alpha=alpha),
        grid=(m // tm,),
        in_specs=[group_spec(g, att_w) for g in range(n_groups)]
        + [group_spec(g, LANES) for g in range(n_groups)]
        + [pl.BlockSpec((att_w, d), fixed, pipeline_mode=pl.Buffered(1)),
           pl.BlockSpec((tm, d), row), pl.BlockSpec((1, d), fixed), pl.BlockSpec((1, d), fixed)],
        out_specs=[pl.BlockSpec((tm, d), row), pl.BlockSpec((tm, d), row)],
        out_shape=[jax.ShapeDtypeStruct((m, d), F32), jax.ShapeDtypeStruct((m, d), BF16)],
        scratch_shapes=[pltpu.VMEM((att_w // LANES, tm, LANES), F32) for _ in range(n_groups)]
        + [pltpu.VMEM((1, tm, LANES), F32) for _ in range(n_groups)],
        compiler_params=_params(("parallel",), 56),
        name="attn_out_ln",
    )(*[group_view(outs[g], g) for g in range(n_groups)],
      *[group_view(lses[g], g) for g in range(n_groups)],
      w_o, x_res, g.reshape(1, d), b.reshape(1, d))


def _matmul_kernel(a_ref, w_ref, o_ref):
    o_ref[...] = jnp.dot(a_ref[...], w_ref[...], preferred_element_type=F32).astype(o_ref.dtype)


def _matmul(a, w, out_dtype):
    m, k = a.shape
    n = w.shape[1]
    tm, tn = _tile(m, 1024), _tile(n, 1024)
    return pl.pallas_call(
        _matmul_kernel,
        grid=(m // tm, n // tn),
        in_specs=[pl.BlockSpec((tm, k), lambda i, j: (i, 0)),
                  pl.BlockSpec((k, tn), lambda i, j: (0, j))],
        out_specs=pl.BlockSpec((tm, tn), lambda i, j: (i, j)),
        out_shape=jax.ShapeDtypeStruct((m, n), out_dtype),
        compiler_params=_params(("parallel", "parallel"), 48),
        name="matmul",
    )(a, w)


def _matmul_res_ln_kernel(a_ref, w_ref, x_ref, g_ref, b_ref, o32_ref, o16_ref, opk_ref, *, alpha):
    h = jnp.dot(a_ref[...], w_ref[...], preferred_element_type=F32)
    y = _layer_norm(alpha * x_ref[...] + h, g_ref[...], b_ref[...])
    o32_ref[...] = y
    o16_ref[...] = y.astype(o16_ref.dtype)
    opk_ref[...] = _pack_bf16_pairs(y)


def _gated_embedding(y, wg_ref, bg_ref, p_ref, wp_ref):
    z = jnp.dot(y.astype(wg_ref.dtype), wg_ref[...], preferred_element_type=F32) + bg_ref[...]
    e = jnp.dot(p_ref[...], wp_ref[...], preferred_element_type=F32)
    return y + jax.nn.sigmoid(z) * e


def _ple_specs(tm, d, kp, row, fixed):
    return [pl.BlockSpec((d, d), fixed, pipeline_mode=pl.Buffered(1)),
            pl.BlockSpec((1, d), fixed),
            pl.BlockSpec((tm, kp), row),
            pl.BlockSpec((kp, d), fixed, pipeline_mode=pl.Buffered(1))]


def _matmul_res_ln_ple_kernel(a_ref, w_ref, x_ref, g_ref, b_ref, wg_ref, bg_ref, p_ref, wp_ref,
                              o32_ref, o16_ref, *, alpha):
    h = jnp.dot(a_ref[...], w_ref[...], preferred_element_type=F32)
    y = _layer_norm(alpha * x_ref[...] + h, g_ref[...], b_ref[...])
    y = _gated_embedding(y, wg_ref, bg_ref, p_ref, wp_ref)
    o32_ref[...] = y
    o16_ref[...] = y.astype(o16_ref.dtype)


def _matmul_res_ln_ple(a, w, x_res, g, b, alpha, w_gate, b_gate, pb, w_proj):
    m, k = a.shape
    n = w.shape[1]
    kp = pb.shape[1]
    tm = _tile(m, 256)
    row = lambda i: (i, 0)
    fixed = lambda i: (0, 0)
    return pl.pallas_call(
        functools.partial(_matmul_res_ln_ple_kernel, alpha=alpha),
        grid=(m // tm,),
        in_specs=[pl.BlockSpec((tm, k), row),
                  pl.BlockSpec((k, n), fixed, pipeline_mode=pl.Buffered(1)),
                  pl.BlockSpec((tm, n), row), pl.BlockSpec((1, n), fixed), pl.BlockSpec((1, n), fixed)]
        + _ple_specs(tm, n, kp, row, fixed),
        out_specs=[pl.BlockSpec((tm, n), row), pl.BlockSpec((tm, n), row)],
        out_shape=[jax.ShapeDtypeStruct((m, n), F32), jax.ShapeDtypeStruct((m, n), BF16)],
        compiler_params=_params(("parallel",), 60),
        name="matmul_res_ln_ple",
    )(a, w, x_res, g.reshape(1, n), b.reshape(1, n), w_gate, b_gate.reshape(1, n), pb, w_proj)


def _matmul_res_ln(a, w, x_res, g, b, alpha):
    m, k = a.shape
    n = w.shape[1]
    tm = _tile(m, 256)
    row = lambda i: (i, 0)
    fixed = lambda i: (0, 0)
    return pl.pallas_call(
        functools.partial(_matmul_res_ln_kernel, alpha=alpha),
        grid=(m // tm,),
        in_specs=[pl.BlockSpec((tm, k), row),
                  pl.BlockSpec((k, n), fixed, pipeline_mode=pl.Buffered(1)),
                  pl.BlockSpec((tm, n), row), pl.BlockSpec((1, n), fixed), pl.BlockSpec((1, n), fixed)],
        out_specs=[pl.BlockSpec((tm, n), row), pl.BlockSpec((tm, n), row),
                   pl.BlockSpec((tm, n // 2), row)],
        out_shape=[jax.ShapeDtypeStruct((m, n), F32), jax.ShapeDtypeStruct((m, n), BF16),
                   jax.ShapeDtypeStruct((m, n // 2), jnp.uint32)],
        compiler_params=_params(("parallel",), 60),
        name="matmul_res_ln",
    )(a, w, x_res, g.reshape(1, n), b.reshape(1, n))


def _swiglu_up_kernel(x_ref, w1_ref, w3_ref, o_ref, w1b_ref, w3b_ref):
    @pl.when(pl.program_id(1) == 0)
    def _():
        w1b_ref[...] = w1_ref[...].astype(w1b_ref.dtype)
        w3b_ref[...] = w3_ref[...].astype(w3b_ref.dtype)

    x = x_ref[...]
    a = jnp.dot(x, w1b_ref[...], preferred_element_type=F32)
    b = jnp.dot(x, w3b_ref[...], preferred_element_type=F32)
    o_ref[...] = (a * jax.nn.sigmoid(a) * b).astype(o_ref.dtype)


def _swiglu_up(xb, w1, w3):
    m, k = xb.shape
    n = w1.shape[1]
    tm, tn = _tile(m, 1024), _tile(n, 512)
    return pl.pallas_call(
        _swiglu_up_kernel,
        grid=(n // tn, m // tm),
        in_specs=[pl.BlockSpec((tm, k), lambda j, i: (i, 0)),
                  pl.BlockSpec((k, tn), lambda j, i: (0, j)),
                  pl.BlockSpec((k, tn), lambda j, i: (0, j))],
        out_specs=pl.BlockSpec((tm, tn), lambda j, i: (i, j)),
        out_shape=jax.ShapeDtypeStruct((m, n), BF16),
        scratch_shapes=[pltpu.VMEM((k, tn), BF16), pltpu.VMEM((k, tn), BF16)],
        compiler_params=_params(("parallel", "arbitrary"), 56),
        name="swiglu_up",
    )(xb, w1, w3)


def _pool_kernel(uc_ref, up_ref, wg_ref, sc_ref, y_ref, buf_ref):
    s = pl.program_id(1)
    tm = uc_ref.shape[0]
    pd = wg_ref.shape[1]
    buf_ref[POOL_HALO:POOL_HALO + tm, :] = uc_ref[...]
    buf_ref[0:POOL_HALO, :] = jnp.where(s > 0, up_ref[...], 0.0)
    t = s * tm + lax.broadcasted_iota(jnp.int32, (tm, 1), 0)
    for g, w in enumerate(POOL_WINDOWS):
        cols = slice(g * pd, (g + 1) * pd)
        u = buf_ref[POOL_HALO:POOL_HALO + tm, cols]
        tot = u
        for i in range(1, w):
            tot = tot + buf_ref[POOL_HALO - i:POOL_HALO - i + tm, cols]
        cnt = jnp.minimum(t + 1, w).astype(F32)
        mixed = (tot / cnt - u).astype(wg_ref.dtype)
        y = jnp.dot(mixed, wg_ref[g], preferred_element_type=F32) * sc_ref[:, cols]
        y_ref[:, cols] = y.astype(y_ref.dtype)


def _pool_mix(u, w_group, scale, batch, seq):
    d = u.shape[1]
    tm = _tile(seq, 512)
    assert tm % POOL_HALO == 0 and max(POOL_WINDOWS) <= POOL_HALO
    halo_blocks = tm // POOL_HALO
    u3 = u.reshape(batch, seq, d)
    y = pl.pallas_call(
        _pool_kernel,
        grid=(batch, seq // tm),
        in_specs=[pl.BlockSpec((None, tm, d), lambda b, s: (b, s, 0)),
                  pl.BlockSpec((None, POOL_HALO, d),
                               lambda b, s: (b, jnp.maximum(s * halo_blocks - 1, 0), 0)),
                  pl.BlockSpec(w_group.shape, lambda b, s: (0, 0, 0)),
                  pl.BlockSpec((1, d), lambda b, s: (0, 0))],
        out_specs=pl.BlockSpec((None, tm, d), lambda b, s: (b, s, 0)),
        out_shape=jax.ShapeDtypeStruct((batch, seq, d), BF16),
        scratch_shapes=[pltpu.VMEM((POOL_HALO + tm, d), F32)],
        compiler_params=_params(("parallel", "arbitrary"), 48),
        name="pool_mix",
    )(u3, u3, w_group, scale.reshape(1, d))
    return y.reshape(batch * seq, d)


def _router_kernel(x_ref, wr_ref, tri_ref, e_ref, rank_ref, gate_ref, cnt_ref, run_ref):
    @pl.when(pl.program_id(0) == 0)
    def _():
        run_ref[...] = jnp.zeros_like(run_ref)

    logits = lax.dot_general(wr_ref[...], x_ref[...], (((1,), (1,)), ((), ())),
                             preferred_element_type=F32)
    n_exp = logits.shape[0]
    eidx = lax.broadcasted_iota(jnp.int32, logits.shape, 0)
    m1 = jnp.max(logits, axis=0, keepdims=True)
    i1 = jnp.min(jnp.where(logits == m1, eidx, n_exp), axis=0, keepdims=True)
    rest = jnp.where(eidx == i1, -jnp.inf, logits)
    m2 = jnp.max(rest, axis=0, keepdims=True)
    i2 = jnp.min(jnp.where(rest == m2, eidx, n_exp), axis=0, keepdims=True)
    ex = jnp.exp(m2 - m1)
    g1 = 1.0 / (1.0 + ex)
    g2 = ex / (1.0 + ex)

    oh1 = (eidx == i1).astype(F32)
    oh2 = (eidx == i2).astype(F32)
    tri = tri_ref[...]
    c1 = jnp.dot(oh1.astype(tri.dtype), tri, preferred_element_type=F32)
    c2 = jnp.dot(oh2.astype(tri.dtype), tri, preferred_element_type=F32)
    tot1 = jnp.sum(oh1, axis=1, keepdims=True)
    tot2 = jnp.sum(oh2, axis=1, keepdims=True)
    run = run_ref[:, 0:1]
    r1 = jnp.sum(oh1 * (run + c1), axis=0, keepdims=True)
    r2 = jnp.sum(oh2 * (run + tot1 + c2), axis=0, keepdims=True)
    run = run + tot1 + tot2
    run_ref[...] = jnp.broadcast_to(run, run_ref.shape)

    e_ref[...] = jnp.concatenate([i1, i2], axis=0)
    rank_ref[...] = jnp.concatenate([r1, r2], axis=0).astype(jnp.int32)
    gate_ref[...] = jnp.concatenate([g1, g2], axis=0)
    cnt_ref[...] = jnp.broadcast_to(run, cnt_ref.shape).astype(jnp.int32)


def _router(xb, w_router_t):
    m, k = xb.shape
    n_exp = w_router_t.shape[0]
    tm = _tile(m, 512)
    tri = (jnp.arange(tm)[:, None] < jnp.arange(tm)[None, :]).astype(BF16)
    pair = lambda dt: jax.ShapeDtypeStruct((TOP_K, m), dt)
    pair_spec = pl.BlockSpec((TOP_K, tm), lambda i: (0, i))
    return pl.pallas_call(
        _router_kernel,
        grid=(m // tm,),
        in_specs=[pl.BlockSpec((tm, k), lambda i: (i, 0)),
                  pl.BlockSpec((n_exp, k), lambda i: (0, 0)),
                  pl.BlockSpec((tm, tm), lambda i: (0, 0))],
        out_specs=[pair_spec, pair_spec, pair_spec, pl.BlockSpec((n_exp, LANES), lambda i: (0, 0))],
        out_shape=[pair(jnp.int32), pair(jnp.int32), pair(F32),
                   jax.ShapeDtypeStruct((n_exp, LANES), jnp.int32)],
        scratch_shapes=[pltpu.VMEM((n_exp, LANES), F32)],
        compiler_params=_params(("arbitrary",), 32),
        name="router",
    )(xb, w_router_t, tri)


def _tile_positions(pos, tm):
    return pos.reshape(TOP_K, -1, tm).transpose(1, 0, 2)


def _dispatch_kernel(fill_ref, pos_ref, x_ref, xs_ref, stage_ref, zero_ref, sem, fill_sem):
    i = pl.program_id(0)
    tm = x_ref.shape[0]
    slot = lax.rem(i, 2)

    @pl.when(i == 0)
    def _():
        zero_ref[...] = jnp.zeros_like(zero_ref)
        fill_rows = zero_ref.shape[0]
        n_groups = fill_ref.shape[0] - 1

        def fill(start):
            return pltpu.make_async_copy(
                zero_ref, xs_ref.at[pl.ds(pl.multiple_of(start, SUBLANES), fill_rows), :], fill_sem)

        fills = [fill(fill_ref[e]) for e in range(n_groups)]
        for c in fills:
            c.start()
        for c in fills:
            c.wait()

        def fill_unused(t, carry):
            c = fill(t * fill_rows)
            c.start()
            c.wait()
            return carry
        lax.fori_loop(fill_ref[n_groups] // fill_rows, xs_ref.shape[0] // fill_rows, fill_unused, 0)

    def row_copy(s, r, k):
        return pltpu.make_async_copy(stage_ref.at[s, pl.ds(r, 1), :],
                                     xs_ref.at[pl.ds(pos_ref[k, r], 1), :], sem.at[s])

    def start(r, carry):
        for k in range(TOP_K):
            row_copy(slot, r, k).start()
        return carry

    def drain(s):
        def wait(r, carry):
            for k in range(TOP_K):
                pltpu.make_async_copy(stage_ref.at[s, pl.ds(r, 1), :],
                                      xs_ref.at[pl.ds(0, 1), :], sem.at[s]).wait()
            return carry
        lax.fori_loop(0, tm, wait, 0, unroll=ROW_COPY_UNROLL)

    stage_ref[slot] = x_ref[...]
    lax.fori_loop(0, tm, start, 0, unroll=ROW_COPY_UNROLL)

    @pl.when(i > 0)
    def _():
        drain(1 - slot)

    @pl.when(i == pl.num_programs(0) - 1)
    def _():
        drain(slot)


def _dispatch(fill_starts, pos, xp, n_slots, fill_rows):
    m, w = xp.shape
    tm = _tile(m, 256)
    grid_spec = pltpu.PrefetchScalarGridSpec(
        num_scalar_prefetch=1,
        grid=(m // tm,),
        in_specs=[pl.BlockSpec((None, TOP_K, tm), lambda i, fill: (i, 0, 0), memory_space=pltpu.SMEM),
                  pl.BlockSpec((tm, w), lambda i, fill: (i, 0))],
        out_specs=pl.BlockSpec(memory_space=pl.ANY),
        scratch_shapes=[pltpu.VMEM((2, tm, w), xp.dtype), pltpu.VMEM((fill_rows, w), xp.dtype),
                        pltpu.SemaphoreType.DMA((2,)), pltpu.SemaphoreType.DMA(())],
    )
    return pl.pallas_call(
        _dispatch_kernel,
        grid_spec=grid_spec,
        out_shape=jax.ShapeDtypeStruct((n_slots, w), xp.dtype),
        compiler_params=_params(("arbitrary",), 32, disable_bounds_checks=True),
        name="dispatch",
    )(fill_starts, _tile_positions(pos, tm), xp)


def _cast_expert_weights(first_ref, pairs):
    @pl.when(first_ref[pl.program_id(1)] > 0)
    def _():
        for src, dst in pairs:
            dst[...] = src[...].astype(dst.dtype)


def _expert_up_kernel(te_ref, tr_ref, tv_ref, tf_ref, xs_ref, w1_ref, w3_ref, h_ref, w1b_ref, w3b_ref):
    del te_ref, tr_ref
    _cast_expert_weights(tf_ref, [(w1_ref, w1b_ref), (w3_ref, w3b_ref)])

    @pl.when(tv_ref[pl.program_id(1)] > 0)
    def _():
        x = _unpack_bf16_pairs(xs_ref[...])
        a = jnp.dot(x, w1b_ref[...], preferred_element_type=F32)
        b = jnp.dot(x, w3b_ref[...], preferred_element_type=F32)
        h_ref[...] = (a * jax.nn.sigmoid(a) * b).astype(h_ref.dtype)

    @pl.when(tv_ref[pl.program_id(1)] == 0)
    def _():
        h_ref[...] = jnp.zeros_like(h_ref)


def _expert_specs():
    rows = lambda c, i, te, tr, tv, tf: (tr[i], 0)
    weights = lambda c, i, te, tr, tv, tf: (te[i], 0, c)
    out = lambda c, i, te, tr, tv, tf: (i, c)
    return rows, weights, out


def _expert_up(tiles, xs, w1, w3, tm):
    n_slots = xs.shape[0]
    _, k, n = w1.shape
    tn = _tile(n, 512)
    rows, weights, out = _expert_specs()
    grid_spec = pltpu.PrefetchScalarGridSpec(
        num_scalar_prefetch=4,
        grid=(n // tn, n_slots // tm),
        in_specs=[pl.BlockSpec((tm, k // 2), rows),
                  pl.BlockSpec((None, k, tn), weights),
                  pl.BlockSpec((None, k, tn), weights)],
        out_specs=pl.BlockSpec((tm, tn), out),
        scratch_shapes=[pltpu.VMEM((k, tn), BF16), pltpu.VMEM((k, tn), BF16)],
    )
    return pl.pallas_call(
        _expert_up_kernel,
        grid_spec=grid_spec,
        out_shape=jax.ShapeDtypeStruct((n_slots, n), BF16),
        compiler_params=_params(("parallel", "arbitrary"), 48),
        name="expert_up",
    )(*tiles, xs, w1, w3)


def _expert_down_kernel(te_ref, tr_ref, tv_ref, tf_ref, h_ref, w2_ref, y_ref, w2b_ref):
    del te_ref, tr_ref
    _cast_expert_weights(tf_ref, [(w2_ref, w2b_ref)])

    @pl.when(tv_ref[pl.program_id(1)] > 0)
    def _():
        y_ref[...] = jnp.dot(h_ref[...], w2b_ref[...], preferred_element_type=F32)

    @pl.when(tv_ref[pl.program_id(1)] == 0)
    def _():
        y_ref[...] = jnp.zeros_like(y_ref)


def _expert_down(tiles, h, w2, tm):
    n_slots, k = h.shape
    n = w2.shape[2]
    tn = _tile(n, 512)
    rows, weights, out = _expert_specs()
    grid_spec = pltpu.PrefetchScalarGridSpec(
        num_scalar_prefetch=4,
        grid=(n // tn, n_slots // tm),
        in_specs=[pl.BlockSpec((tm, k), rows),
                  pl.BlockSpec((None, k, tn), weights, pipeline_mode=pl.Buffered(1))],
        out_specs=pl.BlockSpec((tm, tn), out),
        scratch_shapes=[pltpu.VMEM((k, tn), BF16)],
    )
    return pl.pallas_call(
        _expert_down_kernel,
        grid_spec=grid_spec,
        out_shape=jax.ShapeDtypeStruct((n_slots, n), F32),
        compiler_params=_params(("parallel", "arbitrary"), 56),
        name="expert_down",
    )(*tiles, h, w2)


def _gather_mix_ln_kernel(pos_ref, next_pos_ref, ys_ref, gate_ref, x_ref, g_ref, b_ref, wg_ref,
                          bg_ref, p_ref, wp_ref, o32_ref, o16_ref, buf_ref, sem, *, alpha):
    i = pl.program_id(0)
    tm = x_ref.shape[0]
    slot = lax.rem(i, 2)

    def issue(idx_ref, s):
        def start(r, carry):
            for k in range(TOP_K):
                pltpu.make_async_copy(ys_ref.at[pl.ds(idx_ref[k, r], 1), :],
                                      buf_ref.at[s, k, pl.ds(r, 1), :], sem.at[s]).start()
            return carry
        lax.fori_loop(0, tm, start, 0, unroll=ROW_COPY_UNROLL)

    def wait(r, carry):
        for k in range(TOP_K):
            pltpu.make_async_copy(ys_ref.at[pl.ds(0, 1), :],
                                  buf_ref.at[slot, k, pl.ds(r, 1), :], sem.at[slot]).wait()
        return carry

    @pl.when(i == 0)
    def _():
        issue(pos_ref, slot)

    @pl.when(i + 1 < pl.num_programs(0))
    def _():
        issue(next_pos_ref, 1 - slot)

    lax.fori_loop(0, tm, wait, 0, unroll=ROW_COPY_UNROLL)
    gates = gate_ref[...]
    f = gates[:, 0:1] * buf_ref[slot, 0] + gates[:, 1:2] * buf_ref[slot, 1]
    y = _layer_norm(alpha * x_ref[...] + f, g_ref[...], b_ref[...])
    y = _gated_embedding(y, wg_ref, bg_ref, p_ref, wp_ref)
    o32_ref[...] = y
    o16_ref[...] = y.astype(o16_ref.dtype)


def _gather_mix_ln(pos, ys, gates, x_res, g, b, alpha, w_gate, b_gate, pb, w_proj):
    m, d = x_res.shape
    kp = pb.shape[1]
    tm = _tile(m, 256)
    last = m // tm - 1
    row = lambda i: (i, 0)
    fixed = lambda i: (0, 0)
    pos_tiles = _tile_positions(pos, tm)
    return pl.pallas_call(
        functools.partial(_gather_mix_ln_kernel, alpha=alpha),
        grid=(m // tm,),
        in_specs=[pl.BlockSpec((None, TOP_K, tm), lambda i: (i, 0, 0), memory_space=pltpu.SMEM),
                  pl.BlockSpec((None, TOP_K, tm), lambda i: (jnp.minimum(i + 1, last), 0, 0),
                               memory_space=pltpu.SMEM),
                  pl.BlockSpec(memory_space=pl.ANY),
                  pl.BlockSpec((tm, TOP_K), row),
                  pl.BlockSpec((tm, d), row),
                  pl.BlockSpec((1, d), fixed), pl.BlockSpec((1, d), fixed)]
        + _ple_specs(tm, d, kp, row, fixed),
        out_specs=[pl.BlockSpec((tm, d), row), pl.BlockSpec((tm, d), row)],
        out_shape=[jax.ShapeDtypeStruct((m, d), F32), jax.ShapeDtypeStruct((m, d), BF16)],
        scratch_shapes=[pltpu.VMEM((2, TOP_K, tm, d), F32), pltpu.SemaphoreType.DMA((2,))],
        compiler_params=_params(("arbitrary",), 48, disable_bounds_checks=True),
        name="gather_mix_ln",
    )(pos_tiles, pos_tiles, ys, gates, x_res, g.reshape(1, d), b.reshape(1, d),
      w_gate, b_gate.reshape(1, d), pb, w_proj)


def _moe_ffn_ln(x32, xb, x_words, w_router, w1, w3, w2, g, b, alpha, ple):
    m, _ = xb.shape
    n_exp = w1.shape[0]
    tm = _tile(m, 1024)
    experts, ranks, gates, counts = _router(xb, w_router.T.astype(BF16))

    counts = counts[:, 0]
    padded = (counts + tm - 1) // tm * tm
    ends = jnp.cumsum(padded)
    starts = ends - padded
    expert_ids = jnp.arange(n_exp, dtype=jnp.int32)[:, None, None]
    pos = jnp.sum(jnp.where(experts[None] == expert_ids, starts[:, None, None], 0), axis=0) + ranks
    assert (TOP_K * m) % tm == 0
    n_slots = TOP_K * m + n_exp * tm
    tile_index = jnp.arange(n_slots // tm, dtype=jnp.int32)
    tile_valid = (tile_index * tm < ends[-1]).astype(jnp.int32)
    tile_row = jnp.minimum(tile_index, jnp.maximum(ends[-1] // tm - 1, 0))
    tile_expert = jnp.sum((ends[None, :] <= (tile_row * tm)[:, None]).astype(jnp.int32), axis=1)
    tile_expert = jnp.minimum(tile_expert, n_exp - 1)
    prev_expert = jnp.concatenate([jnp.full((1,), -1, jnp.int32), tile_expert[:-1]])
    tile_first = (tile_expert != prev_expert).astype(jnp.int32)
    tiles = (tile_expert, tile_row, tile_valid, tile_first)

    fill_starts = (starts + counts) // SUBLANES * SUBLANES
    xs = _dispatch(jnp.concatenate([fill_starts, ends[-1:]]), pos, x_words, n_slots, tm)
    h = _expert_up(tiles, xs, w1, w3, tm)
    ys = _expert_down(tiles, h, w2, tm)
    return _gather_mix_ln(pos, ys, gates.T, x32, g, b, alpha, *ple)


def kernel(x, p, attn_w_qkv, attn_w_o, pool_w_in, pool_w_group, pool_scale, pool_w_o,
           ln_mix_g, ln_mix_b, ln_ffn_g, ln_ffn_b, ffn_w1, ffn_w3, ffn_w2,
           moe_router, moe_w1, moe_w3, moe_w2, ple_w_proj, ple_w_gate, ple_b_gate):
    batch, seq, d = x.shape
    depth = p.shape[0]
    att_w = N_HEADS * LANES
    assert attn_w_qkv.shape[2] == N_GROUPS * 3 * att_w
    alpha = (2 * depth) ** 0.25
    m = batch * seq
    bf = lambda t: t.astype(BF16)

    x32 = x.reshape(m, d)
    xb = bf(x32)
    for i in range(depth):
        j = i // 2
        ple = (bf(ple_w_gate[i]), ple_b_gate[i], bf(p[i].reshape(m, -1)), bf(ple_w_proj[i]))
        if i % 2 == 0:
            outs, lses = [], []
            for grp, (_, dil) in enumerate(DILATED_GROUPS):
                x_g = _residue_major(xb, batch, seq, dil) if dil > 1 else xb
                qkv = _qkv_rope(x_g, attn_w_qkv[j], _rope_tables(seq, dil), grp, seq, att_w)
                o_g, lse_g = _dilated_attention(qkv, grp, seq, att_w)
                outs.append(o_g)
                lses.append(lse_g)
            x32, xb = _attn_out(outs, lses, bf(attn_w_o[j]), x32, ln_mix_g[i], ln_mix_b[i],
                                alpha, batch, seq)
            h = _swiglu_up(xb, ffn_w1[j], ffn_w3[j])
            x32, xb = _matmul_res_ln_ple(h, bf(ffn_w2[j]), x32, ln_ffn_g[i], ln_ffn_b[i], alpha, *ple)
        else:
            u = _matmul(xb, bf(pool_w_in[j]), F32)
            y = _pool_mix(u, bf(pool_w_group[j]), pool_scale[j].reshape(-1), batch, seq)
            x32, xb, x_words = _matmul_res_ln(y, bf(pool_w_o[j]), x32, ln_mix_g[i], ln_mix_b[i], alpha)
            x32, xb = _moe_ffn_ln(x32, xb, x_words, moe_router[j], moe_w1[j], moe_w3[j], moe_w2[j],
                                  ln_ffn_g[i], ln_ffn_b[i], alpha, ple)
    return x32.reshape(batch, seq, d)
```

```python
import functools

import jax
import jax.numpy as jnp
from jax import lax
from jax.experimental import pallas as pl
from jax.experimental.pallas import tpu as pltpu

F32 = jnp.float32
BF16 = jnp.bfloat16

N_HEADS = 16
DILATED_GROUPS = ((128, 1), (512, 4), (2048, 16))
N_GROUPS = len(DILATED_GROUPS)
ATT_BLOCK = 128
ATT_BLOCKS_PER_STEP = 2
ROPE_THETA = 10000.0
POOL_WINDOWS = (2, 4, 8, 16)
POOL_HALO = 16
TOP_K = 2
ROW_COPY_UNROLL = 4
LN_EPS = 1e-5
NEG_INF = -1e30

LANES = 128
SUBLANES = 8
MIB = 1024 * 1024


def _tile(dim, pref):
    t = min(dim, pref)
    while dim % t:
        t //= 2
    return t


def _params(semantics, vmem_mib, **kw):
    return pltpu.CompilerParams(dimension_semantics=semantics, vmem_limit_bytes=vmem_mib * MIB, **kw)


def _layer_norm(y, g, b):
    mu = jnp.mean(y, axis=-1, keepdims=True)
    yc = y - mu
    var = jnp.mean(yc * yc, axis=-1, keepdims=True)
    return yc * lax.rsqrt(var + LN_EPS) * g + b


def _pack_bf16_pairs(y):
    half = y.shape[1] // 2
    bits = lambda t: lax.bitcast_convert_type(t.astype(BF16).astype(F32), jnp.uint32)
    return (bits(y[:, :half]) >> 16) | bits(y[:, half:])


def _unpack_bf16_pairs(words):
    lo = lax.bitcast_convert_type(words << 16, F32)
    hi = lax.bitcast_convert_type(words & jnp.uint32(0xFFFF0000), F32)
    return jnp.concatenate([lo, hi], axis=1).astype(BF16)


def _residue_major(t, batch, seq, dil):
    d = t.shape[1]
    return t.reshape(batch, seq // dil, dil, d).transpose(0, 2, 1, 3).reshape(batch * seq, d)


def _qkv_rope_kernel(x_ref, w_ref, cos_ref, sin_ref, o_ref, wb_ref):
    @pl.when(pl.program_id(1) == 0)
    def _():
        wb_ref[...] = w_ref[...].astype(wb_ref.dtype)

    acc = jnp.dot(x_ref[...], wb_ref[...], preferred_element_type=F32)
    cos = cos_ref[...]
    sin = sin_ref[...]
    for h in range(acc.shape[1] // LANES):
        t = acc[:, h * LANES:(h + 1) * LANES]
        rot = pltpu.roll(t, LANES // 2, 1)
        o_ref[:, h * LANES:(h + 1) * LANES] = (t * cos + rot * sin).astype(o_ref.dtype)


def _qkv_rope(x_g, w, tables, group, seq, att_w):
    m, k = x_g.shape
    cos_tab, sin_tab = tables
    tm = _tile(seq, 1024)
    tn = _tile(att_w, 1024)
    s_tiles = seq // tm
    per_role = att_w // tn
    per_group = 3 * per_role
    table_spec = pl.BlockSpec((None, tm, LANES), lambda j, i: (j // per_role, i % s_tiles, 0))
    return pl.pallas_call(
        _qkv_rope_kernel,
        grid=(per_group, m // tm),
        in_specs=[pl.BlockSpec((tm, k), lambda j, i: (i, 0)),
                  pl.BlockSpec((k, tn), lambda j, i: (0, group * per_group + j)),
                  table_spec, table_spec],
        out_specs=pl.BlockSpec((tm, tn), lambda j, i: (i, j)),
        out_shape=jax.ShapeDtypeStruct((m, 3 * att_w), BF16),
        scratch_shapes=[pltpu.VMEM((k, tn), BF16)],
        compiler_params=_params(("parallel", "arbitrary"), 56),
        name=f"qkv_rope_g{group}",
    )(x_g, w, cos_tab, sin_tab)


def _rope_tables(seq, dil):
    half = LANES // 2
    inv = ROPE_THETA ** (-jnp.arange(half, dtype=F32) / half)
    row = jnp.arange(seq)
    n_sub = seq // dil
    pos = (row % n_sub) * dil + row // n_sub
    ang = pos.astype(F32)[:, None] * inv[None, :]
    cos, sin = jnp.cos(ang), jnp.sin(ang)
    cos_full = jnp.concatenate([cos, cos], axis=-1)
    sin_signed = jnp.concatenate([-sin, sin], axis=-1)
    return (jnp.stack([cos_full, cos_full, jnp.ones_like(cos_full)]),
            jnp.stack([sin_signed, sin_signed, jnp.zeros_like(cos_full)]))


def _attn_kernel(q_ref, kc_ref, kp_ref, vc_ref, vp_ref, o_ref, lse_ref, *, nb, n_back, scale):
    blk = kp_ref.shape[0]
    sub = q_ref.shape[0] // blk
    first_has_prev = lax.rem(pl.program_id(0) * sub, nb) > 0
    qi = lax.broadcasted_iota(jnp.int32, (blk, 2 * blk), 0) + blk
    ki = lax.broadcasted_iota(jnp.int32, (blk, 2 * blk), 1)
    dist = qi - ki
    band = (dist >= 0) & (dist <= n_back)
    bias_inner = jnp.where(band, 0.0, NEG_INF)
    bias_first = jnp.where(band & (first_has_prev | (ki >= blk)), 0.0, NEG_INF)
    lane = lax.broadcasted_iota(jnp.int32, (blk, LANES), 1)
    for t in range(sub):
        rows = slice(t * blk, (t + 1) * blk)
        before = slice((t - 1) * blk, t * blk)
        bias = bias_first if t == 0 else bias_inner
        lse_tile = jnp.zeros((blk, LANES), F32)
        for h in range(q_ref.shape[1] // LANES):
            hs = slice(h * LANES, (h + 1) * LANES)
            k_prev = kp_ref[:, hs] if t == 0 else kc_ref[before, hs]
            v_prev = vp_ref[:, hs] if t == 0 else vc_ref[before, hs]
            k2 = jnp.concatenate([k_prev, kc_ref[rows, hs]], axis=0)
            v2 = jnp.concatenate([v_prev, vc_ref[rows, hs]], axis=0)
            s = lax.dot_general(q_ref[rows, hs], k2, (((1,), (1,)), ((), ())),
                                preferred_element_type=F32) * scale + bias
            m = jnp.max(s, axis=-1, keepdims=True)
            e = jnp.exp(s - m)
            l = jnp.sum(e, axis=-1, keepdims=True)
            p = (e * (1.0 / l)).astype(v2.dtype)
            o_ref[rows, hs] = jnp.dot(p, v2, preferred_element_type=F32)
            lse_tile = jnp.where(lane == h, m + jnp.log(l), lse_tile)
        lse_ref[rows, :] = lse_tile


def _dilated_attention(qkv, group, seq, att_w):
    m = qkv.shape[0]
    window, dil = DILATED_GROUPS[group]
    nb = seq // (dil * ATT_BLOCK)
    sub = ATT_BLOCKS_PER_STEP if nb % ATT_BLOCKS_PER_STEP == 0 else 1
    assert seq % (dil * ATT_BLOCK) == 0 and window // dil <= ATT_BLOCK
    step_rows = sub * ATT_BLOCK

    def spec(role, prev):
        if prev:
            return pl.BlockSpec((ATT_BLOCK, att_w), lambda r: (jnp.maximum(sub * r - 1, 0), role))
        return pl.BlockSpec((step_rows, att_w), lambda r: (r, role))

    return pl.pallas_call(
        functools.partial(_attn_kernel, nb=nb, n_back=window // dil, scale=LANES ** -0.5),
        grid=(m // step_rows,),
        in_specs=[spec(0, False), spec(1, False), spec(1, True), spec(2, False), spec(2, True)],
        out_specs=[pl.BlockSpec((step_rows, att_w), lambda r: (r, 0)),
                   pl.BlockSpec((step_rows, LANES), lambda r: (r, 0))],
        out_shape=[jax.ShapeDtypeStruct((m, att_w), F32), jax.ShapeDtypeStruct((m, LANES), F32)],
        compiler_params=_params(("arbitrary",), 32),
        name=f"dilated_attn_g{group}",
    )(qkv, qkv, qkv, qkv, qkv)


def _token_order(ref, scr_ref):
    dil, n, w = ref.shape
    heads = range(w // LANES)
    if dil == 1:
        return [ref[0, :, h * LANES:(h + 1) * LANES] for h in heads]
    for r in range(dil):
        for h in heads:
            scr_ref[h, pl.ds(r, n, stride=dil), :] = ref[r, :, h * LANES:(h + 1) * LANES]
    return [scr_ref[h] for h in heads]


def _attn_out_kernel(*refs, alpha):
    o_refs, l_refs = refs[:N_GROUPS], refs[N_GROUPS:2 * N_GROUPS]
    w_ref, x_ref, g_ref, b_ref, o32_ref, o16_ref = refs[2 * N_GROUPS:2 * N_GROUPS + 6]
    scratch = refs[2 * N_GROUPS + 6:]
    o_scr, l_scr = scratch[:N_GROUPS], scratch[N_GROUPS:]
    outs = [_token_order(o_refs[g], o_scr[g]) for g in range(N_GROUPS)]
    lses = [_token_order(l_refs[g], l_scr[g])[0] for g in range(N_GROUPS)]
    m = functools.reduce(jnp.maximum, lses)
    es = [jnp.exp(l - m) for l in lses]
    inv = 1.0 / functools.reduce(jnp.add, es)
    wts = [e * inv for e in es]
    tm = x_ref.shape[0]
    heads = []
    for h in range(len(outs[0])):
        acc = None
        for g in range(N_GROUPS):
            term = jnp.broadcast_to(wts[g][:, h:h + 1], (tm, LANES)) * outs[g][h]
            acc = term if acc is None else acc + term
        heads.append(acc.astype(w_ref.dtype))
    mixed = jnp.concatenate(heads, axis=1)
    hid = jnp.dot(mixed, w_ref[...], preferred_element_type=F32)
    y = _layer_norm(alpha * x_ref[...] + hid, g_ref[...], b_ref[...])
    o32_ref[...] = y
    o16_ref[...] = y.astype(o16_ref.dtype)


def _attn_out(outs, lses, w_o, x_res, g, b, alpha, batch, seq):
    n_groups = len(outs)
    m, att_w = outs[0].shape
    d = w_o.shape[1]
    dils = [dil for _, dil in DILATED_GROUPS]
    tm = _tile(seq, 256)
    s_tiles = seq // tm
    assert all(tm % (8 * dil) == 0 for dil in dils)

    def group_view(t, g):
        return t.reshape(batch, dils[g], seq // dils[g], t.shape[-1])

    def group_spec(g, width):
        return pl.BlockSpec((None, dils[g], tm // dils[g], width),
                            lambda i: (i // s_tiles, 0, i % s_tiles, 0))

    row = lambda i: (i, 0)
    fixed = lambda i: (0, 0)
    return pl.pallas_call(
        functools.partial(_attn_out_kernel, alpha=alpha),
        grid=(m // tm,),
        in_specs=[group_spec(g, att_w) for g in range(n_groups)]
        + [group_spec(g, LANES) for g in range(n_groups)]
        + [pl.BlockSpec((att_w, d), fixed, pipeline_mode=pl.Buffered(1)),
           pl.BlockSpec((tm, d), row), pl.BlockSpec((1, d), fixed), pl.BlockSpec((1, d), fixed)],
        out_specs=[pl.BlockSpec((tm, d), row), pl.BlockSpec((tm, d), row)],
        out_shape=[jax.ShapeDtypeStruct((m, d), F32), jax.ShapeDtypeStruct((m, d), BF16)],
        scratch_shapes=[pltpu.VMEM((att_w // LANES, tm, LANES), F32) for _ in range(n_groups)]
        + [pltpu.VMEM((1, tm, LANES), F32) for _ in range(n_groups)],
        compiler_params=_params(("parallel",), 56),
        name="attn_out_ln",
    )(*[group_view(outs[g], g) for g in range(n_groups)],
      *[group_view(lses[g], g) for g in range(n_groups)],
      w_o, x_res, g.reshape(1, d), b.reshape(1, d))


def _matmul_kernel(a_ref, w_ref, o_ref):
    o_ref[...] = jnp.dot(a_ref[...], w_ref[...], preferred_element_type=F32).astype(o_ref.dtype)


def _matmul(a, w, out_dtype):
    m, k = a.shape
    n = w.shape[1]
    tm, tn = _tile(m, 1024), _tile(n, 1024)
    return pl.pallas_call(
        _matmul_kernel,
        grid=(m // tm, n // tn),
        in_specs=[pl.BlockSpec((tm, k), lambda i, j: (i, 0)),
                  pl.BlockSpec((k, tn), lambda i, j: (0, j))],
        out_specs=pl.BlockSpec((tm, tn), lambda i, j: (i, j)),
        out_shape=jax.ShapeDtypeStruct((m, n), out_dtype),
        compiler_params=_params(("parallel", "parallel"), 48),
        name="matmul",
    )(a, w)


def _matmul_res_ln_kernel(a_ref, w_ref, x_ref, g_ref, b_ref, o32_ref, o16_ref, opk_ref, *, alpha):
    h = jnp.dot(a_ref[...], w_ref[...], preferred_element_type=F32)
    y = _layer_norm(alpha * x_ref[...] + h, g_ref[...], b_ref[...])
    o32_ref[...] = y
    o16_ref[...] = y.astype(o16_ref.dtype)
    opk_ref[...] = _pack_bf16_pairs(y)


def _gated_embedding(y, wg_ref, bg_ref, p_ref, wp_ref):
    z = jnp.dot(y.astype(wg_ref.dtype), wg_ref[...], preferred_element_type=F32) + bg_ref[...]
    e = jnp.dot(p_ref[...], wp_ref[...], preferred_element_type=F32)
    return y + jax.nn.sigmoid(z) * e


def _ple_specs(tm, d, kp, row, fixed):
    return [pl.BlockSpec((d, d), fixed, pipeline_mode=pl.Buffered(1)),
            pl.BlockSpec((1, d), fixed),
            pl.BlockSpec((tm, kp), row),
            pl.BlockSpec((kp, d), fixed, pipeline_mode=pl.Buffered(1))]


def _matmul_res_ln_ple_kernel(a_ref, w_ref, x_ref, g_ref, b_ref, wg_ref, bg_ref, p_ref, wp_ref,
                              o32_ref, o16_ref, *, alpha):
    h = jnp.dot(a_ref[...], w_ref[...], preferred_element_type=F32)
    y = _layer_norm(alpha * x_ref[...] + h, g_ref[...], b_ref[...])
    y = _gated_embedding(y, wg_ref, bg_ref, p_ref, wp_ref)
    o32_ref[...] = y
    o16_ref[...] = y.astype(o16_ref.dtype)


def _matmul_res_ln_ple(a, w, x_res, g, b, alpha, w_gate, b_gate, pb, w_proj):
    m, k = a.shape
    n = w.shape[1]
    kp = pb.shape[1]
    tm = _tile(m, 256)
    row = lambda i: (i, 0)
    fixed = lambda i: (0, 0)
    return pl.pallas_call(
        functools.partial(_matmul_res_ln_ple_kernel, alpha=alpha),
        grid=(m // tm,),
        in_specs=[pl.BlockSpec((tm, k), row),
                  pl.BlockSpec((k, n), fixed, pipeline_mode=pl.Buffered(1)),
                  pl.BlockSpec((tm, n), row), pl.BlockSpec((1, n), fixed), pl.BlockSpec((1, n), fixed)]
        + _ple_specs(tm, n, kp, row, fixed),
        out_specs=[pl.BlockSpec((tm, n), row), pl.BlockSpec((tm, n), row)],
        out_shape=[jax.ShapeDtypeStruct((m, n), F32), jax.ShapeDtypeStruct((m, n), BF16)],
        compiler_params=_params(("parallel",), 60),
        name="matmul_res_ln_ple",
    )(a, w, x_res, g.reshape(1, n), b.reshape(1, n), w_gate, b_gate.reshape(1, n), pb, w_proj)


def _matmul_res_ln(a, w, x_res, g, b, alpha):
    m, k = a.shape
    n = w.shape[1]
    tm = _tile(m, 256)
    row = lambda i: (i, 0)
    fixed = lambda i: (0, 0)
    return pl.pallas_call(
        functools.partial(_matmul_res_ln_kernel, alpha=alpha),
        grid=(m // tm,),
        in_specs=[pl.BlockSpec((tm, k), row),
                  pl.BlockSpec((k, n), fixed, pipeline_mode=pl.Buffered(1)),
                  pl.BlockSpec((tm, n), row), pl.BlockSpec((1, n), fixed), pl.BlockSpec((1, n), fixed)],
        out_specs=[pl.BlockSpec((tm, n), row), pl.BlockSpec((tm, n), row),
                   pl.BlockSpec((tm, n // 2), row)],
        out_shape=[jax.ShapeDtypeStruct((m, n), F32), jax.ShapeDtypeStruct((m, n), BF16),
                   jax.ShapeDtypeStruct((m, n // 2), jnp.uint32)],
        compiler_params=_params(("parallel",), 60),
        name="matmul_res_ln",
    )(a, w, x_res, g.reshape(1, n), b.reshape(1, n))


def _swiglu_up_kernel(x_ref, w1_ref, w3_ref, o_ref, w1b_ref, w3b_ref):
    @pl.when(pl.program_id(1) == 0)
    def _():
        w1b_ref[...] = w1_ref[...].astype(w1b_ref.dtype)
        w3b_ref[...] = w3_ref[...].astype(w3b_ref.dtype)

    x = x_ref[...]
    a = jnp.dot(x, w1b_ref[...], preferred_element_type=F32)
    b = jnp.dot(x, w3b_ref[...], preferred_element_type=F32)
    o_ref[...] = (a * jax.nn.sigmoid(a) * b).astype(o_ref.dtype)


def _swiglu_up(xb, w1, w3):
    m, k = xb.shape
    n = w1.shape[1]
    tm, tn = _tile(m, 2048), _tile(n, 512)
    return pl.pallas_call(
        _swiglu_up_kernel,
        grid=(n // tn, m // tm),
        in_specs=[pl.BlockSpec((tm, k), lambda j, i: (i, 0)),
                  pl.BlockSpec((k, tn), lambda j, i: (0, j)),
                  pl.BlockSpec((k, tn), lambda j, i: (0, j))],
        out_specs=pl.BlockSpec((tm, tn), lambda j, i: (i, j)),
        out_shape=jax.ShapeDtypeStruct((m, n), BF16),
        scratch_shapes=[pltpu.VMEM((k, tn), BF16), pltpu.VMEM((k, tn), BF16)],
        compiler_params=_params(("parallel", "arbitrary"), 56),
        name="swiglu_up",
    )(xb, w1, w3)


def _pool_kernel(uc_ref, up_ref, wg_ref, sc_ref, y_ref, buf_ref):
    s = pl.program_id(1)
    tm = uc_ref.shape[0]
    pd = wg_ref.shape[1]
    buf_ref[POOL_HALO:POOL_HALO + tm, :] = uc_ref[...]
    buf_ref[0:POOL_HALO, :] = jnp.where(s > 0, up_ref[...], 0.0)
    t = s * tm + lax.broadcasted_iota(jnp.int32, (tm, 1), 0)
    for g, w in enumerate(POOL_WINDOWS):
        cols = slice(g * pd, (g + 1) * pd)
        u = buf_ref[POOL_HALO:POOL_HALO + tm, cols]
        tot = u
        for i in range(1, w):
            tot = tot + buf_ref[POOL_HALO - i:POOL_HALO - i + tm, cols]
        cnt = jnp.minimum(t + 1, w).astype(F32)
        mixed = (tot / cnt - u).astype(wg_ref.dtype)
        y = jnp.dot(mixed, wg_ref[g], preferred_element_type=F32) * sc_ref[:, cols]
        y_ref[:, cols] = y.astype(y_ref.dtype)


def _pool_mix(u, w_group, scale, batch, seq):
    d = u.shape[1]
    tm = _tile(seq, 512)
    assert tm % POOL_HALO == 0 and max(POOL_WINDOWS) <= POOL_HALO
    halo_blocks = tm // POOL_HALO
    u3 = u.reshape(batch, seq, d)
    y = pl.pallas_call(
        _pool_kernel,
        grid=(batch, seq // tm),
        in_specs=[pl.BlockSpec((None, tm, d), lambda b, s: (b, s, 0)),
                  pl.BlockSpec((None, POOL_HALO, d),
                               lambda b, s: (b, jnp.maximum(s * halo_blocks - 1, 0), 0)),
                  pl.BlockSpec(w_group.shape, lambda b, s: (0, 0, 0)),
                  pl.BlockSpec((1, d), lambda b, s: (0, 0))],
        out_specs=pl.BlockSpec((None, tm, d), lambda b, s: (b, s, 0)),
        out_shape=jax.ShapeDtypeStruct((batch, seq, d), BF16),
        scratch_shapes=[pltpu.VMEM((POOL_HALO + tm, d), F32)],
        compiler_params=_params(("parallel", "arbitrary"), 48),
        name="pool_mix",
    )(u3, u3, w_group, scale.reshape(1, d))
    return y.reshape(batch * seq, d)


def _router_kernel(x_ref, wr_ref, tri_ref, e_ref, rank_ref, gate_ref, cnt_ref, run_ref):
    @pl.when(pl.program_id(0) == 0)
    def _():
        run_ref[...] = jnp.zeros_like(run_ref)

    logits = lax.dot_general(wr_ref[...], x_ref[...], (((1,), (1,)), ((), ())),
                             preferred_element_type=F32)
    n_exp = logits.shape[0]
    eidx = lax.broadcasted_iota(jnp.int32, logits.shape, 0)
    m1 = jnp.max(logits, axis=0, keepdims=True)
    i1 = jnp.min(jnp.where(logits == m1, eidx, n_exp), axis=0, keepdims=True)
    rest = jnp.where(eidx == i1, -jnp.inf, logits)
    m2 = jnp.max(rest, axis=0, keepdims=True)
    i2 = jnp.min(jnp.where(rest == m2, eidx, n_exp), axis=0, keepdims=True)
    ex = jnp.exp(m2 - m1)
    g1 = 1.0 / (1.0 + ex)
    g2 = ex / (1.0 + ex)

    oh1 = (eidx == i1).astype(F32)
    oh2 = (eidx == i2).astype(F32)
    tri = tri_ref[...]
    c1 = jnp.dot(oh1.astype(tri.dtype), tri, preferred_element_type=F32)
    c2 = jnp.dot(oh2.astype(tri.dtype), tri, preferred_element_type=F32)
    tot1 = jnp.sum(oh1, axis=1, keepdims=True)
    tot2 = jnp.sum(oh2, axis=1, keepdims=True)
    run = run_ref[:, 0:1]
    r1 = jnp.sum(oh1 * (run + c1), axis=0, keepdims=True)
    r2 = jnp.sum(oh2 * (run + tot1 + c2), axis=0, keepdims=True)
    run = run + tot1 + tot2
    run_ref[...] = jnp.broadcast_to(run, run_ref.shape)

    e_ref[...] = jnp.concatenate([i1, i2], axis=0)
    rank_ref[...] = jnp.concatenate([r1, r2], axis=0).astype(jnp.int32)
    gate_ref[...] = jnp.concatenate([g1, g2], axis=0)
    cnt_ref[...] = jnp.broadcast_to(run, cnt_ref.shape).astype(jnp.int32)


def _router(xb, w_router_t):
    m, k = xb.shape
    n_exp = w_router_t.shape[0]
    tm = _tile(m, 512)
    tri = (jnp.arange(tm)[:, None] < jnp.arange(tm)[None, :]).astype(BF16)
    pair = lambda dt: jax.ShapeDtypeStruct((TOP_K, m), dt)
    pair_spec = pl.BlockSpec((TOP_K, tm), lambda i: (0, i))
    return pl.pallas_call(
        _router_kernel,
        grid=(m // tm,),
        in_specs=[pl.BlockSpec((tm, k), lambda i: (i, 0)),
                  pl.BlockSpec((n_exp, k), lambda i: (0, 0)),
                  pl.BlockSpec((tm, tm), lambda i: (0, 0))],
        out_specs=[pair_spec, pair_spec, pair_spec, pl.BlockSpec((n_exp, LANES), lambda i: (0, 0))],
        out_shape=[pair(jnp.int32), pair(jnp.int32), pair(F32),
                   jax.ShapeDtypeStruct((n_exp, LANES), jnp.int32)],
        scratch_shapes=[pltpu.VMEM((n_exp, LANES), F32)],
        compiler_params=_params(("arbitrary",), 32),
        name="router",
    )(xb, w_router_t, tri)


def _tile_positions(pos, tm):
    return pos.reshape(TOP_K, -1, tm).transpose(1, 0, 2)


def _dispatch_kernel(fill_ref, pos_ref, x_ref, xs_ref, stage_ref, zero_ref, sem, fill_sem):
    i = pl.program_id(0)
    tm = x_ref.shape[0]
    slot = lax.rem(i, 2)

    @pl.when(i == 0)
    def _():
        zero_ref[...] = jnp.zeros_like(zero_ref)
        fill_rows = zero_ref.shape[0]
        n_groups = fill_ref.shape[0] - 1

        def fill(start):
            return pltpu.make_async_copy(
                zero_ref, xs_ref.at[pl.ds(pl.multiple_of(start, SUBLANES), fill_rows), :], fill_sem)

        fills = [fill(fill_ref[e]) for e in range(n_groups)]
        for c in fills:
            c.start()
        for c in fills:
            c.wait()

        def fill_unused(t, carry):
            c = fill(t * fill_rows)
            c.start()
            c.wait()
            return carry
        lax.fori_loop(fill_ref[n_groups] // fill_rows, xs_ref.shape[0] // fill_rows, fill_unused, 0)

    def row_copy(s, r, k):
        return pltpu.make_async_copy(stage_ref.at[s, pl.ds(r, 1), :],
                                     xs_ref.at[pl.ds(pos_ref[k, r], 1), :], sem.at[s])

    def start(r, carry):
        for k in range(TOP_K):
            row_copy(slot, r, k).start()
        return carry

    def drain(s):
        def wait(r, carry):
            for k in range(TOP_K):
                pltpu.make_async_copy(stage_ref.at[s, pl.ds(r, 1), :],
                                      xs_ref.at[pl.ds(0, 1), :], sem.at[s]).wait()
            return carry
        lax.fori_loop(0, tm, wait, 0, unroll=ROW_COPY_UNROLL)

    stage_ref[slot] = x_ref[...]
    lax.fori_loop(0, tm, start, 0, unroll=ROW_COPY_UNROLL)

    @pl.when(i > 0)
    def _():
        drain(1 - slot)

    @pl.when(i == pl.num_programs(0) - 1)
    def _():
        drain(slot)


def _dispatch(fill_starts, pos, xp, n_slots, fill_rows):
    m, w = xp.shape
    tm = _tile(m, 256)
    grid_spec = pltpu.PrefetchScalarGridSpec(
        num_scalar_prefetch=1,
        grid=(m // tm,),
        in_specs=[pl.BlockSpec((None, TOP_K, tm), lambda i, fill: (i, 0, 0), memory_space=pltpu.SMEM),
                  pl.BlockSpec((tm, w), lambda i, fill: (i, 0))],
        out_specs=pl.BlockSpec(memory_space=pl.ANY),
        scratch_shapes=[pltpu.VMEM((2, tm, w), xp.dtype), pltpu.VMEM((fill_rows, w), xp.dtype),
                        pltpu.SemaphoreType.DMA((2,)), pltpu.SemaphoreType.DMA(())],
    )
    return pl.pallas_call(
        _dispatch_kernel,
        grid_spec=grid_spec,
        out_shape=jax.ShapeDtypeStruct((n_slots, w), xp.dtype),
        compiler_params=_params(("arbitrary",), 32, disable_bounds_checks=True),
        name="dispatch",
    )(fill_starts, _tile_positions(pos, tm), xp)


def _cast_expert_weights(first_ref, pairs):
    @pl.when(first_ref[pl.program_id(1)] > 0)
    def _():
        for src, dst in pairs:
            dst[...] = src[...].astype(dst.dtype)


def _expert_up_kernel(te_ref, tr_ref, tv_ref, tf_ref, xs_ref, w1_ref, w3_ref, w2_ref, h_ref, w2b_ref,
                      w1b_ref, w3b_ref):
    del te_ref, tr_ref
    _cast_expert_weights(tf_ref, [(w1_ref, w1b_ref), (w3_ref, w3b_ref), (w2_ref, w2b_ref)])

    @pl.when(tv_ref[pl.program_id(1)] > 0)
    def _():
        x = _unpack_bf16_pairs(xs_ref[...])
        a = jnp.dot(x, w1b_ref[...], preferred_element_type=F32)
        b = jnp.dot(x, w3b_ref[...], preferred_element_type=F32)
        h_ref[...] = (a * jax.nn.sigmoid(a) * b).astype(h_ref.dtype)

    @pl.when(tv_ref[pl.program_id(1)] == 0)
    def _():
        h_ref[...] = jnp.zeros_like(h_ref)


def _expert_specs():
    rows = lambda c, i, te, tr, tv, tf: (tr[i], 0)
    weights = lambda c, i, te, tr, tv, tf: (te[i], 0, c)
    out = lambda c, i, te, tr, tv, tf: (i, c)
    return rows, weights, out


def _expert_up(tiles, xs, w1, w3, w2, tm):
    n_slots = xs.shape[0]
    n_exp, k, n = w1.shape
    d = w2.shape[2]
    tn = _tile(n, 512)
    rows, weights, out = _expert_specs()
    down_rows = lambda c, i, te, tr, tv, tf: (te[i], c, 0)
    grid_spec = pltpu.PrefetchScalarGridSpec(
        num_scalar_prefetch=4,
        grid=(n // tn, n_slots // tm),
        in_specs=[pl.BlockSpec((tm, k // 2), rows),
                  pl.BlockSpec((None, k, tn), weights),
                  pl.BlockSpec((None, k, tn), weights),
                  pl.BlockSpec((None, tn, d), down_rows)],
        out_specs=[pl.BlockSpec((tm, tn), out), pl.BlockSpec((None, tn, d), down_rows)],
        scratch_shapes=[pltpu.VMEM((k, tn), BF16), pltpu.VMEM((k, tn), BF16)],
    )
    return pl.pallas_call(
        _expert_up_kernel,
        grid_spec=grid_spec,
        out_shape=[jax.ShapeDtypeStruct((n_slots, n), BF16),
                   jax.ShapeDtypeStruct((n_exp, n, d), BF16)],
        compiler_params=_params(("parallel", "arbitrary"), 56),
        name="expert_up",
    )(*tiles, xs, w1, w3, w2)


def _expert_down_kernel(te_ref, tr_ref, tv_ref, tf_ref, h_ref, w2_ref, y_ref):
    del te_ref, tr_ref, tf_ref

    @pl.when(tv_ref[pl.program_id(1)] > 0)
    def _():
        y_ref[...] = jnp.dot(h_ref[...], w2_ref[...], preferred_element_type=F32)

    @pl.when(tv_ref[pl.program_id(1)] == 0)
    def _():
        y_ref[...] = jnp.zeros_like(y_ref)


def _expert_down(tiles, h, w2b, tm):
    n_slots, k = h.shape
    n = w2b.shape[2]
    tn = _tile(n, 1024)
    rows, weights, out = _expert_specs()
    grid_spec = pltpu.PrefetchScalarGridSpec(
        num_scalar_prefetch=4,
        grid=(n // tn, n_slots // tm),
        in_specs=[pl.BlockSpec((tm, k), rows),
                  pl.BlockSpec((None, k, tn), weights, pipeline_mode=pl.Buffered(1))],
        out_specs=pl.BlockSpec((tm, tn), out),
    )
    return pl.pallas_call(
        _expert_down_kernel,
        grid_spec=grid_spec,
        out_shape=jax.ShapeDtypeStruct((n_slots, n), F32),
        compiler_params=_params(("parallel", "arbitrary"), 56),
        name="expert_down",
    )(*tiles, h, w2b)


def _gather_mix_ln_kernel(pos_ref, next_pos_ref, ys_ref, gate_ref, x_ref, g_ref, b_ref, wg_ref,
                          bg_ref, p_ref, wp_ref, o32_ref, o16_ref, buf_ref, sem, *, alpha):
    i = pl.program_id(0)
    tm = x_ref.shape[0]
    slot = lax.rem(i, 2)

    def issue(idx_ref, s):
        def start(r, carry):
            for k in range(TOP_K):
                pltpu.make_async_copy(ys_ref.at[pl.ds(idx_ref[k, r], 1), :],
                                      buf_ref.at[s, k, pl.ds(r, 1), :], sem.at[s]).start()
            return carry
        lax.fori_loop(0, tm, start, 0, unroll=ROW_COPY_UNROLL)

    def wait(r, carry):
        for k in range(TOP_K):
            pltpu.make_async_copy(ys_ref.at[pl.ds(0, 1), :],
                                  buf_ref.at[slot, k, pl.ds(r, 1), :], sem.at[slot]).wait()
        return carry

    @pl.when(i == 0)
    def _():
        issue(pos_ref, slot)

    @pl.when(i + 1 < pl.num_programs(0))
    def _():
        issue(next_pos_ref, 1 - slot)

    lax.fori_loop(0, tm, wait, 0, unroll=ROW_COPY_UNROLL)
    gates = gate_ref[...]
    f = gates[:, 0:1] * buf_ref[slot, 0] + gates[:, 1:2] * buf_ref[slot, 1]
    y = _layer_norm(alpha * x_ref[...] + f, g_ref[...], b_ref[...])
    y = _gated_embedding(y, wg_ref, bg_ref, p_ref, wp_ref)
    o32_ref[...] = y
    o16_ref[...] = y.astype(o16_ref.dtype)


def _gather_mix_ln(pos, ys, gates, x_res, g, b, alpha, w_gate, b_gate, pb, w_proj):
    m, d = x_res.shape
    kp = pb.shape[1]
    tm = _tile(m, 256)
    last = m // tm - 1
    row = lambda i: (i, 0)
    fixed = lambda i: (0, 0)
    pos_tiles = _tile_positions(pos, tm)
    return pl.pallas_call(
        functools.partial(_gather_mix_ln_kernel, alpha=alpha),
        grid=(m // tm,),
        in_specs=[pl.BlockSpec((None, TOP_K, tm), lambda i: (i, 0, 0), memory_space=pltpu.SMEM),
                  pl.BlockSpec((None, TOP_K, tm), lambda i: (jnp.minimum(i + 1, last), 0, 0),
                               memory_space=pltpu.SMEM),
                  pl.BlockSpec(memory_space=pl.ANY),
                  pl.BlockSpec((tm, TOP_K), row),
                  pl.BlockSpec((tm, d), row),
                  pl.BlockSpec((1, d), fixed), pl.BlockSpec((1, d), fixed)]
        + _ple_specs(tm, d, kp, row, fixed),
        out_specs=[pl.BlockSpec((tm, d), row), pl.BlockSpec((tm, d), row)],
        out_shape=[jax.ShapeDtypeStruct((m, d), F32), jax.ShapeDtypeStruct((m, d), BF16)],
        scratch_shapes=[pltpu.VMEM((2, TOP_K, tm, d), F32), pltpu.SemaphoreType.DMA((2,))],
        compiler_params=_params(("arbitrary",), 48, disable_bounds_checks=True),
        name="gather_mix_ln",
    )(pos_tiles, pos_tiles, ys, gates, x_res, g.reshape(1, d), b.reshape(1, d),
      w_gate, b_gate.reshape(1, d), pb, w_proj)


def _moe_ffn_ln(x32, xb, x_words, w_router, w1, w3, w2, g, b, alpha, ple):
    m, _ = xb.shape
    n_exp = w1.shape[0]
    tm = _tile(m, 1024)
    experts, ranks, gates, counts = _router(xb, w_router.T.astype(BF16))

    counts = counts[:, 0]
    padded = (counts + tm - 1) // tm * tm
    ends = jnp.cumsum(padded)
    starts = ends - padded
    expert_ids = jnp.arange(n_exp, dtype=jnp.int32)[:, None, None]
    pos = jnp.sum(jnp.where(experts[None] == expert_ids, starts[:, None, None], 0), axis=0) + ranks
    assert (TOP_K * m) % tm == 0
    n_slots = TOP_K * m + n_exp * tm
    tile_index = jnp.arange(n_slots // tm, dtype=jnp.int32)
    tile_valid = (tile_index * tm < ends[-1]).astype(jnp.int32)
    tile_row = jnp.minimum(tile_index, jnp.maximum(ends[-1] // tm - 1, 0))
    tile_expert = jnp.sum((ends[None, :] <= (tile_row * tm)[:, None]).astype(jnp.int32), axis=1)
    tile_expert = jnp.minimum(tile_expert, n_exp - 1)
    prev_expert = jnp.concatenate([jnp.full((1,), -1, jnp.int32), tile_expert[:-1]])
    tile_first = (tile_expert != prev_expert).astype(jnp.int32)
    tiles = (tile_expert, tile_row, tile_valid, tile_first)

    fill_starts = (starts + counts) // SUBLANES * SUBLANES
    xs = _dispatch(jnp.concatenate([fill_starts, ends[-1:]]), pos, x_words, n_slots, tm)
    h, w2b = _expert_up(tiles, xs, w1, w3, w2, tm)
    ys = _expert_down(tiles, h, w2b, tm)
    return _gather_mix_ln(pos, ys, gates.T, x32, g, b, alpha, *ple)


def kernel(x, p, attn_w_qkv, attn_w_o, pool_w_in, pool_w_group, pool_scale, pool_w_o,
           ln_mix_g, ln_mix_b, ln_ffn_g, ln_ffn_b, ffn_w1, ffn_w3, ffn_w2,
           moe_router, moe_w1, moe_w3, moe_w2, ple_w_proj, ple_w_gate, ple_b_gate):
    batch, seq, d = x.shape
    depth = p.shape[0]
    att_w = N_HEADS * LANES
    assert attn_w_qkv.shape[2] == N_GROUPS * 3 * att_w
    alpha = (2 * depth) ** 0.25
    m = batch * seq
    bf = lambda t: t.astype(BF16)

    x32 = x.reshape(m, d)
    xb = bf(x32)
    for i in range(depth):
        j = i // 2
        ple = (bf(ple_w_gate[i]), ple_b_gate[i], bf(p[i].reshape(m, -1)), bf(ple_w_proj[i]))
        if i % 2 == 0:
            outs, lses = [], []
            for grp, (_, dil) in enumerate(DILATED_GROUPS):
                x_g = _residue_major(xb, batch, seq, dil) if dil > 1 else xb
                qkv = _qkv_rope(x_g, attn_w_qkv[j], _rope_tables(seq, dil), grp, seq, att_w)
                o_g, lse_g = _dilated_attention(qkv, grp, seq, att_w)
                outs.append(o_g)
                lses.append(lse_g)
            x32, xb = _attn_out(outs, lses, bf(attn_w_o[j]), x32, ln_mix_g[i], ln_mix_b[i],
                                alpha, batch, seq)
            h = _swiglu_up(xb, ffn_w1[j], ffn_w3[j])
            x32, xb = _matmul_res_ln_ple(h, bf(ffn_w2[j]), x32, ln_ffn_g[i], ln_ffn_b[i], alpha, *ple)
        else:
            u = _matmul(xb, bf(pool_w_in[j]), F32)
            y = _pool_mix(u, bf(pool_w_group[j]), pool_scale[j].reshape(-1), batch, seq)
            x32, xb, x_words = _matmul_res_ln(y, bf(pool_w_o[j]), x32, ln_mix_g[i], ln_mix_b[i], alpha)
            x32, xb = _moe_ffn_ln(x32, xb, x_words, moe_router[j], moe_w1[j], moe_w3[j], moe_w2[j],
                                  ln_ffn_g[i], ln_ffn_b[i], alpha, ple)
    return x32.reshape(batch, seq, d)
```

```python
import functools
import math

import jax
import jax.numpy as jnp
from jax import lax
from jax.experimental import pallas as pl
from jax.experimental.pallas import tpu as pltpu

F32 = jnp.float32
BF16 = jnp.bfloat16

N_HEADS = 16
DILATED_GROUPS = ((128, 1), (512, 4), (2048, 16))
N_GROUPS = len(DILATED_GROUPS)
ATT_BLOCK = 128
ATT_BLOCKS_PER_STEP = 4
ROPE_THETA = 10000.0
POOL_WINDOWS = (2, 4, 8, 16)
POOL_HALO = 16
TOP_K = 2
ROW_COPY_UNROLL = 4
LN_EPS = 1e-5
NEG_INF = -1e30

LANES = 128
SUBLANES = 8
MIB = 1024 * 1024


def _tile(dim, pref):
    t = min(dim, pref)
    while dim % t:
        t //= 2
    return t


def _params(semantics, vmem_mib, **kw):
    return pltpu.CompilerParams(dimension_semantics=semantics, vmem_limit_bytes=vmem_mib * MIB, **kw)


def _layer_norm(y, g, b):
    mu = jnp.mean(y, axis=-1, keepdims=True)
    yc = y - mu
    var = jnp.mean(yc * yc, axis=-1, keepdims=True)
    return yc * lax.rsqrt(var + LN_EPS) * g + b


def _pack_bf16_pairs(y):
    half = y.shape[1] // 2
    bits = lambda t: lax.bitcast_convert_type(t.astype(BF16).astype(F32), jnp.uint32)
    return (bits(y[:, :half]) >> 16) | bits(y[:, half:])


def _unpack_bf16_pairs(words):
    lo = lax.bitcast_convert_type(words << 16, F32)
    hi = lax.bitcast_convert_type(words & jnp.uint32(0xFFFF0000), F32)
    return jnp.concatenate([lo, hi], axis=1).astype(BF16)


def _residue_major(t, batch, seq, dil):
    d = t.shape[1]
    return t.reshape(batch, seq // dil, dil, d).transpose(0, 2, 1, 3).reshape(batch * seq, d)


def _qkv_rope_kernel(x_ref, w_ref, cos_ref, sin_ref, o_ref, wb_ref):
    @pl.when(pl.program_id(1) == 0)
    def _():
        wb_ref[...] = w_ref[...].astype(wb_ref.dtype)

    acc = jnp.dot(x_ref[...], wb_ref[...], preferred_element_type=F32)
    cos = cos_ref[...]
    sin = sin_ref[...]
    for h in range(acc.shape[1] // LANES):
        t = acc[:, h * LANES:(h + 1) * LANES]
        rot = pltpu.roll(t, LANES // 2, 1)
        o_ref[:, h * LANES:(h + 1) * LANES] = (t * cos + rot * sin).astype(o_ref.dtype)


def _qkv_rope(x_g, w, tables, group, seq, att_w):
    m, k = x_g.shape
    cos_tab, sin_tab = tables
    tm = _tile(seq, 1024)
    tn = _tile(att_w, 1024)
    s_tiles = seq // tm
    per_role = att_w // tn
    per_group = 3 * per_role
    table_spec = pl.BlockSpec((None, tm, LANES), lambda j, i: (j // per_role, i % s_tiles, 0))
    return pl.pallas_call(
        _qkv_rope_kernel,
        grid=(per_group, m // tm),
        in_specs=[pl.BlockSpec((tm, k), lambda j, i: (i, 0)),
                  pl.BlockSpec((k, tn), lambda j, i: (0, group * per_group + j)),
                  table_spec, table_spec],
        out_specs=pl.BlockSpec((tm, tn), lambda j, i: (i, j)),
        out_shape=jax.ShapeDtypeStruct((m, 3 * att_w), BF16),
        scratch_shapes=[pltpu.VMEM((k, tn), BF16)],
        compiler_params=_params(("parallel", "arbitrary"), 56),
        name=f"qkv_rope_g{group}",
    )(x_g, w, cos_tab, sin_tab)


def _rope_tables(seq, dil):
    half = LANES // 2
    inv = ROPE_THETA ** (-jnp.arange(half, dtype=F32) / half)
    row = jnp.arange(seq)
    n_sub = seq // dil
    pos = (row % n_sub) * dil + row // n_sub
    ang = pos.astype(F32)[:, None] * inv[None, :]
    cos, sin = jnp.cos(ang), jnp.sin(ang)
    cos_full = jnp.concatenate([cos, cos], axis=-1)
    sin_signed = jnp.concatenate([-sin, sin], axis=-1)
    return (jnp.stack([cos_full, cos_full, jnp.ones_like(cos_full)]),
            jnp.stack([sin_signed, sin_signed, jnp.zeros_like(cos_full)]))


def _attn_kernel(q_ref, kc_ref, kp_ref, vc_ref, vp_ref, o_ref, lse_ref, *, nb, n_back, scale):
    blk = kp_ref.shape[0]
    sub = q_ref.shape[0] // blk
    first_has_prev = lax.rem(pl.program_id(0) * sub, nb) > 0
    qi = lax.broadcasted_iota(jnp.int32, (blk, 2 * blk), 0) + blk
    ki = lax.broadcasted_iota(jnp.int32, (blk, 2 * blk), 1)
    dist = qi - ki
    band = (dist >= 0) & (dist <= n_back)
    bias_inner = jnp.where(band, 0.0, NEG_INF)
    bias_first = jnp.where(band & (first_has_prev | (ki >= blk)), 0.0, NEG_INF)
    lane = lax.broadcasted_iota(jnp.int32, (blk, LANES), 1)
    for t in range(sub):
        rows = slice(t * blk, (t + 1) * blk)
        before = slice((t - 1) * blk, t * blk)
        bias = bias_first if t == 0 else bias_inner
        lse_tile = jnp.zeros((blk, LANES), F32)
        for h in range(q_ref.shape[1] // LANES):
            hs = slice(h * LANES, (h + 1) * LANES)
            k_prev = kp_ref[:, hs] if t == 0 else kc_ref[before, hs]
            v_prev = vp_ref[:, hs] if t == 0 else vc_ref[before, hs]
            k2 = jnp.concatenate([k_prev, kc_ref[rows, hs]], axis=0)
            v2 = jnp.concatenate([v_prev, vc_ref[rows, hs]], axis=0)
            s = lax.dot_general(q_ref[rows, hs], k2, (((1,), (1,)), ((), ())),
                                preferred_element_type=F32) * scale + bias
            m = jnp.max(s, axis=-1, keepdims=True)
            e = jnp.exp(s - m)
            l = jnp.sum(e, axis=-1, keepdims=True)
            p = (e * (1.0 / l)).astype(v2.dtype)
            o_ref[rows, hs] = jnp.dot(p, v2, preferred_element_type=F32)
            lse_tile = jnp.where(lane == h, m + jnp.log(l), lse_tile)
        lse_ref[rows, :] = lse_tile


def _dilated_attention(qkv, group, seq, att_w):
    m = qkv.shape[0]
    window, dil = DILATED_GROUPS[group]
    nb = seq // (dil * ATT_BLOCK)
    sub = math.gcd(nb, ATT_BLOCKS_PER_STEP)
    assert seq % (dil * ATT_BLOCK) == 0 and window // dil <= ATT_BLOCK
    step_rows = sub * ATT_BLOCK

    def spec(role, prev):
        if prev:
            return pl.BlockSpec((ATT_BLOCK, att_w), lambda r: (jnp.maximum(sub * r - 1, 0), role))
        return pl.BlockSpec((step_rows, att_w), lambda r: (r, role))

    return pl.pallas_call(
        functools.partial(_attn_kernel, nb=nb, n_back=window // dil, scale=LANES ** -0.5),
        grid=(m // step_rows,),
        in_specs=[spec(0, False), spec(1, False), spec(1, True), spec(2, False), spec(2, True)],
        out_specs=[pl.BlockSpec((step_rows, att_w), lambda r: (r, 0)),
                   pl.BlockSpec((step_rows, LANES), lambda r: (r, 0))],
        out_shape=[jax.ShapeDtypeStruct((m, att_w), F32), jax.ShapeDtypeStruct((m, LANES), F32)],
        compiler_params=_params(("arbitrary",), 32),
        name=f"dilated_attn_g{group}",
    )(qkv, qkv, qkv, qkv, qkv)


def _token_order(ref, scr_ref):
    dil, n, w = ref.shape
    heads = range(w // LANES)
    if dil == 1:
        return [ref[0, :, h * LANES:(h + 1) * LANES] for h in heads]
    for r in range(dil):
        for h in heads:
            scr_ref[h, pl.ds(r, n, stride=dil), :] = ref[r, :, h * LANES:(h + 1) * LANES]
    return [scr_ref[h] for h in heads]


def _attn_out_kernel(*refs, alpha):
    o_refs, l_refs = refs[:N_GROUPS], refs[N_GROUPS:2 * N_GROUPS]
    w_ref, x_ref, g_ref, b_ref, o32_ref, o16_ref = refs[2 * N_GROUPS:2 * N_GROUPS + 6]
    scratch = refs[2 * N_GROUPS + 6:]
    o_scr, l_scr = scratch[:N_GROUPS], scratch[N_GROUPS:]
    outs = [_token_order(o_refs[g], o_scr[g]) for g in range(N_GROUPS)]
    lses = [_token_order(l_refs[g], l_scr[g])[0] for g in range(N_GROUPS)]
    m = functools.reduce(jnp.maximum, lses)
    es = [jnp.exp(l - m) for l in lses]
    inv = 1.0 / functools.reduce(jnp.add, es)
    wts = [e * inv for e in es]
    tm = x_ref.shape[0]
    heads = []
    for h in range(len(outs[0])):
        acc = None
        for g in range(N_GROUPS):
            term = jnp.broadcast_to(wts[g][:, h:h + 1], (tm, LANES)) * outs[g][h]
            acc = term if acc is None else acc + term
        heads.append(acc.astype(w_ref.dtype))
    mixed = jnp.concatenate(heads, axis=1)
    hid = jnp.dot(mixed, w_ref[...], preferred_element_type=F32)
    y = _layer_norm(alpha * x_ref[...] + hid, g_ref[...], b_ref[...])
    o32_ref[...] = y
    o16_ref[...] = y.astype(o16_ref.dtype)


def _attn_out(outs, lses, w_o, x_res, g, b, alpha, batch, seq):
    n_groups = len(outs)
    m, att_w = outs[0].shape
    d = w_o.shape[1]
    dils = [dil for _, dil in DILATED_GROUPS]
    tm = _tile(seq, 256)
    s_tiles = seq // tm
    assert all(tm % (8 * dil) == 0 for dil in dils)

    def group_view(t, g):
        return t.reshape(batch, dils[g], seq // dils[g], t.shape[-1])

    def group_spec(g, width):
        return pl.BlockSpec((None, dils[g], tm // dils[g], width),
                            lambda i: (i // s_tiles, 0, i % s_tiles, 0))

    row = lambda i: (i, 0)
    fixed = lambda i: (0, 0)
    return pl.pallas_call(
        functools.partial(_attn_out_kernel, alpha=alpha),
        grid=(m // tm,),
        in_specs=[group_spec(g, att_w) for g in range(n_groups)]
        + [group_spec(g, LANES) for g in range(n_groups)]
        + [pl.BlockSpec((att_w, d), fixed, pipeline_mode=pl.Buffered(1)),
           pl.BlockSpec((tm, d), row), pl.BlockSpec((1, d), fixed), pl.BlockSpec((1, d), fixed)],
        out_specs=[pl.BlockSpec((tm, d), row), pl.BlockSpec((tm, d), row)],
        out_shape=[jax.ShapeDtypeStruct((m, d), F32), jax.ShapeDtypeStruct((m, d), BF16)],
        scratch_shapes=[pltpu.VMEM((att_w // LANES, tm, LANES), F32) for _ in range(n_groups)]
        + [pltpu.VMEM((1, tm, LANES), F32) for _ in range(n_groups)],
        compiler_params=_params(("parallel",), 56),
        name="attn_out_ln",
    )(*[group_view(outs[g], g) for g in range(n_groups)],
      *[group_view(lses[g], g) for g in range(n_groups)],
      w_o, x_res, g.reshape(1, d), b.reshape(1, d))


def _gated_embedding(y, wg_ref, bg_ref, p_ref, wp_ref):
    z = jnp.dot(y.astype(wg_ref.dtype), wg_ref[...], preferred_element_type=F32) + bg_ref[...]
    e = jnp.dot(p_ref[...], wp_ref[...], preferred_element_type=F32)
    return y + jax.nn.sigmoid(z) * e


def _ple_specs(tm, d, kp, row, fixed):
    return [pl.BlockSpec((d, d), fixed, pipeline_mode=pl.Buffered(1)),
            pl.BlockSpec((1, d), fixed),
            pl.BlockSpec((tm, kp), row),
            pl.BlockSpec((kp, d), fixed, pipeline_mode=pl.Buffered(1))]


def _matmul_res_ln_ple_kernel(a_ref, w_ref, x_ref, g_ref, b_ref, wg_ref, bg_ref, p_ref, wp_ref,
                              o32_ref, o16_ref, *, alpha):
    h = jnp.dot(a_ref[...], w_ref[...], preferred_element_type=F32)
    y = _layer_norm(alpha * x_ref[...] + h, g_ref[...], b_ref[...])
    y = _gated_embedding(y, wg_ref, bg_ref, p_ref, wp_ref)
    o32_ref[...] = y
    o16_ref[...] = y.astype(o16_ref.dtype)


def _matmul_res_ln_ple(a, w, x_res, g, b, alpha, w_gate, b_gate, pb, w_proj):
    m, k = a.shape
    n = w.shape[1]
    kp = pb.shape[1]
    tm = _tile(m, 256)
    row = lambda i: (i, 0)
    fixed = lambda i: (0, 0)
    return pl.pallas_call(
        functools.partial(_matmul_res_ln_ple_kernel, alpha=alpha),
        grid=(m // tm,),
        in_specs=[pl.BlockSpec((tm, k), row),
                  pl.BlockSpec((k, n), fixed, pipeline_mode=pl.Buffered(1)),
                  pl.BlockSpec((tm, n), row), pl.BlockSpec((1, n), fixed), pl.BlockSpec((1, n), fixed)]
        + _ple_specs(tm, n, kp, row, fixed),
        out_specs=[pl.BlockSpec((tm, n), row), pl.BlockSpec((tm, n), row)],
        out_shape=[jax.ShapeDtypeStruct((m, n), F32), jax.ShapeDtypeStruct((m, n), BF16)],
        compiler_params=_params(("parallel",), 60),
        name="matmul_res_ln_ple",
    )(a, w, x_res, g.reshape(1, n), b.reshape(1, n), w_gate, b_gate.reshape(1, n), pb, w_proj)


def _swiglu_up_kernel(x_ref, w1_ref, w3_ref, o_ref, w1b_ref, w3b_ref):
    @pl.when(pl.program_id(1) == 0)
    def _():
        w1b_ref[...] = w1_ref[...].astype(w1b_ref.dtype)
        w3b_ref[...] = w3_ref[...].astype(w3b_ref.dtype)

    x = x_ref[...]
    a = jnp.dot(x, w1b_ref[...], preferred_element_type=F32)
    b = jnp.dot(x, w3b_ref[...], preferred_element_type=F32)
    o_ref[...] = (a * jax.nn.sigmoid(a) * b).astype(o_ref.dtype)


def _swiglu_up(xb, w1, w3):
    m, k = xb.shape
    n = w1.shape[1]
    tm, tn = _tile(m, 1024), _tile(n, 512)
    return pl.pallas_call(
        _swiglu_up_kernel,
        grid=(n // tn, m // tm),
        in_specs=[pl.BlockSpec((tm, k), lambda j, i: (i, 0)),
                  pl.BlockSpec((k, tn), lambda j, i: (0, j)),
                  pl.BlockSpec((k, tn), lambda j, i: (0, j))],
        out_specs=pl.BlockSpec((tm, tn), lambda j, i: (i, j)),
        out_shape=jax.ShapeDtypeStruct((m, n), BF16),
        scratch_shapes=[pltpu.VMEM((k, tn), BF16), pltpu.VMEM((k, tn), BF16)],
        compiler_params=_params(("parallel", "arbitrary"), 56),
        name="swiglu_up",
    )(xb, w1, w3)


def _pool_mixer_kernel(xb_ref, win_ref, wg_ref, sc_ref, wo_ref, x_ref, g_ref, b_ref,
                       o32_ref, o16_ref, opk_ref, buf_ref, *, alpha):
    s = pl.program_id(1)
    tm = xb_ref.shape[0]
    pd = wg_ref.shape[1]

    @pl.when(s == 0)
    def _():
        buf_ref[0:POOL_HALO, :] = jnp.zeros((POOL_HALO, buf_ref.shape[1]), F32)

    @pl.when(s > 0)
    def _():
        buf_ref[0:POOL_HALO, :] = buf_ref[tm:tm + POOL_HALO, :]

    buf_ref[POOL_HALO:POOL_HALO + tm, :] = jnp.dot(xb_ref[...], win_ref[...], preferred_element_type=F32)
    t = s * tm + lax.broadcasted_iota(jnp.int32, (tm, 1), 0)
    pieces = []
    for g, w in enumerate(POOL_WINDOWS):
        cols = slice(g * pd, (g + 1) * pd)
        u = buf_ref[POOL_HALO:POOL_HALO + tm, cols]
        tot = u
        for i in range(1, w):
            tot = tot + buf_ref[POOL_HALO - i:POOL_HALO - i + tm, cols]
        cnt = jnp.minimum(t + 1, w).astype(F32)
        mixed = (tot / cnt - u).astype(wg_ref.dtype)
        y = jnp.dot(mixed, wg_ref[g], preferred_element_type=F32) * sc_ref[:, cols]
        pieces.append(y.astype(wo_ref.dtype))
    hid = jnp.dot(jnp.concatenate(pieces, axis=1), wo_ref[...], preferred_element_type=F32)
    y = _layer_norm(alpha * x_ref[...] + hid, g_ref[...], b_ref[...])
    o32_ref[...] = y
    o16_ref[...] = y.astype(o16_ref.dtype)
    opk_ref[...] = _pack_bf16_pairs(y)


def _pool_mixer_ln(xb, x_res, w_in, w_group, scale, w_o, g, b, alpha, batch, seq):
    m, d = x_res.shape
    tm = _tile(seq, 256)
    s_tiles = seq // tm
    assert tm % SUBLANES == 0 and max(POOL_WINDOWS) <= POOL_HALO <= tm
    row = lambda bi, s: (bi * s_tiles + s, 0)
    fixed = lambda bi, s: (0, 0)
    resident = lambda shape, index: pl.BlockSpec(shape, index, pipeline_mode=pl.Buffered(1))
    return pl.pallas_call(
        functools.partial(_pool_mixer_kernel, alpha=alpha),
        grid=(batch, s_tiles),
        in_specs=[pl.BlockSpec((tm, d), row),
                  resident((d, d), fixed),
                  resident(w_group.shape, lambda bi, s: (0, 0, 0)),
                  pl.BlockSpec((1, d), fixed),
                  resident((d, d), fixed),
                  pl.BlockSpec((tm, d), row), pl.BlockSpec((1, d), fixed), pl.BlockSpec((1, d), fixed)],
        out_specs=[pl.BlockSpec((tm, d), row), pl.BlockSpec((tm, d), row),
                   pl.BlockSpec((tm, d // 2), row)],
        out_shape=[jax.ShapeDtypeStruct((m, d), F32), jax.ShapeDtypeStruct((m, d), BF16),
                   jax.ShapeDtypeStruct((m, d // 2), jnp.uint32)],
        scratch_shapes=[pltpu.VMEM((POOL_HALO + tm, d), F32)],
        compiler_params=_params(("parallel", "arbitrary"), 56),
        name="pool_mixer_ln",
    )(xb, w_in, w_group, scale.reshape(1, d), w_o, x_res, g.reshape(1, d), b.reshape(1, d))


def _router_kernel(x_ref, wr_ref, tri_ref, e_ref, rank_ref, gate_ref, cnt_ref, run_ref):
    @pl.when(pl.program_id(0) == 0)
    def _():
        run_ref[...] = jnp.zeros_like(run_ref)

    logits = lax.dot_general(wr_ref[...], x_ref[...], (((1,), (1,)), ((), ())),
                             preferred_element_type=F32)
    n_exp = logits.shape[0]
    eidx = lax.broadcasted_iota(jnp.int32, logits.shape, 0)
    m1 = jnp.max(logits, axis=0, keepdims=True)
    i1 = jnp.min(jnp.where(logits == m1, eidx, n_exp), axis=0, keepdims=True)
    rest = jnp.where(eidx == i1, -jnp.inf, logits)
    m2 = jnp.max(rest, axis=0, keepdims=True)
    i2 = jnp.min(jnp.where(rest == m2, eidx, n_exp), axis=0, keepdims=True)
    ex = jnp.exp(m2 - m1)
    g1 = 1.0 / (1.0 + ex)
    g2 = ex / (1.0 + ex)

    oh1 = (eidx == i1).astype(F32)
    oh2 = (eidx == i2).astype(F32)
    tri = tri_ref[...]
    c1 = jnp.dot(oh1.astype(tri.dtype), tri, preferred_element_type=F32)
    c2 = jnp.dot(oh2.astype(tri.dtype), tri, preferred_element_type=F32)
    tot1 = jnp.sum(oh1, axis=1, keepdims=True)
    tot2 = jnp.sum(oh2, axis=1, keepdims=True)
    run = run_ref[:, 0:1]
    r1 = jnp.sum(oh1 * (run + c1), axis=0, keepdims=True)
    r2 = jnp.sum(oh2 * (run + tot1 + c2), axis=0, keepdims=True)
    run = run + tot1 + tot2
    run_ref[...] = jnp.broadcast_to(run, run_ref.shape)

    e_ref[...] = jnp.concatenate([i1, i2], axis=0)
    rank_ref[...] = jnp.concatenate([r1, r2], axis=0).astype(jnp.int32)
    gate_ref[...] = jnp.concatenate([g1, g2], axis=0)
    cnt_ref[...] = jnp.broadcast_to(run, cnt_ref.shape).astype(jnp.int32)


def _router(xb, w_router_t):
    m, k = xb.shape
    n_exp = w_router_t.shape[0]
    tm = _tile(m, 512)
    tri = (jnp.arange(tm)[:, None] < jnp.arange(tm)[None, :]).astype(BF16)
    pair = lambda dt: jax.ShapeDtypeStruct((TOP_K, m), dt)
    pair_spec = pl.BlockSpec((TOP_K, tm), lambda i: (0, i))
    return pl.pallas_call(
        _router_kernel,
        grid=(m // tm,),
        in_specs=[pl.BlockSpec((tm, k), lambda i: (i, 0)),
                  pl.BlockSpec((n_exp, k), lambda i: (0, 0)),
                  pl.BlockSpec((tm, tm), lambda i: (0, 0))],
        out_specs=[pair_spec, pair_spec, pair_spec, pl.BlockSpec((n_exp, LANES), lambda i: (0, 0))],
        out_shape=[pair(jnp.int32), pair(jnp.int32), pair(F32),
                   jax.ShapeDtypeStruct((n_exp, LANES), jnp.int32)],
        scratch_shapes=[pltpu.VMEM((n_exp, LANES), F32)],
        compiler_params=_params(("arbitrary",), 32),
        name="router",
    )(xb, w_router_t, tri)


def _tile_positions(pos, tm):
    return pos.reshape(TOP_K, -1, tm).transpose(1, 0, 2)


def _dispatch_kernel(fill_ref, pos_ref, x_ref, xs_ref, stage_ref, zero_ref, sem, fill_sem):
    i = pl.program_id(0)
    tm = x_ref.shape[0]
    slot = lax.rem(i, 2)

    @pl.when(i == 0)
    def _():
        zero_ref[...] = jnp.zeros_like(zero_ref)
        fill_rows = zero_ref.shape[0]
        n_groups = fill_ref.shape[0] - 1

        def fill(start):
            return pltpu.make_async_copy(
                zero_ref, xs_ref.at[pl.ds(pl.multiple_of(start, SUBLANES), fill_rows), :], fill_sem)

        fills = [fill(fill_ref[e]) for e in range(n_groups)]
        for c in fills:
            c.start()
        for c in fills:
            c.wait()

        def fill_unused(t, carry):
            c = fill(t * fill_rows)
            c.start()
            c.wait()
            return carry
        lax.fori_loop(fill_ref[n_groups] // fill_rows, xs_ref.shape[0] // fill_rows, fill_unused, 0)

    def row_copy(s, r, k):
        return pltpu.make_async_copy(stage_ref.at[s, pl.ds(r, 1), :],
                                     xs_ref.at[pl.ds(pos_ref[k, r], 1), :], sem.at[s])

    def start(r, carry):
        for k in range(TOP_K):
            row_copy(slot, r, k).start()
        return carry

    def drain(s):
        def wait(r, carry):
            for k in range(TOP_K):
                pltpu.make_async_copy(stage_ref.at[s, pl.ds(r, 1), :],
                                      xs_ref.at[pl.ds(0, 1), :], sem.at[s]).wait()
            return carry
        lax.fori_loop(0, tm, wait, 0, unroll=ROW_COPY_UNROLL)

    stage_ref[slot] = x_ref[...]
    lax.fori_loop(0, tm, start, 0, unroll=ROW_COPY_UNROLL)

    @pl.when(i > 0)
    def _():
        drain(1 - slot)

    @pl.when(i == pl.num_programs(0) - 1)
    def _():
        drain(slot)


def _dispatch(fill_starts, pos, xp, n_slots, fill_rows):
    m, w = xp.shape
    tm = _tile(m, 256)
    grid_spec = pltpu.PrefetchScalarGridSpec(
        num_scalar_prefetch=1,
        grid=(m // tm,),
        in_specs=[pl.BlockSpec((None, TOP_K, tm), lambda i, fill: (i, 0, 0), memory_space=pltpu.SMEM),
                  pl.BlockSpec((tm, w), lambda i, fill: (i, 0))],
        out_specs=pl.BlockSpec(memory_space=pl.ANY),
        scratch_shapes=[pltpu.VMEM((2, tm, w), xp.dtype), pltpu.VMEM((fill_rows, w), xp.dtype),
                        pltpu.SemaphoreType.DMA((2,)), pltpu.SemaphoreType.DMA(())],
    )
    return pl.pallas_call(
        _dispatch_kernel,
        grid_spec=grid_spec,
        out_shape=jax.ShapeDtypeStruct((n_slots, w), xp.dtype),
        compiler_params=_params(("arbitrary",), 32, disable_bounds_checks=True),
        name="dispatch",
    )(fill_starts, _tile_positions(pos, tm), xp)


def _cast_expert_weights(first_ref, pairs):
    @pl.when(first_ref[pl.program_id(1)] > 0)
    def _():
        for src, dst in pairs:
            dst[...] = src[...].astype(dst.dtype)


def _expert_up_kernel(te_ref, tr_ref, tv_ref, tf_ref, xs_ref, w1_ref, w3_ref, h_ref, w1b_ref, w3b_ref):
    del te_ref, tr_ref
    _cast_expert_weights(tf_ref, [(w1_ref, w1b_ref), (w3_ref, w3b_ref)])

    @pl.when(tv_ref[pl.program_id(1)] > 0)
    def _():
        x = _unpack_bf16_pairs(xs_ref[...])
        a = jnp.dot(x, w1b_ref[...], preferred_element_type=F32)
        b = jnp.dot(x, w3b_ref[...], preferred_element_type=F32)
        h_ref[...] = (a * jax.nn.sigmoid(a) * b).astype(h_ref.dtype)

    @pl.when(tv_ref[pl.program_id(1)] == 0)
    def _():
        h_ref[...] = jnp.zeros_like(h_ref)


def _expert_specs():
    rows = lambda c, i, te, tr, tv, tf: (tr[i], 0)
    weights = lambda c, i, te, tr, tv, tf: (te[i], 0, c)
    out = lambda c, i, te, tr, tv, tf: (i, c)
    return rows, weights, out


def _expert_up(tiles, xs, w1, w3, tm):
    n_slots = xs.shape[0]
    _, k, n = w1.shape
    tn = _tile(n, 512)
    rows, weights, out = _expert_specs()
    weight_spec = pl.BlockSpec((None, k, tn), weights)
    grid_spec = pltpu.PrefetchScalarGridSpec(
        num_scalar_prefetch=4,
        grid=(n // tn, n_slots // tm),
        in_specs=[pl.BlockSpec((tm, k // 2), rows), weight_spec, weight_spec],
        out_specs=pl.BlockSpec((tm, tn), out),
        scratch_shapes=[pltpu.VMEM((k, tn), BF16), pltpu.VMEM((k, tn), BF16)],
    )
    return pl.pallas_call(
        _expert_up_kernel,
        grid_spec=grid_spec,
        out_shape=jax.ShapeDtypeStruct((n_slots, n), BF16),
        compiler_params=_params(("parallel", "arbitrary"), 56),
        name="expert_up",
    )(*tiles, xs, w1, w3)


def _expert_down_kernel(te_ref, tr_ref, tv_ref, tf_ref, h_ref, w2_ref, y_ref, w2b_ref):
    del te_ref, tr_ref
    _cast_expert_weights(tf_ref, [(w2_ref, w2b_ref)])

    @pl.when(tv_ref[pl.program_id(1)] > 0)
    def _():
        y_ref[...] = jnp.dot(h_ref[...], w2b_ref[...], preferred_element_type=F32)

    @pl.when(tv_ref[pl.program_id(1)] == 0)
    def _():
        y_ref[...] = jnp.zeros_like(y_ref)


def _expert_down(tiles, h, w2, tm):
    n_slots, k = h.shape
    n = w2.shape[2]
    tn = _tile(n, 512)
    rows, weights, out = _expert_specs()
    grid_spec = pltpu.PrefetchScalarGridSpec(
        num_scalar_prefetch=4,
        grid=(n // tn, n_slots // tm),
        in_specs=[pl.BlockSpec((tm, k), rows),
                  pl.BlockSpec((None, k, tn), weights, pipeline_mode=pl.Buffered(1))],
        out_specs=pl.BlockSpec((tm, tn), out),
        scratch_shapes=[pltpu.VMEM((k, tn), BF16)],
    )
    return pl.pallas_call(
        _expert_down_kernel,
        grid_spec=grid_spec,
        out_shape=jax.ShapeDtypeStruct((n_slots, n), F32),
        compiler_params=_params(("parallel", "arbitrary"), 56),
        name="expert_down",
    )(*tiles, h, w2)


def _gather_mix_ln_kernel(pos_ref, next_pos_ref, ys_ref, gate_ref, x_ref, g_ref, b_ref, wg_ref,
                          bg_ref, p_ref, wp_ref, o32_ref, o16_ref, buf_ref, sem, *, alpha):
    i = pl.program_id(0)
    tm = x_ref.shape[0]
    slot = lax.rem(i, 2)

    def issue(idx_ref, s):
        def start(r, carry):
            for k in range(TOP_K):
                pltpu.make_async_copy(ys_ref.at[pl.ds(idx_ref[k, r], 1), :],
                                      buf_ref.at[s, k, pl.ds(r, 1), :], sem.at[s]).start()
            return carry
        lax.fori_loop(0, tm, start, 0, unroll=ROW_COPY_UNROLL)

    def wait(r, carry):
        for k in range(TOP_K):
            pltpu.make_async_copy(ys_ref.at[pl.ds(0, 1), :],
                                  buf_ref.at[slot, k, pl.ds(r, 1), :], sem.at[slot]).wait()
        return carry

    @pl.when(i == 0)
    def _():
        issue(pos_ref, slot)

    @pl.when(i + 1 < pl.num_programs(0))
    def _():
        issue(next_pos_ref, 1 - slot)

    lax.fori_loop(0, tm, wait, 0, unroll=ROW_COPY_UNROLL)
    gates = gate_ref[...]
    f = gates[:, 0:1] * buf_ref[slot, 0] + gates[:, 1:2] * buf_ref[slot, 1]
    y = _layer_norm(alpha * x_ref[...] + f, g_ref[...], b_ref[...])
    y = _gated_embedding(y, wg_ref, bg_ref, p_ref, wp_ref)
    o32_ref[...] = y
    o16_ref[...] = y.astype(o16_ref.dtype)


def _gather_mix_ln(pos, ys, gates, x_res, g, b, alpha, w_gate, b_gate, pb, w_proj):
    m, d = x_res.shape
    kp = pb.shape[1]
    tm = _tile(m, 256)
    last = m // tm - 1
    row = lambda i: (i, 0)
    fixed = lambda i: (0, 0)
    pos_tiles = _tile_positions(pos, tm)
    return pl.pallas_call(
        functools.partial(_gather_mix_ln_kernel, alpha=alpha),
        grid=(m // tm,),
        in_specs=[pl.BlockSpec((None, TOP_K, tm), lambda i: (i, 0, 0), memory_space=pltpu.SMEM),
                  pl.BlockSpec((None, TOP_K, tm), lambda i: (jnp.minimum(i + 1, last), 0, 0),
                               memory_space=pltpu.SMEM),
                  pl.BlockSpec(memory_space=pl.ANY),
                  pl.BlockSpec((tm, TOP_K), row),
                  pl.BlockSpec((tm, d), row),
                  pl.BlockSpec((1, d), fixed), pl.BlockSpec((1, d), fixed)]
        + _ple_specs(tm, d, kp, row, fixed),
        out_specs=[pl.BlockSpec((tm, d), row), pl.BlockSpec((tm, d), row)],
        out_shape=[jax.ShapeDtypeStruct((m, d), F32), jax.ShapeDtypeStruct((m, d), BF16)],
        scratch_shapes=[pltpu.VMEM((2, TOP_K, tm, d), F32), pltpu.SemaphoreType.DMA((2,))],
        compiler_params=_params(("arbitrary",), 48, disable_bounds_checks=True),
        name="gather_mix_ln",
    )(pos_tiles, pos_tiles, ys, gates, x_res, g.reshape(1, d), b.reshape(1, d),
      w_gate, b_gate.reshape(1, d), pb, w_proj)


def _moe_ffn_ln(x32, xb, x_words, w_router, w1, w3, w2, g, b, alpha, ple):
    m, _ = xb.shape
    n_exp = w1.shape[0]
    tm = _tile(m, 1024)
    experts, ranks, gates, counts = _router(xb, w_router.T.astype(BF16))

    counts = counts[:, 0]
    padded = (counts + tm - 1) // tm * tm
    ends = jnp.cumsum(padded)
    starts = ends - padded
    expert_ids = jnp.arange(n_exp, dtype=jnp.int32)[:, None, None]
    pos = jnp.sum(jnp.where(experts[None] == expert_ids, starts[:, None, None], 0), axis=0) + ranks
    assert (TOP_K * m) % tm == 0
    n_slots = TOP_K * m + n_exp * tm
    tile_index = jnp.arange(n_slots // tm, dtype=jnp.int32)
    tile_valid = (tile_index * tm < ends[-1]).astype(jnp.int32)
    tile_row = jnp.minimum(tile_index, jnp.maximum(ends[-1] // tm - 1, 0))
    tile_expert = jnp.sum((ends[None, :] <= (tile_row * tm)[:, None]).astype(jnp.int32), axis=1)
    tile_expert = jnp.minimum(tile_expert, n_exp - 1)
    prev_expert = jnp.concatenate([jnp.full((1,), -1, jnp.int32), tile_expert[:-1]])
    tile_first = (tile_expert != prev_expert).astype(jnp.int32)
    tiles = (tile_expert, tile_row, tile_valid, tile_first)

    fill_starts = (starts + counts) // SUBLANES * SUBLANES
    xs = _dispatch(jnp.concatenate([fill_starts, ends[-1:]]), pos, x_words, n_slots, tm)
    h = _expert_up(tiles, xs, w1, w3, tm)
    ys = _expert_down(tiles, h, w2, tm)
    return _gather_mix_ln(pos, ys, gates.T, x32, g, b, alpha, *ple)


def kernel(x, p, attn_w_qkv, attn_w_o, pool_w_in, pool_w_group, pool_scale, pool_w_o,
           ln_mix_g, ln_mix_b, ln_ffn_g, ln_ffn_b, ffn_w1, ffn_w3, ffn_w2,
           moe_router, moe_w1, moe_w3, moe_w2, ple_w_proj, ple_w_gate, ple_b_gate):
    batch, seq, d = x.shape
    depth = p.shape[0]
    att_w = N_HEADS * LANES
    assert attn_w_qkv.shape[2] == N_GROUPS * 3 * att_w
    alpha = (2 * depth) ** 0.25
    m = batch * seq
    bf = lambda t: t.astype(BF16)

    x32 = x.reshape(m, d)
    xb = bf(x32)
    for i in range(depth):
        j = i // 2
        ple = (bf(ple_w_gate[i]), ple_b_gate[i], bf(p[i].reshape(m, -1)), bf(ple_w_proj[i]))
        if i % 2 == 0:
            outs, lses = [], []
            for grp, (_, dil) in enumerate(DILATED_GROUPS):
                x_g = _residue_major(xb, batch, seq, dil) if dil > 1 else xb
                qkv = _qkv_rope(x_g, attn_w_qkv[j], _rope_tables(seq, dil), grp, seq, att_w)
                o_g, lse_g = _dilated_attention(qkv, grp, seq, att_w)
                outs.append(o_g)
                lses.append(lse_g)
            x32, xb = _attn_out(outs, lses, bf(attn_w_o[j]), x32, ln_mix_g[i], ln_mix_b[i],
                                alpha, batch, seq)
            h = _swiglu_up(xb, ffn_w1[j], ffn_w3[j])
            x32, xb = _matmul_res_ln_ple(h, bf(ffn_w2[j]), x32, ln_ffn_g[i], ln_ffn_b[i], alpha, *ple)
        else:
            x32, xb, x_words = _pool_mixer_ln(
                xb, x32, bf(pool_w_in[j]), bf(pool_w_group[j]), pool_scale[j].reshape(-1),
                bf(pool_w_o[j]), ln_mix_g[i], ln_mix_b[i], alpha, batch, seq)
            x32, xb = _moe_ffn_ln(x32, xb, x_words, moe_router[j], moe_w1[j], moe_w3[j], moe_w2[j],
                                  ln_ffn_g[i], ln_ffn_b[i], alpha, ple)
    return x32.reshape(batch, seq, d)
```

```python
import functools
import math

import jax
import jax.numpy as jnp
from jax import lax
from jax.experimental import pallas as pl
from jax.experimental.pallas import tpu as pltpu

F32 = jnp.float32
BF16 = jnp.bfloat16

N_HEADS = 16
DILATED_GROUPS = ((128, 1), (512, 4), (2048, 16))
N_GROUPS = len(DILATED_GROUPS)
ATT_BLOCK = 128
ATT_BLOCKS_PER_STEP = 4
REGROUP_CHUNK = 256
ROPE_THETA = 10000.0
POOL_WINDOWS = (2, 4, 8, 16)
POOL_HALO = 16
TOP_K = 2
ROW_COPY_UNROLL = 4
LN_EPS = 1e-5
NEG_INF = -1e30

LANES = 128
SUBLANES = 8
MIB = 1024 * 1024


def _tile(dim, pref):
    t = min(dim, pref)
    while dim % t:
        t //= 2
    return t


def _params(semantics, vmem_mib, **kw):
    return pltpu.CompilerParams(dimension_semantics=semantics, vmem_limit_bytes=vmem_mib * MIB, **kw)


def _layer_norm(y, g, b):
    mu = jnp.mean(y, axis=-1, keepdims=True)
    yc = y - mu
    var = jnp.mean(yc * yc, axis=-1, keepdims=True)
    return yc * lax.rsqrt(var + LN_EPS) * g + b


def _pack_bf16_pairs(y):
    half = y.shape[1] // 2
    bits = lambda t: lax.bitcast_convert_type(t.astype(BF16).astype(F32), jnp.uint32)
    return (bits(y[:, :half]) >> 16) | bits(y[:, half:])


def _unpack_bf16_pairs(words):
    lo = lax.bitcast_convert_type(words << 16, F32)
    hi = lax.bitcast_convert_type(words & jnp.uint32(0xFFFF0000), F32)
    return jnp.concatenate([lo, hi], axis=1).astype(BF16)


def _regroup_kernel(x_ref, perm_ref, xb_ref, *group_refs, dils, chunk):
    xb = x_ref[...].astype(xb_ref.dtype)
    xb_ref[...] = xb
    for which, (dil, o_ref) in enumerate(zip(dils, group_refs)):
        slab = chunk // dil
        for ch in range(x_ref.shape[0] // chunk):
            grouped = jnp.dot(perm_ref[which], xb[ch * chunk:(ch + 1) * chunk],
                              preferred_element_type=F32).astype(o_ref.dtype)
            for r in range(dil):
                o_ref[r, ch * slab:(ch + 1) * slab, :] = grouped[r * slab:(r + 1) * slab]


def _cast_and_regroup(x32, batch, seq):
    m, d = x32.shape
    dils = tuple(dil for _, dil in DILATED_GROUPS if dil > 1)
    chunk = REGROUP_CHUNK
    tm = _tile(seq, 512)
    assert tm % chunk == 0 and all(chunk % (dil * 2 * SUBLANES) == 0 for dil in dils)
    src = jnp.arange(chunk)
    perms = jnp.stack([((src[:, None] % (chunk // dil)) * dil + src[:, None] // (chunk // dil)
                        == src[None, :]).astype(BF16) for dil in dils])
    s_tiles = seq // tm
    outs = pl.pallas_call(
        functools.partial(_regroup_kernel, dils=dils, chunk=chunk),
        grid=(batch, s_tiles),
        in_specs=[pl.BlockSpec((tm, d), lambda b, s: (b * s_tiles + s, 0)),
                  pl.BlockSpec(perms.shape, lambda b, s: (0, 0, 0))],
        out_specs=[pl.BlockSpec((tm, d), lambda b, s: (b * s_tiles + s, 0))]
        + [pl.BlockSpec((None, dil, tm // dil, d), lambda b, s: (b, 0, s, 0)) for dil in dils],
        out_shape=[jax.ShapeDtypeStruct((m, d), BF16)]
        + [jax.ShapeDtypeStruct((batch, dil, seq // dil, d), BF16) for dil in dils],
        compiler_params=_params(("parallel", "parallel"), 48),
        name="cast_and_regroup",
    )(x32, perms)
    by_dil = {1: outs[0], **{dil: o.reshape(m, d) for dil, o in zip(dils, outs[1:])}}
    return [by_dil[dil] for _, dil in DILATED_GROUPS]


def _qkv_rope_kernel(x_ref, w_ref, cos_ref, sin_ref, o_ref, wb_ref):
    @pl.when(pl.program_id(1) == 0)
    def _():
        wb_ref[...] = w_ref[...].astype(wb_ref.dtype)

    acc = jnp.dot(x_ref[...], wb_ref[...], preferred_element_type=F32)
    cos = cos_ref[...]
    sin = sin_ref[...]
    for h in range(acc.shape[1] // LANES):
        t = acc[:, h * LANES:(h + 1) * LANES]
        rot = pltpu.roll(t, LANES // 2, 1)
        o_ref[:, h * LANES:(h + 1) * LANES] = (t * cos + rot * sin).astype(o_ref.dtype)


def _qkv_rope(x_g, w, tables, group, seq, att_w):
    m, k = x_g.shape
    cos_tab, sin_tab = tables
    tm = _tile(seq, 1024)
    tn = _tile(att_w, 1024)
    s_tiles = seq // tm
    per_role = att_w // tn
    per_group = 3 * per_role
    table_spec = pl.BlockSpec((None, tm, LANES), lambda j, i: (j // per_role, i % s_tiles, 0))
    return pl.pallas_call(
        _qkv_rope_kernel,
        grid=(per_group, m // tm),
        in_specs=[pl.BlockSpec((tm, k), lambda j, i: (i, 0)),
                  pl.BlockSpec((k, tn), lambda j, i: (0, group * per_group + j)),
                  table_spec, table_spec],
        out_specs=pl.BlockSpec((tm, tn), lambda j, i: (i, j)),
        out_shape=jax.ShapeDtypeStruct((m, 3 * att_w), BF16),
        scratch_shapes=[pltpu.VMEM((k, tn), BF16)],
        compiler_params=_params(("parallel", "arbitrary"), 56),
        name=f"qkv_rope_g{group}",
    )(x_g, w, cos_tab, sin_tab)


def _rope_tables(seq, dil):
    half = LANES // 2
    inv = ROPE_THETA ** (-jnp.arange(half, dtype=F32) / half)
    row = jnp.arange(seq)
    n_sub = seq // dil
    pos = (row % n_sub) * dil + row // n_sub
    ang = pos.astype(F32)[:, None] * inv[None, :]
    cos, sin = jnp.cos(ang), jnp.sin(ang)
    cos_full = jnp.concatenate([cos, cos], axis=-1)
    sin_signed = jnp.concatenate([-sin, sin], axis=-1)
    return (jnp.stack([cos_full, cos_full, jnp.ones_like(cos_full)]),
            jnp.stack([sin_signed, sin_signed, jnp.zeros_like(cos_full)]))


def _attn_kernel(q_ref, kc_ref, kp_ref, vc_ref, vp_ref, o_ref, lse_ref, *, nb, n_back, scale):
    blk = kp_ref.shape[0]
    sub = q_ref.shape[0] // blk
    first_has_prev = lax.rem(pl.program_id(0) * sub, nb) > 0
    qi = lax.broadcasted_iota(jnp.int32, (blk, 2 * blk), 0) + blk
    ki = lax.broadcasted_iota(jnp.int32, (blk, 2 * blk), 1)
    dist = qi - ki
    band = (dist >= 0) & (dist <= n_back)
    bias_inner = jnp.where(band, 0.0, NEG_INF)
    bias_first = jnp.where(band & (first_has_prev | (ki >= blk)), 0.0, NEG_INF)
    lane = lax.broadcasted_iota(jnp.int32, (blk, LANES), 1)
    for t in range(sub):
        rows = slice(t * blk, (t + 1) * blk)
        before = slice((t - 1) * blk, t * blk)
        bias = bias_first if t == 0 else bias_inner
        lse_tile = jnp.zeros((blk, LANES), F32)
        for h in range(q_ref.shape[1] // LANES):
            hs = slice(h * LANES, (h + 1) * LANES)
            k_prev = kp_ref[:, hs] if t == 0 else kc_ref[before, hs]
            v_prev = vp_ref[:, hs] if t == 0 else vc_ref[before, hs]
            k2 = jnp.concatenate([k_prev, kc_ref[rows, hs]], axis=0)
            v2 = jnp.concatenate([v_prev, vc_ref[rows, hs]], axis=0)
            s = lax.dot_general(q_ref[rows, hs], k2, (((1,), (1,)), ((), ())),
                                preferred_element_type=F32) * scale + bias
            m = jnp.max(s, axis=-1, keepdims=True)
            e = jnp.exp(s - m)
            l = jnp.sum(e, axis=-1, keepdims=True)
            p = (e * (1.0 / l)).astype(v2.dtype)
            o_ref[rows, hs] = jnp.dot(p, v2, preferred_element_type=F32)
            lse_tile = jnp.where(lane == h, m + jnp.log(l), lse_tile)
        lse_ref[rows, :] = lse_tile


def _dilated_attention(qkv, group, seq, att_w):
    m = qkv.shape[0]
    window, dil = DILATED_GROUPS[group]
    nb = seq // (dil * ATT_BLOCK)
    sub = math.gcd(nb, ATT_BLOCKS_PER_STEP)
    assert seq % (dil * ATT_BLOCK) == 0 and window // dil <= ATT_BLOCK
    step_rows = sub * ATT_BLOCK

    def spec(role, prev):
        if prev:
            return pl.BlockSpec((ATT_BLOCK, att_w), lambda r: (jnp.maximum(sub * r - 1, 0), role))
        return pl.BlockSpec((step_rows, att_w), lambda r: (r, role))

    return pl.pallas_call(
        functools.partial(_attn_kernel, nb=nb, n_back=window // dil, scale=LANES ** -0.5),
        grid=(m // step_rows,),
        in_specs=[spec(0, False), spec(1, False), spec(1, True), spec(2, False), spec(2, True)],
        out_specs=[pl.BlockSpec((step_rows, att_w), lambda r: (r, 0)),
                   pl.BlockSpec((step_rows, LANES), lambda r: (r, 0))],
        out_shape=[jax.ShapeDtypeStruct((m, att_w), F32), jax.ShapeDtypeStruct((m, LANES), F32)],
        compiler_params=_params(("arbitrary",), 32),
        name=f"dilated_attn_g{group}",
    )(qkv, qkv, qkv, qkv, qkv)


def _token_order(ref, scr_ref):
    dil, n, w = ref.shape
    heads = range(w // LANES)
    if dil == 1:
        return [ref[0, :, h * LANES:(h + 1) * LANES] for h in heads]
    for r in range(dil):
        for h in heads:
            scr_ref[h, pl.ds(r, n, stride=dil), :] = ref[r, :, h * LANES:(h + 1) * LANES]
    return [scr_ref[h] for h in heads]


def _attn_out_kernel(*refs, alpha):
    o_refs, l_refs = refs[:N_GROUPS], refs[N_GROUPS:2 * N_GROUPS]
    w_ref, x_ref, g_ref, b_ref, o32_ref, o16_ref = refs[2 * N_GROUPS:2 * N_GROUPS + 6]
    scratch = refs[2 * N_GROUPS + 6:]
    o_scr, l_scr = scratch[:N_GROUPS], scratch[N_GROUPS:]
    outs = [_token_order(o_refs[g], o_scr[g]) for g in range(N_GROUPS)]
    lses = [_token_order(l_refs[g], l_scr[g])[0] for g in range(N_GROUPS)]
    m = functools.reduce(jnp.maximum, lses)
    es = [jnp.exp(l - m) for l in lses]
    inv = 1.0 / functools.reduce(jnp.add, es)
    wts = [e * inv for e in es]
    tm = x_ref.shape[0]
    heads = []
    for h in range(len(outs[0])):
        acc = None
        for g in range(N_GROUPS):
            term = jnp.broadcast_to(wts[g][:, h:h + 1], (tm, LANES)) * outs[g][h]
            acc = term if acc is None else acc + term
        heads.append(acc.astype(w_ref.dtype))
    mixed = jnp.concatenate(heads, axis=1)
    hid = jnp.dot(mixed, w_ref[...], preferred_element_type=F32)
    y = _layer_norm(alpha * x_ref[...] + hid, g_ref[...], b_ref[...])
    o32_ref[...] = y
    o16_ref[...] = y.astype(o16_ref.dtype)


def _attn_out(outs, lses, w_o, x_res, g, b, alpha, batch, seq):
    n_groups = len(outs)
    m, att_w = outs[0].shape
    d = w_o.shape[1]
    dils = [dil for _, dil in DILATED_GROUPS]
    tm = _tile(seq, 256)
    s_tiles = seq // tm
    assert all(tm % (8 * dil) == 0 for dil in dils)

    def group_view(t, g):
        return t.reshape(batch, dils[g], seq // dils[g], t.shape[-1])

    def group_spec(g, width):
        return pl.BlockSpec((None, dils[g], tm // dils[g], width),
                            lambda i: (i // s_tiles, 0, i % s_tiles, 0))

    row = lambda i: (i, 0)
    fixed = lambda i: (0, 0)
    return pl.pallas_call(
        functools.partial(_attn_out_kernel, alpha=alpha),
        grid=(m // tm,),
        in_specs=[group_spec(g, att_w) for g in range(n_groups)]
        + [group_spec(g, LANES) for g in range(n_groups)]
        + [pl.BlockSpec((att_w, d), fixed, pipeline_mode=pl.Buffered(1)),
           pl.BlockSpec((tm, d), row), pl.BlockSpec((1, d), fixed), pl.BlockSpec((1, d), fixed)],
        out_specs=[pl.BlockSpec((tm, d), row), pl.BlockSpec((tm, d), row)],
        out_shape=[jax.ShapeDtypeStruct((m, d), F32), jax.ShapeDtypeStruct((m, d), BF16)],
        scratch_shapes=[pltpu.VMEM((att_w // LANES, tm, LANES), F32) for _ in range(n_groups)]
        + [pltpu.VMEM((1, tm, LANES), F32) for _ in range(n_groups)],
        compiler_params=_params(("parallel",), 56),
        name="attn_out_ln",
    )(*[group_view(outs[g], g) for g in range(n_groups)],
      *[group_view(lses[g], g) for g in range(n_groups)],
      w_o, x_res, g.reshape(1, d), b.reshape(1, d))


def _gated_embedding(y, wg_ref, bg_ref, p_ref, wp_ref):
    z = jnp.dot(y.astype(wg_ref.dtype), wg_ref[...], preferred_element_type=F32) + bg_ref[...]
    e = jnp.dot(p_ref[...], wp_ref[...], preferred_element_type=F32)
    return y + jax.nn.sigmoid(z) * e


def _ple_specs(tm, d, kp, row, fixed):
    return [pl.BlockSpec((d, d), fixed, pipeline_mode=pl.Buffered(1)),
            pl.BlockSpec((1, d), fixed),
            pl.BlockSpec((tm, kp), row),
            pl.BlockSpec((kp, d), fixed, pipeline_mode=pl.Buffered(1))]


def _matmul_res_ln_ple_kernel(a_ref, w_ref, x_ref, g_ref, b_ref, wg_ref, bg_ref, p_ref, wp_ref,
                              o32_ref, o16_ref, *, alpha):
    h = jnp.dot(a_ref[...], w_ref[...], preferred_element_type=F32)
    y = _layer_norm(alpha * x_ref[...] + h, g_ref[...], b_ref[...])
    y = _gated_embedding(y, wg_ref, bg_ref, p_ref, wp_ref)
    o32_ref[...] = y
    o16_ref[...] = y.astype(o16_ref.dtype)


def _matmul_res_ln_ple(a, w, x_res, g, b, alpha, w_gate, b_gate, pb, w_proj):
    m, k = a.shape
    n = w.shape[1]
    kp = pb.shape[1]
    tm = _tile(m, 256)
    row = lambda i: (i, 0)
    fixed = lambda i: (0, 0)
    return pl.pallas_call(
        functools.partial(_matmul_res_ln_ple_kernel, alpha=alpha),
        grid=(m // tm,),
        in_specs=[pl.BlockSpec((tm, k), row),
                  pl.BlockSpec((k, n), fixed, pipeline_mode=pl.Buffered(1)),
                  pl.BlockSpec((tm, n), row), pl.BlockSpec((1, n), fixed), pl.BlockSpec((1, n), fixed)]
        + _ple_specs(tm, n, kp, row, fixed),
        out_specs=[pl.BlockSpec((tm, n), row), pl.BlockSpec((tm, n), row)],
        out_shape=[jax.ShapeDtypeStruct((m, n), F32), jax.ShapeDtypeStruct((m, n), BF16)],
        compiler_params=_params(("parallel",), 60),
        name="matmul_res_ln_ple",
    )(a, w, x_res, g.reshape(1, n), b.reshape(1, n), w_gate, b_gate.reshape(1, n), pb, w_proj)


def _swiglu_up_kernel(x_ref, w1_ref, w3_ref, o_ref, w1b_ref, w3b_ref):
    @pl.when(pl.program_id(1) == 0)
    def _():
        w1b_ref[...] = w1_ref[...].astype(w1b_ref.dtype)
        w3b_ref[...] = w3_ref[...].astype(w3b_ref.dtype)

    x = x_ref[...]
    a = jnp.dot(x, w1b_ref[...], preferred_element_type=F32)
    b = jnp.dot(x, w3b_ref[...], preferred_element_type=F32)
    o_ref[...] = (a * jax.nn.sigmoid(a) * b).astype(o_ref.dtype)


def _swiglu_up(xb, w1, w3):
    m, k = xb.shape
    n = w1.shape[1]
    tm, tn = _tile(m, 1024), _tile(n, 512)
    return pl.pallas_call(
        _swiglu_up_kernel,
        grid=(n // tn, m // tm),
        in_specs=[pl.BlockSpec((tm, k), lambda j, i: (i, 0)),
                  pl.BlockSpec((k, tn), lambda j, i: (0, j)),
                  pl.BlockSpec((k, tn), lambda j, i: (0, j))],
        out_specs=pl.BlockSpec((tm, tn), lambda j, i: (i, j)),
        out_shape=jax.ShapeDtypeStruct((m, n), BF16),
        scratch_shapes=[pltpu.VMEM((k, tn), BF16), pltpu.VMEM((k, tn), BF16)],
        compiler_params=_params(("parallel", "arbitrary"), 56),
        name="swiglu_up",
    )(xb, w1, w3)


def _pool_mixer_kernel(xb_ref, win_ref, wg_ref, sc_ref, wo_ref, x_ref, g_ref, b_ref,
                       o32_ref, o16_ref, opk_ref, buf_ref, *, alpha):
    s = pl.program_id(1)
    tm = xb_ref.shape[0]
    pd = wg_ref.shape[1]

    @pl.when(s == 0)
    def _():
        buf_ref[0:POOL_HALO, :] = jnp.zeros((POOL_HALO, buf_ref.shape[1]), F32)

    @pl.when(s > 0)
    def _():
        buf_ref[0:POOL_HALO, :] = buf_ref[tm:tm + POOL_HALO, :]

    buf_ref[POOL_HALO:POOL_HALO + tm, :] = jnp.dot(xb_ref[...], win_ref[...], preferred_element_type=F32)
    t = s * tm + lax.broadcasted_iota(jnp.int32, (tm, 1), 0)
    pieces = []
    for g, w in enumerate(POOL_WINDOWS):
        cols = slice(g * pd, (g + 1) * pd)
        u = buf_ref[POOL_HALO:POOL_HALO + tm, cols]
        tot = u
        for i in range(1, w):
            tot = tot + buf_ref[POOL_HALO - i:POOL_HALO - i + tm, cols]
        cnt = jnp.minimum(t + 1, w).astype(F32)
        mixed = (tot / cnt - u).astype(wg_ref.dtype)
        y = jnp.dot(mixed, wg_ref[g], preferred_element_type=F32) * sc_ref[:, cols]
        pieces.append(y.astype(wo_ref.dtype))
    hid = jnp.dot(jnp.concatenate(pieces, axis=1), wo_ref[...], preferred_element_type=F32)
    y = _layer_norm(alpha * x_ref[...] + hid, g_ref[...], b_ref[...])
    o32_ref[...] = y
    o16_ref[...] = y.astype(o16_ref.dtype)
    opk_ref[...] = _pack_bf16_pairs(y)


def _pool_mixer_ln(xb, x_res, w_in, w_group, scale, w_o, g, b, alpha, batch, seq):
    m, d = x_res.shape
    tm = _tile(seq, 256)
    s_tiles = seq // tm
    assert tm % SUBLANES == 0 and max(POOL_WINDOWS) <= POOL_HALO <= tm
    row = lambda bi, s: (bi * s_tiles + s, 0)
    fixed = lambda bi, s: (0, 0)
    resident = lambda shape, index: pl.BlockSpec(shape, index, pipeline_mode=pl.Buffered(1))
    return pl.pallas_call(
        functools.partial(_pool_mixer_kernel, alpha=alpha),
        grid=(batch, s_tiles),
        in_specs=[pl.BlockSpec((tm, d), row),
                  resident((d, d), fixed),
                  resident(w_group.shape, lambda bi, s: (0, 0, 0)),
                  pl.BlockSpec((1, d), fixed),
                  resident((d, d), fixed),
                  pl.BlockSpec((tm, d), row), pl.BlockSpec((1, d), fixed), pl.BlockSpec((1, d), fixed)],
        out_specs=[pl.BlockSpec((tm, d), row), pl.BlockSpec((tm, d), row),
                   pl.BlockSpec((tm, d // 2), row)],
        out_shape=[jax.ShapeDtypeStruct((m, d), F32), jax.ShapeDtypeStruct((m, d), BF16),
                   jax.ShapeDtypeStruct((m, d // 2), jnp.uint32)],
        scratch_shapes=[pltpu.VMEM((POOL_HALO + tm, d), F32)],
        compiler_params=_params(("parallel", "arbitrary"), 56),
        name="pool_mixer_ln",
    )(xb, w_in, w_group, scale.reshape(1, d), w_o, x_res, g.reshape(1, d), b.reshape(1, d))


def _router_kernel(x_ref, wr_ref, tri_ref, e_ref, rank_ref, gate_ref, cnt_ref, run_ref):
    @pl.when(pl.program_id(0) == 0)
    def _():
        run_ref[...] = jnp.zeros_like(run_ref)

    logits = lax.dot_general(wr_ref[...], x_ref[...], (((1,), (1,)), ((), ())),
                             preferred_element_type=F32)
    n_exp = logits.shape[0]
    eidx = lax.broadcasted_iota(jnp.int32, logits.shape, 0)
    m1 = jnp.max(logits, axis=0, keepdims=True)
    i1 = jnp.min(jnp.where(logits == m1, eidx, n_exp), axis=0, keepdims=True)
    rest = jnp.where(eidx == i1, -jnp.inf, logits)
    m2 = jnp.max(rest, axis=0, keepdims=True)
    i2 = jnp.min(jnp.where(rest == m2, eidx, n_exp), axis=0, keepdims=True)
    ex = jnp.exp(m2 - m1)
    g1 = 1.0 / (1.0 + ex)
    g2 = ex / (1.0 + ex)

    oh1 = (eidx == i1).astype(F32)
    oh2 = (eidx == i2).astype(F32)
    tri = tri_ref[...]
    c1 = jnp.dot(oh1.astype(tri.dtype), tri, preferred_element_type=F32)
    c2 = jnp.dot(oh2.astype(tri.dtype), tri, preferred_element_type=F32)
    tot1 = jnp.sum(oh1, axis=1, keepdims=True)
    tot2 = jnp.sum(oh2, axis=1, keepdims=True)
    run = run_ref[:, 0:1]
    r1 = jnp.sum(oh1 * (run + c1), axis=0, keepdims=True)
    r2 = jnp.sum(oh2 * (run + tot1 + c2), axis=0, keepdims=True)
    run = run + tot1 + tot2
    run_ref[...] = jnp.broadcast_to(run, run_ref.shape)

    e_ref[...] = jnp.concatenate([i1, i2], axis=0)
    rank_ref[...] = jnp.concatenate([r1, r2], axis=0).astype(jnp.int32)
    gate_ref[...] = jnp.concatenate([g1, g2], axis=0)
    cnt_ref[...] = jnp.broadcast_to(run, cnt_ref.shape).astype(jnp.int32)


def _router(xb, w_router_t):
    m, k = xb.shape
    n_exp = w_router_t.shape[0]
    tm = _tile(m, 512)
    tri = (jnp.arange(tm)[:, None] < jnp.arange(tm)[None, :]).astype(BF16)
    pair = lambda dt: jax.ShapeDtypeStruct((TOP_K, m), dt)
    pair_spec = pl.BlockSpec((TOP_K, tm), lambda i: (0, i))
    return pl.pallas_call(
        _router_kernel,
        grid=(m // tm,),
        in_specs=[pl.BlockSpec((tm, k), lambda i: (i, 0)),
                  pl.BlockSpec((n_exp, k), lambda i: (0, 0)),
                  pl.BlockSpec((tm, tm), lambda i: (0, 0))],
        out_specs=[pair_spec, pair_spec, pair_spec, pl.BlockSpec((n_exp, LANES), lambda i: (0, 0))],
        out_shape=[pair(jnp.int32), pair(jnp.int32), pair(F32),
                   jax.ShapeDtypeStruct((n_exp, LANES), jnp.int32)],
        scratch_shapes=[pltpu.VMEM((n_exp, LANES), F32)],
        compiler_params=_params(("arbitrary",), 32),
        name="router",
    )(xb, w_router_t, tri)


def _tile_positions(pos, tm):
    return pos.reshape(TOP_K, -1, tm).transpose(1, 0, 2)


def _dispatch_kernel(fill_ref, pos_ref, x_ref, xs_ref, stage_ref, zero_ref, sem, fill_sem):
    i = pl.program_id(0)
    tm = x_ref.shape[0]
    slot = lax.rem(i, 2)

    @pl.when(i == 0)
    def _():
        zero_ref[...] = jnp.zeros_like(zero_ref)
        fill_rows = zero_ref.shape[0]
        n_groups = fill_ref.shape[0] - 1

        def fill(start):
            return pltpu.make_async_copy(
                zero_ref, xs_ref.at[pl.ds(pl.multiple_of(start, SUBLANES), fill_rows), :], fill_sem)

        fills = [fill(fill_ref[e]) for e in range(n_groups)]
        for c in fills:
            c.start()
        for c in fills:
            c.wait()

        def fill_unused(t, carry):
            c = fill(t * fill_rows)
            c.start()
            c.wait()
            return carry
        lax.fori_loop(fill_ref[n_groups] // fill_rows, xs_ref.shape[0] // fill_rows, fill_unused, 0)

    def row_copy(s, r, k):
        return pltpu.make_async_copy(stage_ref.at[s, pl.ds(r, 1), :],
                                     xs_ref.at[pl.ds(pos_ref[k, r], 1), :], sem.at[s])

    def start(r, carry):
        for k in range(TOP_K):
            row_copy(slot, r, k).start()
        return carry

    def drain(s):
        def wait(r, carry):
            for k in range(TOP_K):
                pltpu.make_async_copy(stage_ref.at[s, pl.ds(r, 1), :],
                                      xs_ref.at[pl.ds(0, 1), :], sem.at[s]).wait()
            return carry
        lax.fori_loop(0, tm, wait, 0, unroll=ROW_COPY_UNROLL)

    stage_ref[slot] = x_ref[...]
    lax.fori_loop(0, tm, start, 0, unroll=ROW_COPY_UNROLL)

    @pl.when(i > 0)
    def _():
        drain(1 - slot)

    @pl.when(i == pl.num_programs(0) - 1)
    def _():
        drain(slot)


def _dispatch(fill_starts, pos, xp, n_slots, fill_rows):
    m, w = xp.shape
    tm = _tile(m, 512)
    grid_spec = pltpu.PrefetchScalarGridSpec(
        num_scalar_prefetch=1,
        grid=(m // tm,),
        in_specs=[pl.BlockSpec((None, TOP_K, tm), lambda i, fill: (i, 0, 0), memory_space=pltpu.SMEM),
                  pl.BlockSpec((tm, w), lambda i, fill: (i, 0))],
        out_specs=pl.BlockSpec(memory_space=pl.ANY),
        scratch_shapes=[pltpu.VMEM((2, tm, w), xp.dtype), pltpu.VMEM((fill_rows, w), xp.dtype),
                        pltpu.SemaphoreType.DMA((2,)), pltpu.SemaphoreType.DMA(())],
    )
    return pl.pallas_call(
        _dispatch_kernel,
        grid_spec=grid_spec,
        out_shape=jax.ShapeDtypeStruct((n_slots, w), xp.dtype),
        compiler_params=_params(("arbitrary",), 32, disable_bounds_checks=True),
        name="dispatch",
    )(fill_starts, _tile_positions(pos, tm), xp)


def _cast_expert_weights(first_ref, pairs):
    @pl.when(first_ref[pl.program_id(1)] > 0)
    def _():
        for src, dst in pairs:
            dst[...] = src[...].astype(dst.dtype)


def _expert_up_kernel(te_ref, tr_ref, tv_ref, tf_ref, xs_ref, w1_ref, w3_ref, h_ref, w1b_ref, w3b_ref):
    del te_ref, tr_ref
    _cast_expert_weights(tf_ref, [(w1_ref, w1b_ref), (w3_ref, w3b_ref)])

    @pl.when(tv_ref[pl.program_id(1)] > 0)
    def _():
        x = _unpack_bf16_pairs(xs_ref[...])
        a = jnp.dot(x, w1b_ref[...], preferred_element_type=F32)
        b = jnp.dot(x, w3b_ref[...], preferred_element_type=F32)
        h_ref[...] = (a * jax.nn.sigmoid(a) * b).astype(h_ref.dtype)

    @pl.when(tv_ref[pl.program_id(1)] == 0)
    def _():
        h_ref[...] = jnp.zeros_like(h_ref)


def _expert_specs():
    rows = lambda c, i, te, tr, tv, tf: (tr[i], 0)
    weights = lambda c, i, te, tr, tv, tf: (te[i], 0, c)
    out = lambda c, i, te, tr, tv, tf: (i, c)
    return rows, weights, out


def _expert_up(tiles, xs, w1, w3, tm):
    n_slots = xs.shape[0]
    _, k, n = w1.shape
    tn = _tile(n, 512)
    rows, weights, out = _expert_specs()
    weight_spec = pl.BlockSpec((None, k, tn), weights)
    grid_spec = pltpu.PrefetchScalarGridSpec(
        num_scalar_prefetch=4,
        grid=(n // tn, n_slots // tm),
        in_specs=[pl.BlockSpec((tm, k // 2), rows), weight_spec, weight_spec],
        out_specs=pl.BlockSpec((tm, tn), out),
        scratch_shapes=[pltpu.VMEM((k, tn), BF16), pltpu.VMEM((k, tn), BF16)],
    )
    return pl.pallas_call(
        _expert_up_kernel,
        grid_spec=grid_spec,
        out_shape=jax.ShapeDtypeStruct((n_slots, n), BF16),
        compiler_params=_params(("parallel", "arbitrary"), 56),
        name="expert_up",
    )(*tiles, xs, w1, w3)


def _expert_down_kernel(te_ref, tr_ref, tv_ref, tf_ref, h_ref, w2_ref, y_ref, w2b_ref):
    del te_ref, tr_ref
    _cast_expert_weights(tf_ref, [(w2_ref, w2b_ref)])

    @pl.when(tv_ref[pl.program_id(1)] > 0)
    def _():
        y_ref[...] = jnp.dot(h_ref[...], w2b_ref[...], preferred_element_type=F32)

    @pl.when(tv_ref[pl.program_id(1)] == 0)
    def _():
        y_ref[...] = jnp.zeros_like(y_ref)


def _expert_down(tiles, h, w2, tm):
    n_slots, k = h.shape
    n = w2.shape[2]
    tn = _tile(n, 512)
    rows, weights, out = _expert_specs()
    grid_spec = pltpu.PrefetchScalarGridSpec(
        num_scalar_prefetch=4,
        grid=(n // tn, n_slots // tm),
        in_specs=[pl.BlockSpec((tm, k), rows),
                  pl.BlockSpec((None, k, tn), weights, pipeline_mode=pl.Buffered(1))],
        out_specs=pl.BlockSpec((tm, tn), out),
        scratch_shapes=[pltpu.VMEM((k, tn), BF16)],
    )
    return pl.pallas_call(
        _expert_down_kernel,
        grid_spec=grid_spec,
        out_shape=jax.ShapeDtypeStruct((n_slots, n), F32),
        compiler_params=_params(("parallel", "arbitrary"), 56),
        name="expert_down",
    )(*tiles, h, w2)


def _gather_mix_ln_kernel(pos_ref, next_pos_ref, ys_ref, gate_ref, x_ref, g_ref, b_ref, wg_ref,
                          bg_ref, p_ref, wp_ref, o32_ref, o16_ref, buf_ref, sem, *, alpha):
    i = pl.program_id(0)
    tm = x_ref.shape[0]
    slot = lax.rem(i, 2)

    def issue(idx_ref, s):
        def start(r, carry):
            for k in range(TOP_K):
                pltpu.make_async_copy(ys_ref.at[pl.ds(idx_ref[k, r], 1), :],
                                      buf_ref.at[s, k, pl.ds(r, 1), :], sem.at[s]).start()
            return carry
        lax.fori_loop(0, tm, start, 0, unroll=ROW_COPY_UNROLL)

    def wait(r, carry):
        for k in range(TOP_K):
            pltpu.make_async_copy(ys_ref.at[pl.ds(0, 1), :],
                                  buf_ref.at[slot, k, pl.ds(r, 1), :], sem.at[slot]).wait()
        return carry

    @pl.when(i == 0)
    def _():
        issue(pos_ref, slot)

    @pl.when(i + 1 < pl.num_programs(0))
    def _():
        issue(next_pos_ref, 1 - slot)

    lax.fori_loop(0, tm, wait, 0, unroll=ROW_COPY_UNROLL)
    gates = gate_ref[...]
    f = gates[:, 0:1] * buf_ref[slot, 0] + gates[:, 1:2] * buf_ref[slot, 1]
    y = _layer_norm(alpha * x_ref[...] + f, g_ref[...], b_ref[...])
    y = _gated_embedding(y, wg_ref, bg_ref, p_ref, wp_ref)
    o32_ref[...] = y
    o16_ref[...] = y.astype(o16_ref.dtype)


def _gather_mix_ln(pos, ys, gates, x_res, g, b, alpha, w_gate, b_gate, pb, w_proj):
    m, d = x_res.shape
    kp = pb.shape[1]
    tm = _tile(m, 512)
    last = m // tm - 1
    row = lambda i: (i, 0)
    fixed = lambda i: (0, 0)
    pos_tiles = _tile_positions(pos, tm)
    return pl.pallas_call(
        functools.partial(_gather_mix_ln_kernel, alpha=alpha),
        grid=(m // tm,),
        in_specs=[pl.BlockSpec((None, TOP_K, tm), lambda i: (i, 0, 0), memory_space=pltpu.SMEM),
                  pl.BlockSpec((None, TOP_K, tm), lambda i: (jnp.minimum(i + 1, last), 0, 0),
                               memory_space=pltpu.SMEM),
                  pl.BlockSpec(memory_space=pl.ANY),
                  pl.BlockSpec((tm, TOP_K), row),
                  pl.BlockSpec((tm, d), row),
                  pl.BlockSpec((1, d), fixed), pl.BlockSpec((1, d), fixed)]
        + _ple_specs(tm, d, kp, row, fixed),
        out_specs=[pl.BlockSpec((tm, d), row), pl.BlockSpec((tm, d), row)],
        out_shape=[jax.ShapeDtypeStruct((m, d), F32), jax.ShapeDtypeStruct((m, d), BF16)],
        scratch_shapes=[pltpu.VMEM((2, TOP_K, tm, d), F32), pltpu.SemaphoreType.DMA((2,))],
        compiler_params=_params(("arbitrary",), 56, disable_bounds_checks=True),
        name="gather_mix_ln",
    )(pos_tiles, pos_tiles, ys, gates, x_res, g.reshape(1, d), b.reshape(1, d),
      w_gate, b_gate.reshape(1, d), pb, w_proj)


def _moe_ffn_ln(x32, xb, x_words, w_router, w1, w3, w2, g, b, alpha, ple):
    m, _ = xb.shape
    n_exp = w1.shape[0]
    tm = _tile(m, 1024)
    experts, ranks, gates, counts = _router(xb, w_router.T.astype(BF16))

    counts = counts[:, 0]
    padded = (counts + tm - 1) // tm * tm
    ends = jnp.cumsum(padded)
    starts = ends - padded
    expert_ids = jnp.arange(n_exp, dtype=jnp.int32)[:, None, None]
    pos = jnp.sum(jnp.where(experts[None] == expert_ids, starts[:, None, None], 0), axis=0) + ranks
    assert (TOP_K * m) % tm == 0
    n_slots = TOP_K * m + n_exp * tm
    tile_index = jnp.arange(n_slots // tm, dtype=jnp.int32)
    tile_valid = (tile_index * tm < ends[-1]).astype(jnp.int32)
    tile_row = jnp.minimum(tile_index, jnp.maximum(ends[-1] // tm - 1, 0))
    tile_expert = jnp.sum((ends[None, :] <= (tile_row * tm)[:, None]).astype(jnp.int32), axis=1)
    tile_expert = jnp.minimum(tile_expert, n_exp - 1)
    prev_expert = jnp.concatenate([jnp.full((1,), -1, jnp.int32), tile_expert[:-1]])
    tile_first = (tile_expert != prev_expert).astype(jnp.int32)
    tiles = (tile_expert, tile_row, tile_valid, tile_first)

    fill_starts = (starts + counts) // SUBLANES * SUBLANES
    xs = _dispatch(jnp.concatenate([fill_starts, ends[-1:]]), pos, x_words, n_slots, tm)
    h = _expert_up(tiles, xs, w1, w3, tm)
    ys = _expert_down(tiles, h, w2, tm)
    return _gather_mix_ln(pos, ys, gates.T, x32, g, b, alpha, *ple)


def kernel(x, p, attn_w_qkv, attn_w_o, pool_w_in, pool_w_group, pool_scale, pool_w_o,
           ln_mix_g, ln_mix_b, ln_ffn_g, ln_ffn_b, ffn_w1, ffn_w3, ffn_w2,
           moe_router, moe_w1, moe_w3, moe_w2, ple_w_proj, ple_w_gate, ple_b_gate):
    batch, seq, d = x.shape
    depth = p.shape[0]
    att_w = N_HEADS * LANES
    assert attn_w_qkv.shape[2] == N_GROUPS * 3 * att_w
    alpha = (2 * depth) ** 0.25
    m = batch * seq
    bf = lambda t: t.astype(BF16)

    x32 = x.reshape(m, d)
    xb = None
    for i in range(depth):
        j = i // 2
        ple = (bf(ple_w_gate[i]), ple_b_gate[i], bf(p[i].reshape(m, -1)), bf(ple_w_proj[i]))
        if i % 2 == 0:
            outs, lses = [], []
            x_groups = _cast_and_regroup(x32, batch, seq)
            for grp, (_, dil) in enumerate(DILATED_GROUPS):
                qkv = _qkv_rope(x_groups[grp], attn_w_qkv[j], _rope_tables(seq, dil), grp, seq, att_w)
                o_g, lse_g = _dilated_attention(qkv, grp, seq, att_w)
                outs.append(o_g)
                lses.append(lse_g)
            x32, xb = _attn_out(outs, lses, bf(attn_w_o[j]), x32, ln_mix_g[i], ln_mix_b[i],
                                alpha, batch, seq)
            h = _swiglu_up(xb, ffn_w1[j], ffn_w3[j])
            x32, xb = _matmul_res_ln_ple(h, bf(ffn_w2[j]), x32, ln_ffn_g[i], ln_ffn_b[i], alpha, *ple)
        else:
            x32, xb, x_words = _pool_mixer_ln(
                xb, x32, bf(pool_w_in[j]), bf(pool_w_group[j]), pool_scale[j].reshape(-1),
                bf(pool_w_o[j]), ln_mix_g[i], ln_mix_b[i], alpha, batch, seq)
            x32, xb = _moe_ffn_ln(x32, xb, x_words, moe_router[j], moe_w1[j], moe_w3[j], moe_w2[j],
                                  ln_ffn_g[i], ln_ffn_b[i], alpha, ple)
    return x32.reshape(batch, seq, d)
```

```python
import functools
import math

import jax
import jax.numpy as jnp
from jax import lax
from jax.experimental import pallas as pl
from jax.experimental.pallas import tpu as pltpu

F32 = jnp.float32
BF16 = jnp.bfloat16

N_HEADS = 16
DILATED_GROUPS = ((128, 1), (512, 4), (2048, 16))
N_GROUPS = len(DILATED_GROUPS)
ATT_BLOCK = 128
ATT_BLOCKS_PER_STEP = 4
REGROUP_CHUNK = 256
ROPE_THETA = 10000.0
POOL_WINDOWS = (2, 4, 8, 16)
POOL_HALO = 16
TOP_K = 2
ROW_COPY_UNROLL = 4
LN_EPS = 1e-5
NEG_INF = -1e30

LANES = 128
SUBLANES = 8
MIB = 1024 * 1024


def _tile(dim, pref):
    t = min(dim, pref)
    while dim % t:
        t //= 2
    return t


def _params(semantics, vmem_mib, **kw):
    return pltpu.CompilerParams(dimension_semantics=semantics, vmem_limit_bytes=vmem_mib * MIB, **kw)


def _layer_norm(y, g, b):
    mu = jnp.mean(y, axis=-1, keepdims=True)
    yc = y - mu
    var = jnp.mean(yc * yc, axis=-1, keepdims=True)
    return yc * lax.rsqrt(var + LN_EPS) * g + b


def _pack_bf16_pairs(y):
    half = y.shape[1] // 2
    bits = lambda t: lax.bitcast_convert_type(t.astype(BF16).astype(F32), jnp.uint32)
    return (bits(y[:, :half]) >> 16) | bits(y[:, half:])


def _unpack_bf16_pairs(words):
    lo = lax.bitcast_convert_type(words << 16, F32)
    hi = lax.bitcast_convert_type(words & jnp.uint32(0xFFFF0000), F32)
    return jnp.concatenate([lo, hi], axis=1).astype(BF16)


def _regroup_kernel(x_ref, perm_ref, xb_ref, *group_refs, dils, chunk):
    xb = x_ref[...].astype(xb_ref.dtype)
    xb_ref[...] = xb
    for which, (dil, o_ref) in enumerate(zip(dils, group_refs)):
        slab = chunk // dil
        for ch in range(x_ref.shape[0] // chunk):
            grouped = jnp.dot(perm_ref[which], xb[ch * chunk:(ch + 1) * chunk],
                              preferred_element_type=F32).astype(o_ref.dtype)
            for r in range(dil):
                o_ref[r, ch * slab:(ch + 1) * slab, :] = grouped[r * slab:(r + 1) * slab]


def _cast_and_regroup(x32, batch, seq):
    m, d = x32.shape
    dils = tuple(dil for _, dil in DILATED_GROUPS if dil > 1)
    chunk = REGROUP_CHUNK
    tm = _tile(seq, 512)
    assert tm % chunk == 0 and all(chunk % (dil * 2 * SUBLANES) == 0 for dil in dils)
    src = jnp.arange(chunk)
    perms = jnp.stack([((src[:, None] % (chunk // dil)) * dil + src[:, None] // (chunk // dil)
                        == src[None, :]).astype(BF16) for dil in dils])
    s_tiles = seq // tm
    outs = pl.pallas_call(
        functools.partial(_regroup_kernel, dils=dils, chunk=chunk),
        grid=(batch, s_tiles),
        in_specs=[pl.BlockSpec((tm, d), lambda b, s: (b * s_tiles + s, 0)),
                  pl.BlockSpec(perms.shape, lambda b, s: (0, 0, 0))],
        out_specs=[pl.BlockSpec((tm, d), lambda b, s: (b * s_tiles + s, 0))]
        + [pl.BlockSpec((None, dil, tm // dil, d), lambda b, s: (b, 0, s, 0)) for dil in dils],
        out_shape=[jax.ShapeDtypeStruct((m, d), BF16)]
        + [jax.ShapeDtypeStruct((batch, dil, seq // dil, d), BF16) for dil in dils],
        compiler_params=_params(("parallel", "parallel"), 48),
        name="cast_and_regroup",
    )(x32, perms)
    by_dil = {1: outs[0], **{dil: o.reshape(m, d) for dil, o in zip(dils, outs[1:])}}
    return [by_dil[dil] for _, dil in DILATED_GROUPS]


def _qkv_rope_kernel(x_ref, w_ref, cos_ref, sin_ref, o_ref, wb_ref):
    @pl.when(pl.program_id(1) == 0)
    def _():
        wb_ref[...] = w_ref[...].astype(wb_ref.dtype)

    acc = jnp.dot(x_ref[...], wb_ref[...], preferred_element_type=F32)
    cos = cos_ref[...]
    sin = sin_ref[...]
    for h in range(acc.shape[1] // LANES):
        t = acc[:, h * LANES:(h + 1) * LANES]
        rot = pltpu.roll(t, LANES // 2, 1)
        o_ref[:, h * LANES:(h + 1) * LANES] = (t * cos + rot * sin).astype(o_ref.dtype)


def _qkv_rope(x_g, w, tables, group, seq, att_w):
    m, k = x_g.shape
    cos_tab, sin_tab = tables
    tm = _tile(seq, 1024)
    tn = _tile(att_w, 1024)
    s_tiles = seq // tm
    per_role = att_w // tn
    per_group = 3 * per_role
    table_spec = pl.BlockSpec((None, tm, LANES), lambda j, i: (j // per_role, i % s_tiles, 0))
    return pl.pallas_call(
        _qkv_rope_kernel,
        grid=(per_group, m // tm),
        in_specs=[pl.BlockSpec((tm, k), lambda j, i: (i, 0)),
                  pl.BlockSpec((k, tn), lambda j, i: (0, group * per_group + j)),
                  table_spec, table_spec],
        out_specs=pl.BlockSpec((tm, tn), lambda j, i: (i, j)),
        out_shape=jax.ShapeDtypeStruct((m, 3 * att_w), BF16),
        scratch_shapes=[pltpu.VMEM((k, tn), BF16)],
        compiler_params=_params(("parallel", "arbitrary"), 56),
        name=f"qkv_rope_g{group}",
    )(x_g, w, cos_tab, sin_tab)


def _rope_tables(seq, dil):
    half = LANES // 2
    inv = ROPE_THETA ** (-jnp.arange(half, dtype=F32) / half)
    row = jnp.arange(seq)
    n_sub = seq // dil
    pos = (row % n_sub) * dil + row // n_sub
    ang = pos.astype(F32)[:, None] * inv[None, :]
    cos, sin = jnp.cos(ang), jnp.sin(ang)
    cos_full = jnp.concatenate([cos, cos], axis=-1)
    sin_signed = jnp.concatenate([-sin, sin], axis=-1)
    return (jnp.stack([cos_full, cos_full, jnp.ones_like(cos_full)]),
            jnp.stack([sin_signed, sin_signed, jnp.zeros_like(cos_full)]))


def _attn_kernel(q_ref, kc_ref, kp_ref, vc_ref, vp_ref, o_ref, lse_ref, *, nb, n_back, scale):
    blk = kp_ref.shape[0]
    sub = q_ref.shape[0] // blk
    first_has_prev = lax.rem(pl.program_id(0) * sub, nb) > 0
    qi = lax.broadcasted_iota(jnp.int32, (blk, 2 * blk), 0) + blk
    ki = lax.broadcasted_iota(jnp.int32, (blk, 2 * blk), 1)
    dist = qi - ki
    band = (dist >= 0) & (dist <= n_back)
    bias_inner = jnp.where(band, 0.0, NEG_INF)
    bias_first = jnp.where(band & (first_has_prev | (ki >= blk)), 0.0, NEG_INF)
    lane = lax.broadcasted_iota(jnp.int32, (blk, LANES), 1)
    for t in range(sub):
        rows = slice(t * blk, (t + 1) * blk)
        before = slice((t - 1) * blk, t * blk)
        bias = bias_first if t == 0 else bias_inner
        lse_tile = jnp.zeros((blk, LANES), F32)
        for h in range(q_ref.shape[1] // LANES):
            hs = slice(h * LANES, (h + 1) * LANES)
            k_prev = kp_ref[:, hs] if t == 0 else kc_ref[before, hs]
            v_prev = vp_ref[:, hs] if t == 0 else vc_ref[before, hs]
            k2 = jnp.concatenate([k_prev, kc_ref[rows, hs]], axis=0)
            v2 = jnp.concatenate([v_prev, vc_ref[rows, hs]], axis=0)
            s = lax.dot_general(q_ref[rows, hs], k2, (((1,), (1,)), ((), ())),
                                preferred_element_type=F32) * scale + bias
            m = jnp.max(s, axis=-1, keepdims=True)
            e = jnp.exp(s - m)
            l = jnp.sum(e, axis=-1, keepdims=True)
            p = (e * (1.0 / l)).astype(v2.dtype)
            o_ref[rows, hs] = jnp.dot(p, v2, preferred_element_type=F32)
            lse_tile = jnp.where(lane == h, m + jnp.log(l), lse_tile)
        lse_ref[rows, :] = lse_tile


def _dilated_attention(qkv, group, seq, att_w):
    m = qkv.shape[0]
    window, dil = DILATED_GROUPS[group]
    nb = seq // (dil * ATT_BLOCK)
    sub = math.gcd(nb, ATT_BLOCKS_PER_STEP)
    assert seq % (dil * ATT_BLOCK) == 0 and window // dil <= ATT_BLOCK
    step_rows = sub * ATT_BLOCK

    def spec(role, prev):
        if prev:
            return pl.BlockSpec((ATT_BLOCK, att_w), lambda r: (jnp.maximum(sub * r - 1, 0), role))
        return pl.BlockSpec((step_rows, att_w), lambda r: (r, role))

    return pl.pallas_call(
        functools.partial(_attn_kernel, nb=nb, n_back=window // dil, scale=LANES ** -0.5),
        grid=(m // step_rows,),
        in_specs=[spec(0, False), spec(1, False), spec(1, True), spec(2, False), spec(2, True)],
        out_specs=[pl.BlockSpec((step_rows, att_w), lambda r: (r, 0)),
                   pl.BlockSpec((step_rows, LANES), lambda r: (r, 0))],
        out_shape=[jax.ShapeDtypeStruct((m, att_w), F32), jax.ShapeDtypeStruct((m, LANES), F32)],
        compiler_params=_params(("arbitrary",), 32),
        name=f"dilated_attn_g{group}",
    )(qkv, qkv, qkv, qkv, qkv)


def _token_order(ref, scr_ref):
    dil, n, w = ref.shape
    heads = range(w // LANES)
    if dil == 1:
        return [ref[0, :, h * LANES:(h + 1) * LANES] for h in heads]
    for r in range(dil):
        for h in heads:
            scr_ref[h, pl.ds(r, n, stride=dil), :] = ref[r, :, h * LANES:(h + 1) * LANES]
    return [scr_ref[h] for h in heads]


def _attn_out_kernel(*refs, alpha):
    o_refs, l_refs = refs[:N_GROUPS], refs[N_GROUPS:2 * N_GROUPS]
    w_ref, x_ref, g_ref, b_ref, o32_ref, o16_ref = refs[2 * N_GROUPS:2 * N_GROUPS + 6]
    scratch = refs[2 * N_GROUPS + 6:]
    o_scr, l_scr = scratch[:N_GROUPS], scratch[N_GROUPS:]
    outs = [_token_order(o_refs[g], o_scr[g]) for g in range(N_GROUPS)]
    lses = [_token_order(l_refs[g], l_scr[g])[0] for g in range(N_GROUPS)]
    m = functools.reduce(jnp.maximum, lses)
    es = [jnp.exp(l - m) for l in lses]
    inv = 1.0 / functools.reduce(jnp.add, es)
    wts = [e * inv for e in es]
    tm = x_ref.shape[0]
    heads = []
    for h in range(len(outs[0])):
        acc = None
        for g in range(N_GROUPS):
            term = jnp.broadcast_to(wts[g][:, h:h + 1], (tm, LANES)) * outs[g][h]
            acc = term if acc is None else acc + term
        heads.append(acc.astype(w_ref.dtype))
    mixed = jnp.concatenate(heads, axis=1)
    hid = jnp.dot(mixed, w_ref[...], preferred_element_type=F32)
    y = _layer_norm(alpha * x_ref[...] + hid, g_ref[...], b_ref[...])
    o32_ref[...] = y
    o16_ref[...] = y.astype(o16_ref.dtype)


def _attn_out(outs, lses, w_o, x_res, g, b, alpha, batch, seq):
    n_groups = len(outs)
    m, att_w = outs[0].shape
    d = w_o.shape[1]
    dils = [dil for _, dil in DILATED_GROUPS]
    tm = _tile(seq, 256)
    s_tiles = seq // tm
    assert all(tm % (8 * dil) == 0 for dil in dils)

    def group_view(t, g):
        return t.reshape(batch, dils[g], seq // dils[g], t.shape[-1])

    def group_spec(g, width):
        return pl.BlockSpec((None, dils[g], tm // dils[g], width),
                            lambda i: (i // s_tiles, 0, i % s_tiles, 0))

    row = lambda i: (i, 0)
    fixed = lambda i: (0, 0)
    return pl.pallas_call(
        functools.partial(_attn_out_kernel, alpha=alpha),
        grid=(m // tm,),
        in_specs=[group_spec(g, att_w) for g in range(n_groups)]
        + [group_spec(g, LANES) for g in range(n_groups)]
        + [pl.BlockSpec((att_w, d), fixed, pipeline_mode=pl.Buffered(1)),
           pl.BlockSpec((tm, d), row), pl.BlockSpec((1, d), fixed), pl.BlockSpec((1, d), fixed)],
        out_specs=[pl.BlockSpec((tm, d), row), pl.BlockSpec((tm, d), row)],
        out_shape=[jax.ShapeDtypeStruct((m, d), F32), jax.ShapeDtypeStruct((m, d), BF16)],
        scratch_shapes=[pltpu.VMEM((att_w // LANES, tm, LANES), F32) for _ in range(n_groups)]
        + [pltpu.VMEM((1, tm, LANES), F32) for _ in range(n_groups)],
        compiler_params=_params(("parallel",), 56),
        name="attn_out_ln",
    )(*[group_view(outs[g], g) for g in range(n_groups)],
      *[group_view(lses[g], g) for g in range(n_groups)],
      w_o, x_res, g.reshape(1, d), b.reshape(1, d))


def _gated_embedding(y, wg_ref, bg_ref, p_ref, wp_ref):
    z = jnp.dot(y.astype(wg_ref.dtype), wg_ref[...], preferred_element_type=F32) + bg_ref[...]
    e = jnp.dot(p_ref[...], wp_ref[...], preferred_element_type=F32)
    return y + jax.nn.sigmoid(z) * e


def _ple_specs(tm, d, kp, row, fixed):
    return [pl.BlockSpec((d, d), fixed, pipeline_mode=pl.Buffered(1)),
            pl.BlockSpec((1, d), fixed),
            pl.BlockSpec((tm, kp), row),
            pl.BlockSpec((kp, d), fixed, pipeline_mode=pl.Buffered(1))]


def _matmul_res_ln_ple_kernel(a_ref, w_ref, x_ref, g_ref, b_ref, wg_ref, bg_ref, p_ref, wp_ref,
                              o32_ref, o16_ref, *, alpha):
    h = jnp.dot(a_ref[...], w_ref[...], preferred_element_type=F32)
    y = _layer_norm(alpha * x_ref[...] + h, g_ref[...], b_ref[...])
    y = _gated_embedding(y, wg_ref, bg_ref, p_ref, wp_ref)
    o32_ref[...] = y
    o16_ref[...] = y.astype(o16_ref.dtype)


def _matmul_res_ln_ple(a, w, x_res, g, b, alpha, w_gate, b_gate, pb, w_proj):
    m, k = a.shape
    n = w.shape[1]
    kp = pb.shape[1]
    tm = _tile(m, 256)
    row = lambda i: (i, 0)
    fixed = lambda i: (0, 0)
    return pl.pallas_call(
        functools.partial(_matmul_res_ln_ple_kernel, alpha=alpha),
        grid=(m // tm,),
        in_specs=[pl.BlockSpec((tm, k), row),
                  pl.BlockSpec((k, n), fixed, pipeline_mode=pl.Buffered(1)),
                  pl.BlockSpec((tm, n), row), pl.BlockSpec((1, n), fixed), pl.BlockSpec((1, n), fixed)]
        + _ple_specs(tm, n, kp, row, fixed),
        out_specs=[pl.BlockSpec((tm, n), row), pl.BlockSpec((tm, n), row)],
        out_shape=[jax.ShapeDtypeStruct((m, n), F32), jax.ShapeDtypeStruct((m, n), BF16)],
        compiler_params=_params(("parallel",), 60),
        name="matmul_res_ln_ple",
    )(a, w, x_res, g.reshape(1, n), b.reshape(1, n), w_gate, b_gate.reshape(1, n), pb, w_proj)


def _swiglu_up_kernel(x_ref, w1_ref, w3_ref, o_ref, w1b_ref, w3b_ref):
    @pl.when(pl.program_id(1) == 0)
    def _():
        w1b_ref[...] = w1_ref[...].astype(w1b_ref.dtype)
        w3b_ref[...] = w3_ref[...].astype(w3b_ref.dtype)

    x = x_ref[...]
    a = jnp.dot(x, w1b_ref[...], preferred_element_type=F32)
    b = jnp.dot(x, w3b_ref[...], preferred_element_type=F32)
    o_ref[...] = (a * jax.nn.sigmoid(a) * b).astype(o_ref.dtype)


def _swiglu_up(xb, w1, w3):
    m, k = xb.shape
    n = w1.shape[1]
    tm, tn = _tile(m, 1024), _tile(n, 512)
    return pl.pallas_call(
        _swiglu_up_kernel,
        grid=(n // tn, m // tm),
        in_specs=[pl.BlockSpec((tm, k), lambda j, i: (i, 0)),
                  pl.BlockSpec((k, tn), lambda j, i: (0, j)),
                  pl.BlockSpec((k, tn), lambda j, i: (0, j))],
        out_specs=pl.BlockSpec((tm, tn), lambda j, i: (i, j)),
        out_shape=jax.ShapeDtypeStruct((m, n), BF16),
        scratch_shapes=[pltpu.VMEM((k, tn), BF16), pltpu.VMEM((k, tn), BF16)],
        compiler_params=_params(("parallel", "arbitrary"), 56),
        name="swiglu_up",
    )(xb, w1, w3)


def _pool_mixer_kernel(xb_ref, win_ref, wg_ref, sc_ref, wo_ref, x_ref, g_ref, b_ref,
                       o32_ref, o16_ref, opk_ref, buf_ref, *, alpha):
    s = pl.program_id(1)
    tm = xb_ref.shape[0]
    pd = wg_ref.shape[1]

    @pl.when(s == 0)
    def _():
        buf_ref[0:POOL_HALO, :] = jnp.zeros((POOL_HALO, buf_ref.shape[1]), F32)

    @pl.when(s > 0)
    def _():
        buf_ref[0:POOL_HALO, :] = buf_ref[tm:tm + POOL_HALO, :]

    buf_ref[POOL_HALO:POOL_HALO + tm, :] = jnp.dot(xb_ref[...], win_ref[...], preferred_element_type=F32)
    t = s * tm + lax.broadcasted_iota(jnp.int32, (tm, 1), 0)
    pieces = []
    for g, w in enumerate(POOL_WINDOWS):
        cols = slice(g * pd, (g + 1) * pd)
        u = buf_ref[POOL_HALO:POOL_HALO + tm, cols]
        tot = u
        for i in range(1, w):
            tot = tot + buf_ref[POOL_HALO - i:POOL_HALO - i + tm, cols]
        cnt = jnp.minimum(t + 1, w).astype(F32)
        mixed = (tot / cnt - u).astype(wg_ref.dtype)
        y = jnp.dot(mixed, wg_ref[g], preferred_element_type=F32) * sc_ref[:, cols]
        pieces.append(y.astype(wo_ref.dtype))
    hid = jnp.dot(jnp.concatenate(pieces, axis=1), wo_ref[...], preferred_element_type=F32)
    y = _layer_norm(alpha * x_ref[...] + hid, g_ref[...], b_ref[...])
    o32_ref[...] = y
    o16_ref[...] = y.astype(o16_ref.dtype)
    opk_ref[...] = _pack_bf16_pairs(y)


def _pool_mixer_ln(xb, x_res, w_in, w_group, scale, w_o, g, b, alpha, batch, seq):
    m, d = x_res.shape
    tm = _tile(seq, 256)
    s_tiles = seq // tm
    assert tm % SUBLANES == 0 and max(POOL_WINDOWS) <= POOL_HALO <= tm
    row = lambda bi, s: (bi * s_tiles + s, 0)
    fixed = lambda bi, s: (0, 0)
    resident = lambda shape, index: pl.BlockSpec(shape, index, pipeline_mode=pl.Buffered(1))
    return pl.pallas_call(
        functools.partial(_pool_mixer_kernel, alpha=alpha),
        grid=(batch, s_tiles),
        in_specs=[pl.BlockSpec((tm, d), row),
                  resident((d, d), fixed),
                  resident(w_group.shape, lambda bi, s: (0, 0, 0)),
                  pl.BlockSpec((1, d), fixed),
                  resident((d, d), fixed),
                  pl.BlockSpec((tm, d), row), pl.BlockSpec((1, d), fixed), pl.BlockSpec((1, d), fixed)],
        out_specs=[pl.BlockSpec((tm, d), row), pl.BlockSpec((tm, d), row),
                   pl.BlockSpec((tm, d // 2), row)],
        out_shape=[jax.ShapeDtypeStruct((m, d), F32), jax.ShapeDtypeStruct((m, d), BF16),
                   jax.ShapeDtypeStruct((m, d // 2), jnp.uint32)],
        scratch_shapes=[pltpu.VMEM((POOL_HALO + tm, d), F32)],
        compiler_params=_params(("parallel", "arbitrary"), 56),
        name="pool_mixer_ln",
    )(xb, w_in, w_group, scale.reshape(1, d), w_o, x_res, g.reshape(1, d), b.reshape(1, d))


def _router_kernel(x_ref, wr_ref, tri_ref, e_ref, rank_ref, gate_ref, cnt_ref, run_ref):
    @pl.when(pl.program_id(0) == 0)
    def _():
        run_ref[...] = jnp.zeros_like(run_ref)

    logits = lax.dot_general(wr_ref[...], x_ref[...], (((1,), (1,)), ((), ())),
                             preferred_element_type=F32)
    n_exp = logits.shape[0]
    eidx = lax.broadcasted_iota(jnp.int32, logits.shape, 0)
    m1 = jnp.max(logits, axis=0, keepdims=True)
    i1 = jnp.min(jnp.where(logits == m1, eidx, n_exp), axis=0, keepdims=True)
    rest = jnp.where(eidx == i1, -jnp.inf, logits)
    m2 = jnp.max(rest, axis=0, keepdims=True)
    i2 = jnp.min(jnp.where(rest == m2, eidx, n_exp), axis=0, keepdims=True)
    ex = jnp.exp(m2 - m1)
    g1 = 1.0 / (1.0 + ex)
    g2 = ex / (1.0 + ex)

    oh1 = (eidx == i1).astype(F32)
    oh2 = (eidx == i2).astype(F32)
    tri = tri_ref[...]
    c1 = jnp.dot(oh1.astype(tri.dtype), tri, preferred_element_type=F32)
    c2 = jnp.dot(oh2.astype(tri.dtype), tri, preferred_element_type=F32)
    tot1 = jnp.sum(oh1, axis=1, keepdims=True)
    tot2 = jnp.sum(oh2, axis=1, keepdims=True)
    run = run_ref[:, 0:1]
    r1 = jnp.sum(oh1 * (run + c1), axis=0, keepdims=True)
    r2 = jnp.sum(oh2 * (run + tot1 + c2), axis=0, keepdims=True)
    run = run + tot1 + tot2
    run_ref[...] = jnp.broadcast_to(run, run_ref.shape)

    e_ref[...] = jnp.concatenate([i1, i2], axis=0)
    rank_ref[...] = jnp.concatenate([r1, r2], axis=0).astype(jnp.int32)
    gate_ref[...] = jnp.concatenate([g1, g2], axis=0)
    cnt_ref[...] = jnp.broadcast_to(run, cnt_ref.shape).astype(jnp.int32)


def _router(xb, w_router_t):
    m, k = xb.shape
    n_exp = w_router_t.shape[0]
    tm = _tile(m, 512)
    tri = (jnp.arange(tm)[:, None] < jnp.arange(tm)[None, :]).astype(BF16)
    pair = lambda dt: jax.ShapeDtypeStruct((TOP_K, m), dt)
    pair_spec = pl.BlockSpec((TOP_K, tm), lambda i: (0, i))
    return pl.pallas_call(
        _router_kernel,
        grid=(m // tm,),
        in_specs=[pl.BlockSpec((tm, k), lambda i: (i, 0)),
                  pl.BlockSpec((n_exp, k), lambda i: (0, 0)),
                  pl.BlockSpec((tm, tm), lambda i: (0, 0))],
        out_specs=[pair_spec, pair_spec, pair_spec, pl.BlockSpec((n_exp, LANES), lambda i: (0, 0))],
        out_shape=[pair(jnp.int32), pair(jnp.int32), pair(F32),
                   jax.ShapeDtypeStruct((n_exp, LANES), jnp.int32)],
        scratch_shapes=[pltpu.VMEM((n_exp, LANES), F32)],
        compiler_params=_params(("arbitrary",), 32),
        name="router",
    )(xb, w_router_t, tri)


def _tile_positions(pos, tm):
    return pos.reshape(TOP_K, -1, tm).transpose(1, 0, 2)


def _dispatch_kernel(fill_ref, pos_ref, x_ref, xs_ref, stage_ref, zero_ref, sem, fill_sem):
    i = pl.program_id(0)
    tm = x_ref.shape[0]
    slot = lax.rem(i, 2)

    @pl.when(i == 0)
    def _():
        zero_ref[...] = jnp.zeros_like(zero_ref)
        fill_rows = zero_ref.shape[0]
        n_groups = fill_ref.shape[0] - 1

        def fill(start):
            return pltpu.make_async_copy(
                zero_ref, xs_ref.at[pl.ds(pl.multiple_of(start, SUBLANES), fill_rows), :], fill_sem)

        fills = [fill(fill_ref[e]) for e in range(n_groups)]
        for c in fills:
            c.start()
        for c in fills:
            c.wait()

        def fill_unused(t, carry):
            c = fill(t * fill_rows)
            c.start()
            c.wait()
            return carry
        lax.fori_loop(fill_ref[n_groups] // fill_rows, xs_ref.shape[0] // fill_rows, fill_unused, 0)

    def row_copy(s, r, k):
        return pltpu.make_async_copy(stage_ref.at[s, pl.ds(r, 1), :],
                                     xs_ref.at[pl.ds(pos_ref[k, r], 1), :], sem.at[s])

    def start(r, carry):
        for k in range(TOP_K):
            row_copy(slot, r, k).start()
        return carry

    def drain(s):
        def wait(r, carry):
            for k in range(TOP_K):
                pltpu.make_async_copy(stage_ref.at[s, pl.ds(r, 1), :],
                                      xs_ref.at[pl.ds(0, 1), :], sem.at[s]).wait()
            return carry
        lax.fori_loop(0, tm, wait, 0, unroll=ROW_COPY_UNROLL)

    stage_ref[slot] = x_ref[...]
    lax.fori_loop(0, tm, start, 0, unroll=ROW_COPY_UNROLL)

    @pl.when(i > 0)
    def _():
        drain(1 - slot)

    @pl.when(i == pl.num_programs(0) - 1)
    def _():
        drain(slot)


def _dispatch(fill_starts, pos, xp, n_slots, fill_rows):
    m, w = xp.shape
    tm = _tile(m, 512)
    grid_spec = pltpu.PrefetchScalarGridSpec(
        num_scalar_prefetch=1,
        grid=(m // tm,),
        in_specs=[pl.BlockSpec((None, TOP_K, tm), lambda i, fill: (i, 0, 0), memory_space=pltpu.SMEM),
                  pl.BlockSpec((tm, w), lambda i, fill: (i, 0))],
        out_specs=pl.BlockSpec(memory_space=pl.ANY),
        scratch_shapes=[pltpu.VMEM((2, tm, w), xp.dtype), pltpu.VMEM((fill_rows, w), xp.dtype),
                        pltpu.SemaphoreType.DMA((2,)), pltpu.SemaphoreType.DMA(())],
    )
    return pl.pallas_call(
        _dispatch_kernel,
        grid_spec=grid_spec,
        out_shape=jax.ShapeDtypeStruct((n_slots, w), xp.dtype),
        compiler_params=_params(("arbitrary",), 32, disable_bounds_checks=True),
        name="dispatch",
    )(fill_starts, _tile_positions(pos, tm), xp)


def _cast_expert_weights(first_ref, pairs):
    @pl.when(first_ref[pl.program_id(1)] > 0)
    def _():
        for src, dst in pairs:
            dst[...] = src[...].astype(dst.dtype)


def _for_used_rows(fill_ref, out_ref, compute):
    tm = out_ref.shape[0]
    fill = fill_ref[pl.program_id(1)]

    @pl.when(fill == 2)
    def _():
        compute(slice(0, tm))

    @pl.when(fill == 1)
    def _():
        compute(slice(0, tm // 2))
        out_ref[tm // 2:, :] = jnp.zeros((tm - tm // 2, out_ref.shape[1]), out_ref.dtype)

    @pl.when(fill == 0)
    def _():
        out_ref[...] = jnp.zeros_like(out_ref)


def _expert_up_kernel(te_ref, tr_ref, tv_ref, tf_ref, xs_ref, w1_ref, w3_ref, h_ref, w1b_ref, w3b_ref):
    del te_ref, tr_ref
    _cast_expert_weights(tf_ref, [(w1_ref, w1b_ref), (w3_ref, w3b_ref)])

    def compute(rows):
        x = _unpack_bf16_pairs(xs_ref[rows, :])
        a = jnp.dot(x, w1b_ref[...], preferred_element_type=F32)
        b = jnp.dot(x, w3b_ref[...], preferred_element_type=F32)
        h_ref[rows, :] = (a * jax.nn.sigmoid(a) * b).astype(h_ref.dtype)

    _for_used_rows(tv_ref, h_ref, compute)


def _expert_specs():
    rows = lambda c, i, te, tr, tv, tf: (tr[i], 0)
    weights = lambda c, i, te, tr, tv, tf: (te[i], 0, c)
    out = lambda c, i, te, tr, tv, tf: (i, c)
    return rows, weights, out


def _expert_up(tiles, xs, w1, w3, tm):
    n_slots = xs.shape[0]
    _, k, n = w1.shape
    tn = _tile(n, 512)
    rows, weights, out = _expert_specs()
    weight_spec = pl.BlockSpec((None, k, tn), weights)
    grid_spec = pltpu.PrefetchScalarGridSpec(
        num_scalar_prefetch=4,
        grid=(n // tn, n_slots // tm),
        in_specs=[pl.BlockSpec((tm, k // 2), rows), weight_spec, weight_spec],
        out_specs=pl.BlockSpec((tm, tn), out),
        scratch_shapes=[pltpu.VMEM((k, tn), BF16), pltpu.VMEM((k, tn), BF16)],
    )
    return pl.pallas_call(
        _expert_up_kernel,
        grid_spec=grid_spec,
        out_shape=jax.ShapeDtypeStruct((n_slots, n), BF16),
        compiler_params=_params(("parallel", "arbitrary"), 56),
        name="expert_up",
    )(*tiles, xs, w1, w3)


def _expert_down_kernel(te_ref, tr_ref, tv_ref, tf_ref, h_ref, w2_ref, y_ref, w2b_ref):
    del te_ref, tr_ref
    _cast_expert_weights(tf_ref, [(w2_ref, w2b_ref)])

    def compute(rows):
        y_ref[rows, :] = jnp.dot(h_ref[rows, :], w2b_ref[...], preferred_element_type=F32)

    _for_used_rows(tv_ref, y_ref, compute)


def _expert_down(tiles, h, w2, tm):
    n_slots, k = h.shape
    n = w2.shape[2]
    tn = _tile(n, 512)
    rows, weights, out = _expert_specs()
    grid_spec = pltpu.PrefetchScalarGridSpec(
        num_scalar_prefetch=4,
        grid=(n // tn, n_slots // tm),
        in_specs=[pl.BlockSpec((tm, k), rows),
                  pl.BlockSpec((None, k, tn), weights, pipeline_mode=pl.Buffered(1))],
        out_specs=pl.BlockSpec((tm, tn), out),
        scratch_shapes=[pltpu.VMEM((k, tn), BF16)],
    )
    return pl.pallas_call(
        _expert_down_kernel,
        grid_spec=grid_spec,
        out_shape=jax.ShapeDtypeStruct((n_slots, n), F32),
        compiler_params=_params(("parallel", "arbitrary"), 56),
        name="expert_down",
    )(*tiles, h, w2)


def _gather_mix_ln_kernel(pos_ref, next_pos_ref, ys_ref, gate_ref, x_ref, g_ref, b_ref, wg_ref,
                          bg_ref, p_ref, wp_ref, o32_ref, o16_ref, buf_ref, sem, *, alpha):
    i = pl.program_id(0)
    tm = x_ref.shape[0]
    slot = lax.rem(i, 2)

    def issue(idx_ref, s):
        def start(r, carry):
            for k in range(TOP_K):
                pltpu.make_async_copy(ys_ref.at[pl.ds(idx_ref[k, r], 1), :],
                                      buf_ref.at[s, k, pl.ds(r, 1), :], sem.at[s]).start()
            return carry
        lax.fori_loop(0, tm, start, 0, unroll=ROW_COPY_UNROLL)

    def wait(r, carry):
        for k in range(TOP_K):
            pltpu.make_async_copy(ys_ref.at[pl.ds(0, 1), :],
                                  buf_ref.at[slot, k, pl.ds(r, 1), :], sem.at[slot]).wait()
        return carry

    @pl.when(i == 0)
    def _():
        issue(pos_ref, slot)

    @pl.when(i + 1 < pl.num_programs(0))
    def _():
        issue(next_pos_ref, 1 - slot)

    lax.fori_loop(0, tm, wait, 0, unroll=ROW_COPY_UNROLL)
    gates = gate_ref[...]
    f = gates[:, 0:1] * buf_ref[slot, 0] + gates[:, 1:2] * buf_ref[slot, 1]
    y = _layer_norm(alpha * x_ref[...] + f, g_ref[...], b_ref[...])
    y = _gated_embedding(y, wg_ref, bg_ref, p_ref, wp_ref)
    o32_ref[...] = y
    o16_ref[...] = y.astype(o16_ref.dtype)


def _gather_mix_ln(pos, ys, gates, x_res, g, b, alpha, w_gate, b_gate, pb, w_proj):
    m, d = x_res.shape
    kp = pb.shape[1]
    tm = _tile(m, 512)
    last = m // tm - 1
    row = lambda i: (i, 0)
    fixed = lambda i: (0, 0)
    pos_tiles = _tile_positions(pos, tm)
    return pl.pallas_call(
        functools.partial(_gather_mix_ln_kernel, alpha=alpha),
        grid=(m // tm,),
        in_specs=[pl.BlockSpec((None, TOP_K, tm), lambda i: (i, 0, 0), memory_space=pltpu.SMEM),
                  pl.BlockSpec((None, TOP_K, tm), lambda i: (jnp.minimum(i + 1, last), 0, 0),
                               memory_space=pltpu.SMEM),
                  pl.BlockSpec(memory_space=pl.ANY),
                  pl.BlockSpec((tm, TOP_K), row),
                  pl.BlockSpec((tm, d), row),
                  pl.BlockSpec((1, d), fixed), pl.BlockSpec((1, d), fixed)]
        + _ple_specs(tm, d, kp, row, fixed),
        out_specs=[pl.BlockSpec((tm, d), row), pl.BlockSpec((tm, d), row)],
        out_shape=[jax.ShapeDtypeStruct((m, d), F32), jax.ShapeDtypeStruct((m, d), BF16)],
        scratch_shapes=[pltpu.VMEM((2, TOP_K, tm, d), F32), pltpu.SemaphoreType.DMA((2,))],
        compiler_params=_params(("arbitrary",), 56, disable_bounds_checks=True),
        name="gather_mix_ln",
    )(pos_tiles, pos_tiles, ys, gates, x_res, g.reshape(1, d), b.reshape(1, d),
      w_gate, b_gate.reshape(1, d), pb, w_proj)


def _moe_ffn_ln(x32, xb, x_words, w_router, w1, w3, w2, g, b, alpha, ple):
    m, _ = xb.shape
    n_exp = w1.shape[0]
    tm = _tile(m, 1024)
    experts, ranks, gates, counts = _router(xb, w_router.T.astype(BF16))

    counts = counts[:, 0]
    padded = (counts + tm - 1) // tm * tm
    ends = jnp.cumsum(padded)
    starts = ends - padded
    expert_ids = jnp.arange(n_exp, dtype=jnp.int32)[:, None, None]
    pos = jnp.sum(jnp.where(experts[None] == expert_ids, starts[:, None, None], 0), axis=0) + ranks
    assert (TOP_K * m) % tm == 0
    n_slots = TOP_K * m + n_exp * tm
    tile_index = jnp.arange(n_slots // tm, dtype=jnp.int32)
    tile_row = jnp.minimum(tile_index, jnp.maximum(ends[-1] // tm - 1, 0))
    tile_expert = jnp.sum((ends[None, :] <= (tile_row * tm)[:, None]).astype(jnp.int32), axis=1)
    tile_expert = jnp.minimum(tile_expert, n_exp - 1)
    prev_expert = jnp.concatenate([jnp.full((1,), -1, jnp.int32), tile_expert[:-1]])
    tile_first = (tile_expert != prev_expert).astype(jnp.int32)
    group_real_end = jnp.sum(jnp.where(tile_expert[:, None] == jnp.arange(n_exp)[None, :],
                                       (starts + counts)[None, :], 0), axis=1)
    real_rows = jnp.clip(group_real_end - tile_index * tm, 0, tm)
    tile_fill = jnp.where(tile_index * tm >= ends[-1], 0, jnp.where(real_rows > tm // 2, 2, 1))
    tiles = (tile_expert, tile_row, tile_fill.astype(jnp.int32), tile_first)

    fill_starts = (starts + counts) // SUBLANES * SUBLANES
    xs = _dispatch(jnp.concatenate([fill_starts, ends[-1:]]), pos, x_words, n_slots, tm)
    h = _expert_up(tiles, xs, w1, w3, tm)
    ys = _expert_down(tiles, h, w2, tm)
    return _gather_mix_ln(pos, ys, gates.T, x32, g, b, alpha, *ple)


def kernel(x, p, attn_w_qkv, attn_w_o, pool_w_in, pool_w_group, pool_scale, pool_w_o,
           ln_mix_g, ln_mix_b, ln_ffn_g, ln_ffn_b, ffn_w1, ffn_w3, ffn_w2,
           moe_router, moe_w1, moe_w3, moe_w2, ple_w_proj, ple_w_gate, ple_b_gate):
    batch, seq, d = x.shape
    depth = p.shape[0]
    att_w = N_HEADS * LANES
    assert attn_w_qkv.shape[2] == N_GROUPS * 3 * att_w
    alpha = (2 * depth) ** 0.25
    m = batch * seq
    bf = lambda t: t.astype(BF16)

    x32 = x.reshape(m, d)
    xb = None
    for i in range(depth):
        j = i // 2
        ple = (bf(ple_w_gate[i]), ple_b_gate[i], bf(p[i].reshape(m, -1)), bf(ple_w_proj[i]))
        if i % 2 == 0:
            outs, lses = [], []
            x_groups = _cast_and_regroup(x32, batch, seq)
            for grp, (_, dil) in enumerate(DILATED_GROUPS):
                qkv = _qkv_rope(x_groups[grp], attn_w_qkv[j], _rope_tables(seq, dil), grp, seq, att_w)
                o_g, lse_g = _dilated_attention(qkv, grp, seq, att_w)
                outs.append(o_g)
                lses.append(lse_g)
            x32, xb = _attn_out(outs, lses, bf(attn_w_o[j]), x32, ln_mix_g[i], ln_mix_b[i],
                                alpha, batch, seq)
            h = _swiglu_up(xb, ffn_w1[j], ffn_w3[j])
            x32, xb = _matmul_res_ln_ple(h, bf(ffn_w2[j]), x32, ln_ffn_g[i], ln_ffn_b[i], alpha, *ple)
        else:
            x32, xb, x_words = _pool_mixer_ln(
                xb, x32, bf(pool_w_in[j]), bf(pool_w_group[j]), pool_scale[j].reshape(-1),
                bf(pool_w_o[j]), ln_mix_g[i], ln_mix_b[i], alpha, batch, seq)
            x32, xb = _moe_ffn_ln(x32, xb, x_words, moe_router[j], moe_w1[j], moe_w3[j], moe_w2[j],
                                  ln_ffn_g[i], ln_ffn_b[i], alpha, ple)
    return x32.reshape(batch, seq, d)
```

```python
import functools
import math

import jax
import jax.numpy as jnp
from jax import lax
from jax.experimental import pallas as pl
from jax.experimental.pallas import tpu as pltpu

F32 = jnp.float32
BF16 = jnp.bfloat16

N_HEADS = 16
DILATED_GROUPS = ((128, 1), (512, 4), (2048, 16))
N_GROUPS = len(DILATED_GROUPS)
ATT_BLOCK = 128
ATT_BLOCKS_PER_STEP = 4
REGROUP_CHUNK = 256
SIDE_BLOCK_ELEMS = 1024 * 1024
ROPE_THETA = 10000.0
POOL_WINDOWS = (2, 4, 8, 16)
POOL_HALO = 16
TOP_K = 2
ROW_COPY_UNROLL = 4
LN_EPS = 1e-5
NEG_INF = -1e30

LANES = 128
SUBLANES = 8
MIB = 1024 * 1024


def _tile(dim, pref):
    t = min(dim, pref)
    while dim % t:
        t //= 2
    return t


def _params(semantics, vmem_mib, **kw):
    return pltpu.CompilerParams(dimension_semantics=semantics, vmem_limit_bytes=vmem_mib * MIB, **kw)


def _layer_norm(y, g, b):
    mu = jnp.mean(y, axis=-1, keepdims=True)
    yc = y - mu
    var = jnp.mean(yc * yc, axis=-1, keepdims=True)
    return yc * lax.rsqrt(var + LN_EPS) * g + b


def _pack_bf16_pairs(y):
    half = y.shape[1] // 2
    bits = lambda t: lax.bitcast_convert_type(t.astype(BF16).astype(F32), jnp.uint32)
    return (bits(y[:, :half]) >> 16) | bits(y[:, half:])


def _unpack_bf16_pairs(words):
    lo = lax.bitcast_convert_type(words << 16, F32)
    hi = lax.bitcast_convert_type(words & jnp.uint32(0xFFFF0000), F32)
    return jnp.concatenate([lo, hi], axis=1).astype(BF16)


def _regroup_kernel(x_ref, perm_ref, xb_ref, *group_refs, dils, chunk):
    xb = x_ref[...].astype(xb_ref.dtype)
    xb_ref[...] = xb
    for which, (dil, o_ref) in enumerate(zip(dils, group_refs)):
        slab = chunk // dil
        for ch in range(x_ref.shape[0] // chunk):
            grouped = jnp.dot(perm_ref[which], xb[ch * chunk:(ch + 1) * chunk],
                              preferred_element_type=F32).astype(o_ref.dtype)
            for r in range(dil):
                o_ref[r, ch * slab:(ch + 1) * slab, :] = grouped[r * slab:(r + 1) * slab]


def _cast_and_regroup(x32, batch, seq):
    m, d = x32.shape
    dils = tuple(dil for _, dil in DILATED_GROUPS if dil > 1)
    chunk = REGROUP_CHUNK
    tm = _tile(seq, 512)
    assert tm % chunk == 0 and all(chunk % (dil * 2 * SUBLANES) == 0 for dil in dils)
    src = jnp.arange(chunk)
    perms = jnp.stack([((src[:, None] % (chunk // dil)) * dil + src[:, None] // (chunk // dil)
                        == src[None, :]).astype(BF16) for dil in dils])
    s_tiles = seq // tm
    outs = pl.pallas_call(
        functools.partial(_regroup_kernel, dils=dils, chunk=chunk),
        grid=(batch, s_tiles),
        in_specs=[pl.BlockSpec((tm, d), lambda b, s: (b * s_tiles + s, 0)),
                  pl.BlockSpec(perms.shape, lambda b, s: (0, 0, 0))],
        out_specs=[pl.BlockSpec((tm, d), lambda b, s: (b * s_tiles + s, 0))]
        + [pl.BlockSpec((None, dil, tm // dil, d), lambda b, s: (b, 0, s, 0)) for dil in dils],
        out_shape=[jax.ShapeDtypeStruct((m, d), BF16)]
        + [jax.ShapeDtypeStruct((batch, dil, seq // dil, d), BF16) for dil in dils],
        compiler_params=_params(("parallel", "parallel"), 48),
        name="cast_and_regroup",
    )(x32, perms)
    by_dil = {1: outs[0], **{dil: o.reshape(m, d) for dil, o in zip(dils, outs[1:])}}
    return [by_dil[dil] for _, dil in DILATED_GROUPS]


def _qkv_rope_kernel(x_ref, w_ref, cos_ref, sin_ref, side_ref, o_ref, side_out_ref, wb_ref):
    @pl.when(pl.program_id(1) == 0)
    def _():
        wb_ref[...] = w_ref[...].astype(wb_ref.dtype)

    side_out_ref[...] = side_ref[...].astype(side_out_ref.dtype)

    acc = jnp.dot(x_ref[...], wb_ref[...], preferred_element_type=F32)
    cos = cos_ref[...]
    sin = sin_ref[...]
    for h in range(acc.shape[1] // LANES):
        t = acc[:, h * LANES:(h + 1) * LANES]
        rot = pltpu.roll(t, LANES // 2, 1)
        o_ref[:, h * LANES:(h + 1) * LANES] = (t * cos + rot * sin).astype(o_ref.dtype)


def _qkv_rope(x_g, w, tables, group, seq, att_w, side):
    m, k = x_g.shape
    cos_tab, sin_tab = tables
    tm = _tile(seq, 1024)
    tn = _tile(att_w, 1024)
    s_tiles = seq // tm
    per_role = att_w // tn
    per_group = 3 * per_role
    row_tiles = m // tm
    table_spec = pl.BlockSpec((None, tm, LANES), lambda j, i: (j // per_role, i % s_tiles, 0))
    side_cols = _tile(side.shape[1], 2048)
    side_rows = SIDE_BLOCK_ELEMS // side_cols
    assert side.shape[0] % side_rows == 0
    col_blocks = side.shape[1] // side_cols
    side_blocks = (side.shape[0] // side_rows) * col_blocks
    assert side_blocks <= per_group * row_tiles, "not enough grid steps to convert the side matrix"

    def side_index(j, i):
        block = jnp.minimum(j * row_tiles + i, side_blocks - 1)
        return (block // col_blocks, block % col_blocks)

    side_spec = pl.BlockSpec((side_rows, side_cols), side_index)
    return pl.pallas_call(
        _qkv_rope_kernel,
        grid=(per_group, row_tiles),
        in_specs=[pl.BlockSpec((tm, k), lambda j, i: (i, 0)),
                  pl.BlockSpec((k, tn), lambda j, i: (0, group * per_group + j)),
                  table_spec, table_spec, side_spec],
        out_specs=[pl.BlockSpec((tm, tn), lambda j, i: (i, j)), side_spec],
        out_shape=[jax.ShapeDtypeStruct((m, 3 * att_w), BF16),
                   jax.ShapeDtypeStruct(side.shape, BF16)],
        scratch_shapes=[pltpu.VMEM((k, tn), BF16)],
        compiler_params=_params(("arbitrary", "arbitrary"), 60),
        name=f"qkv_rope_g{group}",
    )(x_g, w, cos_tab, sin_tab, side)


def _rope_tables(seq, dil):
    half = LANES // 2
    inv = ROPE_THETA ** (-jnp.arange(half, dtype=F32) / half)
    row = jnp.arange(seq)
    n_sub = seq // dil
    pos = (row % n_sub) * dil + row // n_sub
    ang = pos.astype(F32)[:, None] * inv[None, :]
    cos, sin = jnp.cos(ang), jnp.sin(ang)
    cos_full = jnp.concatenate([cos, cos], axis=-1)
    sin_signed = jnp.concatenate([-sin, sin], axis=-1)
    return (jnp.stack([cos_full, cos_full, jnp.ones_like(cos_full)]),
            jnp.stack([sin_signed, sin_signed, jnp.zeros_like(cos_full)]))


def _attn_kernel(q_ref, kc_ref, kp_ref, vc_ref, vp_ref, o_ref, lse_ref, *, nb, n_back, scale):
    blk = kp_ref.shape[0]
    sub = q_ref.shape[0] // blk
    first_has_prev = lax.rem(pl.program_id(0) * sub, nb) > 0
    qi = lax.broadcasted_iota(jnp.int32, (blk, 2 * blk), 0) + blk
    ki = lax.broadcasted_iota(jnp.int32, (blk, 2 * blk), 1)
    dist = qi - ki
    band = (dist >= 0) & (dist <= n_back)
    bias_inner = jnp.where(band, 0.0, NEG_INF)
    bias_first = jnp.where(band & (first_has_prev | (ki >= blk)), 0.0, NEG_INF)
    lane = lax.broadcasted_iota(jnp.int32, (blk, LANES), 1)
    for t in range(sub):
        rows = slice(t * blk, (t + 1) * blk)
        before = slice((t - 1) * blk, t * blk)
        bias = bias_first if t == 0 else bias_inner
        lse_tile = jnp.zeros((blk, LANES), F32)
        for h in range(q_ref.shape[1] // LANES):
            hs = slice(h * LANES, (h + 1) * LANES)
            k_prev = kp_ref[:, hs] if t == 0 else kc_ref[before, hs]
            v_prev = vp_ref[:, hs] if t == 0 else vc_ref[before, hs]
            k2 = jnp.concatenate([k_prev, kc_ref[rows, hs]], axis=0)
            v2 = jnp.concatenate([v_prev, vc_ref[rows, hs]], axis=0)
            s = lax.dot_general(q_ref[rows, hs], k2, (((1,), (1,)), ((), ())),
                                preferred_element_type=F32) * scale + bias
            m = jnp.max(s, axis=-1, keepdims=True)
            e = jnp.exp(s - m)
            l = jnp.sum(e, axis=-1, keepdims=True)
            p = (e * (1.0 / l)).astype(v2.dtype)
            o_ref[rows, hs] = jnp.dot(p, v2, preferred_element_type=F32)
            lse_tile = jnp.where(lane == h, m + jnp.log(l), lse_tile)
        lse_ref[rows, :] = lse_tile


def _dilated_attention(qkv, group, seq, att_w):
    m = qkv.shape[0]
    window, dil = DILATED_GROUPS[group]
    nb = seq // (dil * ATT_BLOCK)
    sub = math.gcd(nb, ATT_BLOCKS_PER_STEP)
    assert seq % (dil * ATT_BLOCK) == 0 and window // dil <= ATT_BLOCK
    step_rows = sub * ATT_BLOCK

    def spec(role, prev):
        if prev:
            return pl.BlockSpec((ATT_BLOCK, att_w), lambda r: (jnp.maximum(sub * r - 1, 0), role))
        return pl.BlockSpec((step_rows, att_w), lambda r: (r, role))

    return pl.pallas_call(
        functools.partial(_attn_kernel, nb=nb, n_back=window // dil, scale=LANES ** -0.5),
        grid=(m // step_rows,),
        in_specs=[spec(0, False), spec(1, False), spec(1, True), spec(2, False), spec(2, True)],
        out_specs=[pl.BlockSpec((step_rows, att_w), lambda r: (r, 0)),
                   pl.BlockSpec((step_rows, LANES), lambda r: (r, 0))],
        out_shape=[jax.ShapeDtypeStruct((m, att_w), F32), jax.ShapeDtypeStruct((m, LANES), F32)],
        compiler_params=_params(("arbitrary",), 32),
        name=f"dilated_attn_g{group}",
    )(qkv, qkv, qkv, qkv, qkv)


def _token_order(ref, scr_ref):
    dil, n, w = ref.shape
    heads = range(w // LANES)
    if dil == 1:
        return [ref[0, :, h * LANES:(h + 1) * LANES] for h in heads]
    for r in range(dil):
        for h in heads:
            scr_ref[h, pl.ds(r, n, stride=dil), :] = ref[r, :, h * LANES:(h + 1) * LANES]
    return [scr_ref[h] for h in heads]


def _attn_out_kernel(*refs, alpha):
    o_refs, l_refs = refs[:N_GROUPS], refs[N_GROUPS:2 * N_GROUPS]
    w_ref, x_ref, g_ref, b_ref, o32_ref, o16_ref = refs[2 * N_GROUPS:2 * N_GROUPS + 6]
    scratch = refs[2 * N_GROUPS + 6:]
    o_scr, l_scr = scratch[:N_GROUPS], scratch[N_GROUPS:]
    outs = [_token_order(o_refs[g], o_scr[g]) for g in range(N_GROUPS)]
    lses = [_token_order(l_refs[g], l_scr[g])[0] for g in range(N_GROUPS)]
    m = functools.reduce(jnp.maximum, lses)
    es = [jnp.exp(l - m) for l in lses]
    inv = 1.0 / functools.reduce(jnp.add, es)
    wts = [e * inv for e in es]
    tm = x_ref.shape[0]
    heads = []
    for h in range(len(outs[0])):
        acc = None
        for g in range(N_GROUPS):
            term = jnp.broadcast_to(wts[g][:, h:h + 1], (tm, LANES)) * outs[g][h]
            acc = term if acc is None else acc + term
        heads.append(acc.astype(w_ref.dtype))
    mixed = jnp.concatenate(heads, axis=1)
    hid = jnp.dot(mixed, w_ref[...], preferred_element_type=F32)
    y = _layer_norm(alpha * x_ref[...] + hid, g_ref[...], b_ref[...])
    o32_ref[...] = y
    o16_ref[...] = y.astype(o16_ref.dtype)


def _attn_out(outs, lses, w_o, x_res, g, b, alpha, batch, seq):
    n_groups = len(outs)
    m, att_w = outs[0].shape
    d = w_o.shape[1]
    dils = [dil for _, dil in DILATED_GROUPS]
    tm = _tile(seq, 256)
    s_tiles = seq // tm
    assert all(tm % (8 * dil) == 0 for dil in dils)

    def group_view(t, g):
        return t.reshape(batch, dils[g], seq // dils[g], t.shape[-1])

    def group_spec(g, width):
        return pl.BlockSpec((None, dils[g], tm // dils[g], width),
                            lambda i: (i // s_tiles, 0, i % s_tiles, 0))

    row = lambda i: (i, 0)
    fixed = lambda i: (0, 0)
    return pl.pallas_call(
        functools.partial(_attn_out_kernel, alpha=alpha),
        grid=(m // tm,),
        in_specs=[group_spec(g, att_w) for g in range(n_groups)]
        + [group_spec(g, LANES) for g in range(n_groups)]
        + [pl.BlockSpec((att_w, d), fixed, pipeline_mode=pl.Buffered(1)),
           pl.BlockSpec((tm, d), row), pl.BlockSpec((1, d), fixed), pl.BlockSpec((1, d), fixed)],
        out_specs=[pl.BlockSpec((tm, d), row), pl.BlockSpec((tm, d), row)],
        out_shape=[jax.ShapeDtypeStruct((m, d), F32), jax.ShapeDtypeStruct((m, d), BF16)],
        scratch_shapes=[pltpu.VMEM((att_w // LANES, tm, LANES), F32) for _ in range(n_groups)]
        + [pltpu.VMEM((1, tm, LANES), F32) for _ in range(n_groups)],
        compiler_params=_params(("parallel",), 56),
        name="attn_out_ln",
    )(*[group_view(outs[g], g) for g in range(n_groups)],
      *[group_view(lses[g], g) for g in range(n_groups)],
      w_o, x_res, g.reshape(1, d), b.reshape(1, d))


def _gated_embedding(y, wg_ref, bg_ref, p_ref, wp_ref):
    z = jnp.dot(y.astype(wg_ref.dtype), wg_ref[...], preferred_element_type=F32) + bg_ref[...]
    e = jnp.dot(p_ref[...], wp_ref[...], preferred_element_type=F32)
    return y + jax.nn.sigmoid(z) * e


def _ple_specs(tm, d, kp, row, fixed):
    return [pl.BlockSpec((d, d), fixed, pipeline_mode=pl.Buffered(1)),
            pl.BlockSpec((1, d), fixed),
            pl.BlockSpec((tm, kp), row),
            pl.BlockSpec((kp, d), fixed, pipeline_mode=pl.Buffered(1))]


def _matmul_res_ln_ple_kernel(a_ref, w_ref, x_ref, g_ref, b_ref, wg_ref, bg_ref, p_ref, wp_ref,
                              o32_ref, o16_ref, *, alpha):
    h = jnp.dot(a_ref[...], w_ref[...], preferred_element_type=F32)
    y = _layer_norm(alpha * x_ref[...] + h, g_ref[...], b_ref[...])
    y = _gated_embedding(y, wg_ref, bg_ref, p_ref, wp_ref)
    o32_ref[...] = y
    o16_ref[...] = y.astype(o16_ref.dtype)


def _matmul_res_ln_ple(a, w, x_res, g, b, alpha, w_gate, b_gate, pb, w_proj):
    m, k = a.shape
    n = w.shape[1]
    kp = pb.shape[1]
    tm = _tile(m, 256)
    row = lambda i: (i, 0)
    fixed = lambda i: (0, 0)
    return pl.pallas_call(
        functools.partial(_matmul_res_ln_ple_kernel, alpha=alpha),
        grid=(m // tm,),
        in_specs=[pl.BlockSpec((tm, k), row),
                  pl.BlockSpec((k, n), fixed, pipeline_mode=pl.Buffered(1)),
                  pl.BlockSpec((tm, n), row), pl.BlockSpec((1, n), fixed), pl.BlockSpec((1, n), fixed)]
        + _ple_specs(tm, n, kp, row, fixed),
        out_specs=[pl.BlockSpec((tm, n), row), pl.BlockSpec((tm, n), row)],
        out_shape=[jax.ShapeDtypeStruct((m, n), F32), jax.ShapeDtypeStruct((m, n), BF16)],
        compiler_params=_params(("parallel",), 60),
        name="matmul_res_ln_ple",
    )(a, w, x_res, g.reshape(1, n), b.reshape(1, n), w_gate, b_gate.reshape(1, n), pb, w_proj)


def _swiglu_up_kernel(x_ref, w1_ref, w3_ref, o_ref, w1b_ref, w3b_ref):
    @pl.when(pl.program_id(1) == 0)
    def _():
        w1b_ref[...] = w1_ref[...].astype(w1b_ref.dtype)
        w3b_ref[...] = w3_ref[...].astype(w3b_ref.dtype)

    x = x_ref[...]
    a = jnp.dot(x, w1b_ref[...], preferred_element_type=F32)
    b = jnp.dot(x, w3b_ref[...], preferred_element_type=F32)
    o_ref[...] = (a * jax.nn.sigmoid(a) * b).astype(o_ref.dtype)


def _swiglu_up(xb, w1, w3):
    m, k = xb.shape
    n = w1.shape[1]
    tm, tn = _tile(m, 1024), _tile(n, 512)
    return pl.pallas_call(
        _swiglu_up_kernel,
        grid=(n // tn, m // tm),
        in_specs=[pl.BlockSpec((tm, k), lambda j, i: (i, 0)),
                  pl.BlockSpec((k, tn), lambda j, i: (0, j)),
                  pl.BlockSpec((k, tn), lambda j, i: (0, j))],
        out_specs=pl.BlockSpec((tm, tn), lambda j, i: (i, j)),
        out_shape=jax.ShapeDtypeStruct((m, n), BF16),
        scratch_shapes=[pltpu.VMEM((k, tn), BF16), pltpu.VMEM((k, tn), BF16)],
        compiler_params=_params(("parallel", "arbitrary"), 56),
        name="swiglu_up",
    )(xb, w1, w3)


def _pool_mixer_kernel(xb_ref, win_ref, wg_ref, sc_ref, wo_ref, x_ref, g_ref, b_ref,
                       o32_ref, o16_ref, opk_ref, buf_ref, *, alpha):
    s = pl.program_id(1)
    tm = xb_ref.shape[0]
    pd = wg_ref.shape[1]

    @pl.when(s == 0)
    def _():
        buf_ref[0:POOL_HALO, :] = jnp.zeros((POOL_HALO, buf_ref.shape[1]), F32)

    @pl.when(s > 0)
    def _():
        buf_ref[0:POOL_HALO, :] = buf_ref[tm:tm + POOL_HALO, :]

    buf_ref[POOL_HALO:POOL_HALO + tm, :] = jnp.dot(xb_ref[...], win_ref[...], preferred_element_type=F32)
    t = s * tm + lax.broadcasted_iota(jnp.int32, (tm, 1), 0)
    pieces = []
    for g, w in enumerate(POOL_WINDOWS):
        cols = slice(g * pd, (g + 1) * pd)
        u = buf_ref[POOL_HALO:POOL_HALO + tm, cols]
        tot = u
        for i in range(1, w):
            tot = tot + buf_ref[POOL_HALO - i:POOL_HALO - i + tm, cols]
        cnt = jnp.minimum(t + 1, w).astype(F32)
        mixed = (tot / cnt - u).astype(wg_ref.dtype)
        y = jnp.dot(mixed, wg_ref[g], preferred_element_type=F32) * sc_ref[:, cols]
        pieces.append(y.astype(wo_ref.dtype))
    hid = jnp.dot(jnp.concatenate(pieces, axis=1), wo_ref[...], preferred_element_type=F32)
    y = _layer_norm(alpha * x_ref[...] + hid, g_ref[...], b_ref[...])
    o32_ref[...] = y
    o16_ref[...] = y.astype(o16_ref.dtype)
    opk_ref[...] = _pack_bf16_pairs(y)


def _pool_mixer_ln(xb, x_res, w_in, w_group, scale, w_o, g, b, alpha, batch, seq):
    m, d = x_res.shape
    tm = _tile(seq, 256)
    s_tiles = seq // tm
    assert tm % SUBLANES == 0 and max(POOL_WINDOWS) <= POOL_HALO <= tm
    row = lambda bi, s: (bi * s_tiles + s, 0)
    fixed = lambda bi, s: (0, 0)
    resident = lambda shape, index: pl.BlockSpec(shape, index, pipeline_mode=pl.Buffered(1))
    return pl.pallas_call(
        functools.partial(_pool_mixer_kernel, alpha=alpha),
        grid=(batch, s_tiles),
        in_specs=[pl.BlockSpec((tm, d), row),
                  resident((d, d), fixed),
                  resident(w_group.shape, lambda bi, s: (0, 0, 0)),
                  pl.BlockSpec((1, d), fixed),
                  resident((d, d), fixed),
                  pl.BlockSpec((tm, d), row), pl.BlockSpec((1, d), fixed), pl.BlockSpec((1, d), fixed)],
        out_specs=[pl.BlockSpec((tm, d), row), pl.BlockSpec((tm, d), row),
                   pl.BlockSpec((tm, d // 2), row)],
        out_shape=[jax.ShapeDtypeStruct((m, d), F32), jax.ShapeDtypeStruct((m, d), BF16),
                   jax.ShapeDtypeStruct((m, d // 2), jnp.uint32)],
        scratch_shapes=[pltpu.VMEM((POOL_HALO + tm, d), F32)],
        compiler_params=_params(("parallel", "arbitrary"), 56),
        name="pool_mixer_ln",
    )(xb, w_in, w_group, scale.reshape(1, d), w_o, x_res, g.reshape(1, d), b.reshape(1, d))


def _router_kernel(x_ref, wr_ref, tri_ref, e_ref, rank_ref, gate_ref, cnt_ref, run_ref):
    @pl.when(pl.program_id(0) == 0)
    def _():
        run_ref[...] = jnp.zeros_like(run_ref)

    logits = lax.dot_general(wr_ref[...], x_ref[...], (((1,), (1,)), ((), ())),
                             preferred_element_type=F32)
    n_exp = logits.shape[0]
    eidx = lax.broadcasted_iota(jnp.int32, logits.shape, 0)
    m1 = jnp.max(logits, axis=0, keepdims=True)
    i1 = jnp.min(jnp.where(logits == m1, eidx, n_exp), axis=0, keepdims=True)
    rest = jnp.where(eidx == i1, -jnp.inf, logits)
    m2 = jnp.max(rest, axis=0, keepdims=True)
    i2 = jnp.min(jnp.where(rest == m2, eidx, n_exp), axis=0, keepdims=True)
    ex = jnp.exp(m2 - m1)
    g1 = 1.0 / (1.0 + ex)
    g2 = ex / (1.0 + ex)

    oh1 = (eidx == i1).astype(F32)
    oh2 = (eidx == i2).astype(F32)
    tri = tri_ref[...]
    c1 = jnp.dot(oh1.astype(tri.dtype), tri, preferred_element_type=F32)
    c2 = jnp.dot(oh2.astype(tri.dtype), tri, preferred_element_type=F32)
    tot1 = jnp.sum(oh1, axis=1, keepdims=True)
    tot2 = jnp.sum(oh2, axis=1, keepdims=True)
    run = run_ref[:, 0:1]
    r1 = jnp.sum(oh1 * (run + c1), axis=0, keepdims=True)
    r2 = jnp.sum(oh2 * (run + tot1 + c2), axis=0, keepdims=True)
    run = run + tot1 + tot2
    run_ref[...] = jnp.broadcast_to(run, run_ref.shape)

    e_ref[...] = jnp.concatenate([i1, i2], axis=0)
    rank_ref[...] = jnp.concatenate([r1, r2], axis=0).astype(jnp.int32)
    gate_ref[...] = jnp.concatenate([g1, g2], axis=0)
    cnt_ref[...] = jnp.broadcast_to(run, cnt_ref.shape).astype(jnp.int32)


def _router(xb, w_router_t):
    m, k = xb.shape
    n_exp = w_router_t.shape[0]
    tm = _tile(m, 512)
    tri = (jnp.arange(tm)[:, None] < jnp.arange(tm)[None, :]).astype(BF16)
    pair = lambda dt: jax.ShapeDtypeStruct((TOP_K, m), dt)
    pair_spec = pl.BlockSpec((TOP_K, tm), lambda i: (0, i))
    return pl.pallas_call(
        _router_kernel,
        grid=(m // tm,),
        in_specs=[pl.BlockSpec((tm, k), lambda i: (i, 0)),
                  pl.BlockSpec((n_exp, k), lambda i: (0, 0)),
                  pl.BlockSpec((tm, tm), lambda i: (0, 0))],
        out_specs=[pair_spec, pair_spec, pair_spec, pl.BlockSpec((n_exp, LANES), lambda i: (0, 0))],
        out_shape=[pair(jnp.int32), pair(jnp.int32), pair(F32),
                   jax.ShapeDtypeStruct((n_exp, LANES), jnp.int32)],
        scratch_shapes=[pltpu.VMEM((n_exp, LANES), F32)],
        compiler_params=_params(("arbitrary",), 32),
        name="router",
    )(xb, w_router_t, tri)


def _tile_positions(pos, tm):
    return pos.reshape(TOP_K, -1, tm).transpose(1, 0, 2)


def _dispatch_kernel(fill_ref, pos_ref, x_ref, xs_ref, stage_ref, zero_ref, sem, fill_sem):
    i = pl.program_id(0)
    tm = x_ref.shape[0]
    slot = lax.rem(i, 2)

    @pl.when(i == 0)
    def _():
        zero_ref[...] = jnp.zeros_like(zero_ref)
        fill_rows = zero_ref.shape[0]
        n_groups = fill_ref.shape[0] - 1

        def fill(start):
            return pltpu.make_async_copy(
                zero_ref, xs_ref.at[pl.ds(pl.multiple_of(start, SUBLANES), fill_rows), :], fill_sem)

        fills = [fill(fill_ref[e]) for e in range(n_groups)]
        for c in fills:
            c.start()
        for c in fills:
            c.wait()

        def fill_unused(t, carry):
            c = fill(t * fill_rows)
            c.start()
            c.wait()
            return carry
        lax.fori_loop(fill_ref[n_groups] // fill_rows, xs_ref.shape[0] // fill_rows, fill_unused, 0)

    def row_copy(s, r, k):
        return pltpu.make_async_copy(stage_ref.at[s, pl.ds(r, 1), :],
                                     xs_ref.at[pl.ds(pos_ref[k, r], 1), :], sem.at[s])

    def start(r, carry):
        for k in range(TOP_K):
            row_copy(slot, r, k).start()
        return carry

    def drain(s):
        def wait(r, carry):
            for k in range(TOP_K):
                pltpu.make_async_copy(stage_ref.at[s, pl.ds(r, 1), :],
                                      xs_ref.at[pl.ds(0, 1), :], sem.at[s]).wait()
            return carry
        lax.fori_loop(0, tm, wait, 0, unroll=ROW_COPY_UNROLL)

    stage_ref[slot] = x_ref[...]
    lax.fori_loop(0, tm, start, 0, unroll=ROW_COPY_UNROLL)

    @pl.when(i > 0)
    def _():
        drain(1 - slot)

    @pl.when(i == pl.num_programs(0) - 1)
    def _():
        drain(slot)


def _dispatch(fill_starts, pos, xp, n_slots, fill_rows):
    m, w = xp.shape
    tm = _tile(m, 512)
    grid_spec = pltpu.PrefetchScalarGridSpec(
        num_scalar_prefetch=1,
        grid=(m // tm,),
        in_specs=[pl.BlockSpec((None, TOP_K, tm), lambda i, fill: (i, 0, 0), memory_space=pltpu.SMEM),
                  pl.BlockSpec((tm, w), lambda i, fill: (i, 0))],
        out_specs=pl.BlockSpec(memory_space=pl.ANY),
        scratch_shapes=[pltpu.VMEM((2, tm, w), xp.dtype), pltpu.VMEM((fill_rows, w), xp.dtype),
                        pltpu.SemaphoreType.DMA((2,)), pltpu.SemaphoreType.DMA(())],
    )
    return pl.pallas_call(
        _dispatch_kernel,
        grid_spec=grid_spec,
        out_shape=jax.ShapeDtypeStruct((n_slots, w), xp.dtype),
        compiler_params=_params(("arbitrary",), 32, disable_bounds_checks=True),
        name="dispatch",
    )(fill_starts, _tile_positions(pos, tm), xp)


def _for_used_rows(fill_ref, out_ref, compute):
    tm = out_ref.shape[0]
    fill = fill_ref[pl.program_id(1)]

    @pl.when(fill == 2)
    def _():
        compute(slice(0, tm))

    @pl.when(fill == 1)
    def _():
        compute(slice(0, tm // 2))
        out_ref[tm // 2:, :] = jnp.zeros((tm - tm // 2, out_ref.shape[1]), out_ref.dtype)

    @pl.when(fill == 0)
    def _():
        out_ref[...] = jnp.zeros_like(out_ref)


def _expert_up_kernel(te_ref, tr_ref, tv_ref, xs_ref, w1_ref, w3_ref, h_ref):
    del te_ref, tr_ref

    def compute(rows):
        x = _unpack_bf16_pairs(xs_ref[rows, :])
        a = jnp.dot(x, w1_ref[...], preferred_element_type=F32)
        b = jnp.dot(x, w3_ref[...], preferred_element_type=F32)
        h_ref[rows, :] = (a * jax.nn.sigmoid(a) * b).astype(h_ref.dtype)

    _for_used_rows(tv_ref, h_ref, compute)


def _expert_specs():
    rows = lambda c, i, te, tr, tv: (tr[i], 0)
    weights = lambda c, i, te, tr, tv: (te[i], 0, c)
    out = lambda c, i, te, tr, tv: (i, c)
    return rows, weights, out


def _expert_up(tiles, xs, w1, w3, tm):
    n_slots = xs.shape[0]
    _, k, n = w1.shape
    tn = _tile(n, 512)
    rows, weights, out = _expert_specs()
    weight_spec = pl.BlockSpec((None, k, tn), weights)
    grid_spec = pltpu.PrefetchScalarGridSpec(
        num_scalar_prefetch=3,
        grid=(n // tn, n_slots // tm),
        in_specs=[pl.BlockSpec((tm, k // 2), rows), weight_spec, weight_spec],
        out_specs=pl.BlockSpec((tm, tn), out),
    )
    return pl.pallas_call(
        _expert_up_kernel,
        grid_spec=grid_spec,
        out_shape=jax.ShapeDtypeStruct((n_slots, n), BF16),
        compiler_params=_params(("parallel", "arbitrary"), 48),
        name="expert_up",
    )(*tiles, xs, w1, w3)


def _expert_down_kernel(te_ref, tr_ref, tv_ref, h_ref, w2_ref, y_ref):
    del te_ref, tr_ref

    def compute(rows):
        y_ref[rows, :] = jnp.dot(h_ref[rows, :], w2_ref[...], preferred_element_type=F32)

    _for_used_rows(tv_ref, y_ref, compute)


def _expert_down(tiles, h, w2, tm):
    n_slots, k = h.shape
    n = w2.shape[2]
    tn = _tile(n, 1024)
    rows, weights, out = _expert_specs()
    grid_spec = pltpu.PrefetchScalarGridSpec(
        num_scalar_prefetch=3,
        grid=(n // tn, n_slots // tm),
        in_specs=[pl.BlockSpec((tm, k), rows),
                  pl.BlockSpec((None, k, tn), weights, pipeline_mode=pl.Buffered(1))],
        out_specs=pl.BlockSpec((tm, tn), out),
    )
    return pl.pallas_call(
        _expert_down_kernel,
        grid_spec=grid_spec,
        out_shape=jax.ShapeDtypeStruct((n_slots, n), F32),
        compiler_params=_params(("parallel", "arbitrary"), 56),
        name="expert_down",
    )(*tiles, h, w2)


def _gather_mix_ln_kernel(pos_ref, next_pos_ref, ys_ref, gate_ref, x_ref, g_ref, b_ref, wg_ref,
                          bg_ref, p_ref, wp_ref, o32_ref, o16_ref, buf_ref, sem, *, alpha):
    i = pl.program_id(0)
    tm = x_ref.shape[0]
    slot = lax.rem(i, 2)

    def issue(idx_ref, s):
        def start(r, carry):
            for k in range(TOP_K):
                pltpu.make_async_copy(ys_ref.at[pl.ds(idx_ref[k, r], 1), :],
                                      buf_ref.at[s, k, pl.ds(r, 1), :], sem.at[s]).start()
            return carry
        lax.fori_loop(0, tm, start, 0, unroll=ROW_COPY_UNROLL)

    def wait(r, carry):
        for k in range(TOP_K):
            pltpu.make_async_copy(ys_ref.at[pl.ds(0, 1), :],
                                  buf_ref.at[slot, k, pl.ds(r, 1), :], sem.at[slot]).wait()
        return carry

    @pl.when(i == 0)
    def _():
        issue(pos_ref, slot)

    @pl.when(i + 1 < pl.num_programs(0))
    def _():
        issue(next_pos_ref, 1 - slot)

    lax.fori_loop(0, tm, wait, 0, unroll=ROW_COPY_UNROLL)
    gates = gate_ref[...]
    f = gates[:, 0:1] * buf_ref[slot, 0] + gates[:, 1:2] * buf_ref[slot, 1]
    y = _layer_norm(alpha * x_ref[...] + f, g_ref[...], b_ref[...])
    y = _gated_embedding(y, wg_ref, bg_ref, p_ref, wp_ref)
    o32_ref[...] = y
    o16_ref[...] = y.astype(o16_ref.dtype)


def _gather_mix_ln(pos, ys, gates, x_res, g, b, alpha, w_gate, b_gate, pb, w_proj):
    m, d = x_res.shape
    kp = pb.shape[1]
    tm = _tile(m, 512)
    last = m // tm - 1
    row = lambda i: (i, 0)
    fixed = lambda i: (0, 0)
    pos_tiles = _tile_positions(pos, tm)
    return pl.pallas_call(
        functools.partial(_gather_mix_ln_kernel, alpha=alpha),
        grid=(m // tm,),
        in_specs=[pl.BlockSpec((None, TOP_K, tm), lambda i: (i, 0, 0), memory_space=pltpu.SMEM),
                  pl.BlockSpec((None, TOP_K, tm), lambda i: (jnp.minimum(i + 1, last), 0, 0),
                               memory_space=pltpu.SMEM),
                  pl.BlockSpec(memory_space=pl.ANY),
                  pl.BlockSpec((tm, TOP_K), row),
                  pl.BlockSpec((tm, d), row),
                  pl.BlockSpec((1, d), fixed), pl.BlockSpec((1, d), fixed)]
        + _ple_specs(tm, d, kp, row, fixed),
        out_specs=[pl.BlockSpec((tm, d), row), pl.BlockSpec((tm, d), row)],
        out_shape=[jax.ShapeDtypeStruct((m, d), F32), jax.ShapeDtypeStruct((m, d), BF16)],
        scratch_shapes=[pltpu.VMEM((2, TOP_K, tm, d), F32), pltpu.SemaphoreType.DMA((2,))],
        compiler_params=_params(("arbitrary",), 56, disable_bounds_checks=True),
        name="gather_mix_ln",
    )(pos_tiles, pos_tiles, ys, gates, x_res, g.reshape(1, d), b.reshape(1, d),
      w_gate, b_gate.reshape(1, d), pb, w_proj)


def _moe_ffn_ln(x32, xb, x_words, w_router, w1, w3, w2, g, b, alpha, ple):
    m, _ = xb.shape
    n_exp = w1.shape[0]
    tm = _tile(m, 1024)
    experts, ranks, gates, counts = _router(xb, w_router.T.astype(BF16))

    counts = counts[:, 0]
    padded = (counts + tm - 1) // tm * tm
    ends = jnp.cumsum(padded)
    starts = ends - padded
    expert_ids = jnp.arange(n_exp, dtype=jnp.int32)[:, None, None]
    pos = jnp.sum(jnp.where(experts[None] == expert_ids, starts[:, None, None], 0), axis=0) + ranks
    assert (TOP_K * m) % tm == 0
    n_slots = TOP_K * m + n_exp * tm
    tile_index = jnp.arange(n_slots // tm, dtype=jnp.int32)
    tile_row = jnp.minimum(tile_index, jnp.maximum(ends[-1] // tm - 1, 0))
    tile_expert = jnp.sum((ends[None, :] <= (tile_row * tm)[:, None]).astype(jnp.int32), axis=1)
    tile_expert = jnp.minimum(tile_expert, n_exp - 1)
    group_real_end = jnp.sum(jnp.where(tile_expert[:, None] == jnp.arange(n_exp)[None, :],
                                       (starts + counts)[None, :], 0), axis=1)
    real_rows = jnp.clip(group_real_end - tile_index * tm, 0, tm)
    tile_fill = jnp.where(tile_index * tm >= ends[-1], 0, jnp.where(real_rows > tm // 2, 2, 1))
    tiles = (tile_expert, tile_row, tile_fill.astype(jnp.int32))

    fill_starts = (starts + counts) // SUBLANES * SUBLANES
    xs = _dispatch(jnp.concatenate([fill_starts, ends[-1:]]), pos, x_words, n_slots, tm)
    h = _expert_up(tiles, xs, w1, w3, tm)
    ys = _expert_down(tiles, h, w2, tm)
    return _gather_mix_ln(pos, ys, gates.T, x32, g, b, alpha, *ple)


def kernel(x, p, attn_w_qkv, attn_w_o, pool_w_in, pool_w_group, pool_scale, pool_w_o,
           ln_mix_g, ln_mix_b, ln_ffn_g, ln_ffn_b, ffn_w1, ffn_w3, ffn_w2,
           moe_router, moe_w1, moe_w3, moe_w2, ple_w_proj, ple_w_gate, ple_b_gate):
    batch, seq, d = x.shape
    depth = p.shape[0]
    att_w = N_HEADS * LANES
    assert attn_w_qkv.shape[2] == N_GROUPS * 3 * att_w
    assert depth % 2 == 0
    alpha = (2 * depth) ** 0.25
    m = batch * seq
    bf = lambda t: t.astype(BF16)

    x32 = x.reshape(m, d)
    xb = None
    for i in range(depth):
        j = i // 2
        ple = (bf(ple_w_gate[i]), ple_b_gate[i], bf(p[i].reshape(m, -1)), bf(ple_w_proj[i]))
        if i % 2 == 0:
            outs, lses, expert_w = [], [], []
            x_groups = _cast_and_regroup(x32, batch, seq)
            expert_w32 = (moe_w1[j], moe_w3[j], moe_w2[j])
            assert len(expert_w32) == N_GROUPS
            for grp, (_, dil) in enumerate(DILATED_GROUPS):
                side = expert_w32[grp]
                qkv, side_b = _qkv_rope(x_groups[grp], attn_w_qkv[j], _rope_tables(seq, dil), grp, seq,
                                        att_w, side.reshape(-1, side.shape[-1]))
                expert_w.append(side_b.reshape(side.shape))
                o_g, lse_g = _dilated_attention(qkv, grp, seq, att_w)
                outs.append(o_g)
                lses.append(lse_g)
            x32, xb = _attn_out(outs, lses, bf(attn_w_o[j]), x32, ln_mix_g[i], ln_mix_b[i],
                                alpha, batch, seq)
            h = _swiglu_up(xb, ffn_w1[j], ffn_w3[j])
            x32, xb = _matmul_res_ln_ple(h, bf(ffn_w2[j]), x32, ln_ffn_g[i], ln_ffn_b[i], alpha, *ple)
        else:
            x32, xb, x_words = _pool_mixer_ln(
                xb, x32, bf(pool_w_in[j]), bf(pool_w_group[j]), pool_scale[j].reshape(-1),
                bf(pool_w_o[j]), ln_mix_g[i], ln_mix_b[i], alpha, batch, seq)
            x32, xb = _moe_ffn_ln(x32, xb, x_words, moe_router[j], *expert_w,
                                  ln_ffn_g[i], ln_ffn_b[i], alpha, ple)
    return x32.reshape(batch, seq, d)
```

```python
import functools
import math

import jax
import jax.numpy as jnp
from jax import lax
from jax.experimental import pallas as pl
from jax.experimental.pallas import tpu as pltpu

F32 = jnp.float32
BF16 = jnp.bfloat16

N_HEADS = 16
DILATED_GROUPS = ((128, 1), (512, 4), (2048, 16))
N_GROUPS = len(DILATED_GROUPS)
ATT_BLOCK = 128
ATT_BLOCKS_PER_STEP = 4
REGROUP_CHUNK = 256
SIDE_BLOCK_ELEMS = 1024 * 1024
ROPE_THETA = 10000.0
POOL_WINDOWS = (2, 4, 8, 16)
POOL_HALO = 16
TOP_K = 2
ROW_COPY_UNROLL = 4
LN_EPS = 1e-5
NEG_INF = -1e30

LANES = 128
SUBLANES = 8
MIB = 1024 * 1024


def _tile(dim, pref):
    t = min(dim, pref)
    while dim % t:
        t //= 2
    return t


def _params(semantics, vmem_mib, **kw):
    return pltpu.CompilerParams(dimension_semantics=semantics, vmem_limit_bytes=vmem_mib * MIB, **kw)


def _layer_norm(y, g, b):
    mu = jnp.mean(y, axis=-1, keepdims=True)
    yc = y - mu
    var = jnp.mean(yc * yc, axis=-1, keepdims=True)
    return yc * lax.rsqrt(var + LN_EPS) * g + b


def _regroup_kernel(x_ref, perm_ref, xb_ref, *group_refs, dils, chunk):
    xb = x_ref[...].astype(xb_ref.dtype)
    xb_ref[...] = xb
    for which, (dil, o_ref) in enumerate(zip(dils, group_refs)):
        slab = chunk // dil
        for ch in range(x_ref.shape[0] // chunk):
            grouped = jnp.dot(perm_ref[which], xb[ch * chunk:(ch + 1) * chunk],
                              preferred_element_type=F32).astype(o_ref.dtype)
            for r in range(dil):
                o_ref[r, ch * slab:(ch + 1) * slab, :] = grouped[r * slab:(r + 1) * slab]


def _cast_and_regroup(x32, batch, seq):
    m, d = x32.shape
    dils = tuple(dil for _, dil in DILATED_GROUPS if dil > 1)
    chunk = REGROUP_CHUNK
    tm = _tile(seq, 512)
    assert tm % chunk == 0 and all(chunk % (dil * 2 * SUBLANES) == 0 for dil in dils)
    src = jnp.arange(chunk)
    perms = jnp.stack([((src[:, None] % (chunk // dil)) * dil + src[:, None] // (chunk // dil)
                        == src[None, :]).astype(BF16) for dil in dils])
    s_tiles = seq // tm
    outs = pl.pallas_call(
        functools.partial(_regroup_kernel, dils=dils, chunk=chunk),
        grid=(batch, s_tiles),
        in_specs=[pl.BlockSpec((tm, d), lambda b, s: (b * s_tiles + s, 0)),
                  pl.BlockSpec(perms.shape, lambda b, s: (0, 0, 0))],
        out_specs=[pl.BlockSpec((tm, d), lambda b, s: (b * s_tiles + s, 0))]
        + [pl.BlockSpec((None, dil, tm // dil, d), lambda b, s: (b, 0, s, 0)) for dil in dils],
        out_shape=[jax.ShapeDtypeStruct((m, d), BF16)]
        + [jax.ShapeDtypeStruct((batch, dil, seq // dil, d), BF16) for dil in dils],
        compiler_params=_params(("parallel", "parallel"), 48),
        name="cast_and_regroup",
    )(x32, perms)
    by_dil = {1: outs[0], **{dil: o.reshape(m, d) for dil, o in zip(dils, outs[1:])}}
    return [by_dil[dil] for _, dil in DILATED_GROUPS]


def _qkv_rope_kernel(x_ref, w_ref, cos_ref, sin_ref, side_ref, o_ref, side_out_ref, wb_ref):
    @pl.when(pl.program_id(1) == 0)
    def _():
        wb_ref[...] = w_ref[...].astype(wb_ref.dtype)

    side_out_ref[...] = side_ref[...].astype(side_out_ref.dtype)

    acc = jnp.dot(x_ref[...], wb_ref[...], preferred_element_type=F32)
    cos = cos_ref[...]
    sin = sin_ref[...]
    for h in range(acc.shape[1] // LANES):
        t = acc[:, h * LANES:(h + 1) * LANES]
        rot = pltpu.roll(t, LANES // 2, 1)
        o_ref[:, h * LANES:(h + 1) * LANES] = (t * cos + rot * sin).astype(o_ref.dtype)


def _qkv_rope(x_g, w, tables, group, seq, att_w, side):
    m, k = x_g.shape
    cos_tab, sin_tab = tables
    tm = _tile(seq, 1024)
    tn = _tile(att_w, 1024)
    s_tiles = seq // tm
    per_role = att_w // tn
    per_group = 3 * per_role
    row_tiles = m // tm
    table_spec = pl.BlockSpec((None, tm, LANES), lambda j, i: (j // per_role, i % s_tiles, 0))
    side_cols = _tile(side.shape[1], 2048)
    side_rows = SIDE_BLOCK_ELEMS // side_cols
    assert side.shape[0] % side_rows == 0
    col_blocks = side.shape[1] // side_cols
    side_blocks = (side.shape[0] // side_rows) * col_blocks
    assert side_blocks <= per_group * row_tiles, "not enough grid steps to convert the side matrix"

    def side_index(j, i):
        block = jnp.minimum(j * row_tiles + i, side_blocks - 1)
        return (block // col_blocks, block % col_blocks)

    side_spec = pl.BlockSpec((side_rows, side_cols), side_index)
    return pl.pallas_call(
        _qkv_rope_kernel,
        grid=(per_group, row_tiles),
        in_specs=[pl.BlockSpec((tm, k), lambda j, i: (i, 0)),
                  pl.BlockSpec((k, tn), lambda j, i: (0, group * per_group + j)),
                  table_spec, table_spec, side_spec],
        out_specs=[pl.BlockSpec((tm, tn), lambda j, i: (i, j)), side_spec],
        out_shape=[jax.ShapeDtypeStruct((m, 3 * att_w), BF16),
                   jax.ShapeDtypeStruct(side.shape, BF16)],
        scratch_shapes=[pltpu.VMEM((k, tn), BF16)],
        compiler_params=_params(("arbitrary", "arbitrary"), 60),
        name=f"qkv_rope_g{group}",
    )(x_g, w, cos_tab, sin_tab, side)


def _rope_tables(seq, dil):
    half = LANES // 2
    inv = ROPE_THETA ** (-jnp.arange(half, dtype=F32) / half)
    row = jnp.arange(seq)
    n_sub = seq // dil
    pos = (row % n_sub) * dil + row // n_sub
    ang = pos.astype(F32)[:, None] * inv[None, :]
    cos, sin = jnp.cos(ang), jnp.sin(ang)
    cos_full = jnp.concatenate([cos, cos], axis=-1)
    sin_signed = jnp.concatenate([-sin, sin], axis=-1)
    return (jnp.stack([cos_full, cos_full, jnp.ones_like(cos_full)]),
            jnp.stack([sin_signed, sin_signed, jnp.zeros_like(cos_full)]))


def _attn_kernel(q_ref, kc_ref, kp_ref, vc_ref, vp_ref, o_ref, lse_ref, *, nb, n_back, scale):
    blk = kp_ref.shape[0]
    sub = q_ref.shape[0] // blk
    first_has_prev = lax.rem(pl.program_id(0) * sub, nb) > 0
    qi = lax.broadcasted_iota(jnp.int32, (blk, 2 * blk), 0) + blk
    ki = lax.broadcasted_iota(jnp.int32, (blk, 2 * blk), 1)
    dist = qi - ki
    band = (dist >= 0) & (dist <= n_back)
    bias_inner = jnp.where(band, 0.0, NEG_INF)
    bias_first = jnp.where(band & (first_has_prev | (ki >= blk)), 0.0, NEG_INF)
    lane = lax.broadcasted_iota(jnp.int32, (blk, LANES), 1)
    for t in range(sub):
        rows = slice(t * blk, (t + 1) * blk)
        before = slice((t - 1) * blk, t * blk)
        bias = bias_first if t == 0 else bias_inner
        lse_tile = jnp.zeros((blk, LANES), F32)
        for h in range(q_ref.shape[1] // LANES):
            hs = slice(h * LANES, (h + 1) * LANES)
            k_prev = kp_ref[:, hs] if t == 0 else kc_ref[before, hs]
            v_prev = vp_ref[:, hs] if t == 0 else vc_ref[before, hs]
            k2 = jnp.concatenate([k_prev, kc_ref[rows, hs]], axis=0)
            v2 = jnp.concatenate([v_prev, vc_ref[rows, hs]], axis=0)
            s = lax.dot_general(q_ref[rows, hs], k2, (((1,), (1,)), ((), ())),
                                preferred_element_type=F32) * scale + bias
            m = jnp.max(s, axis=-1, keepdims=True)
            e = jnp.exp(s - m)
            l = jnp.sum(e, axis=-1, keepdims=True)
            p = (e * (1.0 / l)).astype(v2.dtype)
            o_ref[rows, hs] = jnp.dot(p, v2, preferred_element_type=F32)
            lse_tile = jnp.where(lane == h, m + jnp.log(l), lse_tile)
        lse_ref[rows, :] = lse_tile


def _dilated_attention(qkv, group, seq, att_w):
    m = qkv.shape[0]
    window, dil = DILATED_GROUPS[group]
    nb = seq // (dil * ATT_BLOCK)
    sub = math.gcd(nb, ATT_BLOCKS_PER_STEP)
    assert seq % (dil * ATT_BLOCK) == 0 and window // dil <= ATT_BLOCK
    step_rows = sub * ATT_BLOCK

    def spec(role, prev):
        if prev:
            return pl.BlockSpec((ATT_BLOCK, att_w), lambda r: (jnp.maximum(sub * r - 1, 0), role))
        return pl.BlockSpec((step_rows, att_w), lambda r: (r, role))

    return pl.pallas_call(
        functools.partial(_attn_kernel, nb=nb, n_back=window // dil, scale=LANES ** -0.5),
        grid=(m // step_rows,),
        in_specs=[spec(0, False), spec(1, False), spec(1, True), spec(2, False), spec(2, True)],
        out_specs=[pl.BlockSpec((step_rows, att_w), lambda r: (r, 0)),
                   pl.BlockSpec((step_rows, LANES), lambda r: (r, 0))],
        out_shape=[jax.ShapeDtypeStruct((m, att_w), F32), jax.ShapeDtypeStruct((m, LANES), F32)],
        compiler_params=_params(("arbitrary",), 32),
        name=f"dilated_attn_g{group}",
    )(qkv, qkv, qkv, qkv, qkv)


def _token_order(ref, scr_ref):
    dil, n, w = ref.shape
    heads = range(w // LANES)
    if dil == 1:
        return [ref[0, :, h * LANES:(h + 1) * LANES] for h in heads]
    for r in range(dil):
        for h in heads:
            scr_ref[h, pl.ds(r, n, stride=dil), :] = ref[r, :, h * LANES:(h + 1) * LANES]
    return [scr_ref[h] for h in heads]


def _attn_out_kernel(*refs, alpha):
    o_refs, l_refs = refs[:N_GROUPS], refs[N_GROUPS:2 * N_GROUPS]
    w_ref, x_ref, g_ref, b_ref, o32_ref, o16_ref = refs[2 * N_GROUPS:2 * N_GROUPS + 6]
    scratch = refs[2 * N_GROUPS + 6:]
    o_scr, l_scr = scratch[:N_GROUPS], scratch[N_GROUPS:]
    outs = [_token_order(o_refs[g], o_scr[g]) for g in range(N_GROUPS)]
    lses = [_token_order(l_refs[g], l_scr[g])[0] for g in range(N_GROUPS)]
    m = functools.reduce(jnp.maximum, lses)
    es = [jnp.exp(l - m) for l in lses]
    inv = 1.0 / functools.reduce(jnp.add, es)
    wts = [e * inv for e in es]
    tm = x_ref.shape[0]
    heads = []
    for h in range(len(outs[0])):
        acc = None
        for g in range(N_GROUPS):
            term = jnp.broadcast_to(wts[g][:, h:h + 1], (tm, LANES)) * outs[g][h]
            acc = term if acc is None else acc + term
        heads.append(acc.astype(w_ref.dtype))
    mixed = jnp.concatenate(heads, axis=1)
    hid = jnp.dot(mixed, w_ref[...], preferred_element_type=F32)
    y = _layer_norm(alpha * x_ref[...] + hid, g_ref[...], b_ref[...])
    o32_ref[...] = y
    o16_ref[...] = y.astype(o16_ref.dtype)


def _attn_out(outs, lses, w_o, x_res, g, b, alpha, batch, seq):
    n_groups = len(outs)
    m, att_w = outs[0].shape
    d = w_o.shape[1]
    dils = [dil for _, dil in DILATED_GROUPS]
    tm = _tile(seq, 256)
    s_tiles = seq // tm
    assert all(tm % (8 * dil) == 0 for dil in dils)

    def group_view(t, g):
        return t.reshape(batch, dils[g], seq // dils[g], t.shape[-1])

    def group_spec(g, width):
        return pl.BlockSpec((None, dils[g], tm // dils[g], width),
                            lambda i: (i // s_tiles, 0, i % s_tiles, 0))

    row = lambda i: (i, 0)
    fixed = lambda i: (0, 0)
    return pl.pallas_call(
        functools.partial(_attn_out_kernel, alpha=alpha),
        grid=(m // tm,),
        in_specs=[group_spec(g, att_w) for g in range(n_groups)]
        + [group_spec(g, LANES) for g in range(n_groups)]
        + [pl.BlockSpec((att_w, d), fixed, pipeline_mode=pl.Buffered(1)),
           pl.BlockSpec((tm, d), row), pl.BlockSpec((1, d), fixed), pl.BlockSpec((1, d), fixed)],
        out_specs=[pl.BlockSpec((tm, d), row), pl.BlockSpec((tm, d), row)],
        out_shape=[jax.ShapeDtypeStruct((m, d), F32), jax.ShapeDtypeStruct((m, d), BF16)],
        scratch_shapes=[pltpu.VMEM((att_w // LANES, tm, LANES), F32) for _ in range(n_groups)]
        + [pltpu.VMEM((1, tm, LANES), F32) for _ in range(n_groups)],
        compiler_params=_params(("parallel",), 56),
        name="attn_out_ln",
    )(*[group_view(outs[g], g) for g in range(n_groups)],
      *[group_view(lses[g], g) for g in range(n_groups)],
      w_o, x_res, g.reshape(1, d), b.reshape(1, d))


def _gated_embedding(y, wg_ref, bg_ref, p_ref, wp_ref):
    z = jnp.dot(y.astype(wg_ref.dtype), wg_ref[...], preferred_element_type=F32) + bg_ref[...]
    e = jnp.dot(p_ref[...], wp_ref[...], preferred_element_type=F32)
    return y + jax.nn.sigmoid(z) * e


def _ple_specs(tm, d, kp, row, fixed):
    return [pl.BlockSpec((d, d), fixed, pipeline_mode=pl.Buffered(1)),
            pl.BlockSpec((1, d), fixed),
            pl.BlockSpec((tm, kp), row),
            pl.BlockSpec((kp, d), fixed, pipeline_mode=pl.Buffered(1))]


def _matmul_res_ln_ple_kernel(a_ref, w_ref, x_ref, g_ref, b_ref, wg_ref, bg_ref, p_ref, wp_ref,
                              o32_ref, o16_ref, *, alpha):
    h = jnp.dot(a_ref[...], w_ref[...], preferred_element_type=F32)
    y = _layer_norm(alpha * x_ref[...] + h, g_ref[...], b_ref[...])
    y = _gated_embedding(y, wg_ref, bg_ref, p_ref, wp_ref)
    o32_ref[...] = y
    o16_ref[...] = y.astype(o16_ref.dtype)


def _matmul_res_ln_ple(a, w, x_res, g, b, alpha, w_gate, b_gate, pb, w_proj):
    m, k = a.shape
    n = w.shape[1]
    kp = pb.shape[1]
    tm = _tile(m, 256)
    row = lambda i: (i, 0)
    fixed = lambda i: (0, 0)
    return pl.pallas_call(
        functools.partial(_matmul_res_ln_ple_kernel, alpha=alpha),
        grid=(m // tm,),
        in_specs=[pl.BlockSpec((tm, k), row),
                  pl.BlockSpec((k, n), fixed, pipeline_mode=pl.Buffered(1)),
                  pl.BlockSpec((tm, n), row), pl.BlockSpec((1, n), fixed), pl.BlockSpec((1, n), fixed)]
        + _ple_specs(tm, n, kp, row, fixed),
        out_specs=[pl.BlockSpec((tm, n), row), pl.BlockSpec((tm, n), row)],
        out_shape=[jax.ShapeDtypeStruct((m, n), F32), jax.ShapeDtypeStruct((m, n), BF16)],
        compiler_params=_params(("parallel",), 60),
        name="matmul_res_ln_ple",
    )(a, w, x_res, g.reshape(1, n), b.reshape(1, n), w_gate, b_gate.reshape(1, n), pb, w_proj)


def _swiglu_up_kernel(x_ref, w1_ref, w3_ref, o_ref, w1b_ref, w3b_ref):
    @pl.when(pl.program_id(1) == 0)
    def _():
        w1b_ref[...] = w1_ref[...].astype(w1b_ref.dtype)
        w3b_ref[...] = w3_ref[...].astype(w3b_ref.dtype)

    x = x_ref[...]
    a = jnp.dot(x, w1b_ref[...], preferred_element_type=F32)
    b = jnp.dot(x, w3b_ref[...], preferred_element_type=F32)
    o_ref[...] = (a * jax.nn.sigmoid(a) * b).astype(o_ref.dtype)


def _swiglu_up(xb, w1, w3):
    m, k = xb.shape
    n = w1.shape[1]
    tm, tn = _tile(m, 1024), _tile(n, 512)
    return pl.pallas_call(
        _swiglu_up_kernel,
        grid=(n // tn, m // tm),
        in_specs=[pl.BlockSpec((tm, k), lambda j, i: (i, 0)),
                  pl.BlockSpec((k, tn), lambda j, i: (0, j)),
                  pl.BlockSpec((k, tn), lambda j, i: (0, j))],
        out_specs=pl.BlockSpec((tm, tn), lambda j, i: (i, j)),
        out_shape=jax.ShapeDtypeStruct((m, n), BF16),
        scratch_shapes=[pltpu.VMEM((k, tn), BF16), pltpu.VMEM((k, tn), BF16)],
        compiler_params=_params(("parallel", "arbitrary"), 56),
        name="swiglu_up",
    )(xb, w1, w3)


def _pool_mixer_kernel(xb_ref, win_ref, wg_ref, sc_ref, wo_ref, x_ref, g_ref, b_ref,
                       o32_ref, o16_ref, buf_ref, *, alpha):
    s = pl.program_id(1)
    tm = xb_ref.shape[0]
    pd = wg_ref.shape[1]

    @pl.when(s == 0)
    def _():
        buf_ref[0:POOL_HALO, :] = jnp.zeros((POOL_HALO, buf_ref.shape[1]), F32)

    @pl.when(s > 0)
    def _():
        buf_ref[0:POOL_HALO, :] = buf_ref[tm:tm + POOL_HALO, :]

    buf_ref[POOL_HALO:POOL_HALO + tm, :] = jnp.dot(xb_ref[...], win_ref[...], preferred_element_type=F32)
    t = s * tm + lax.broadcasted_iota(jnp.int32, (tm, 1), 0)
    pieces = []
    for g, w in enumerate(POOL_WINDOWS):
        cols = slice(g * pd, (g + 1) * pd)
        u = buf_ref[POOL_HALO:POOL_HALO + tm, cols]
        tot = u
        for i in range(1, w):
            tot = tot + buf_ref[POOL_HALO - i:POOL_HALO - i + tm, cols]
        cnt = jnp.minimum(t + 1, w).astype(F32)
        mixed = (tot / cnt - u).astype(wg_ref.dtype)
        y = jnp.dot(mixed, wg_ref[g], preferred_element_type=F32) * sc_ref[:, cols]
        pieces.append(y.astype(wo_ref.dtype))
    hid = jnp.dot(jnp.concatenate(pieces, axis=1), wo_ref[...], preferred_element_type=F32)
    y = _layer_norm(alpha * x_ref[...] + hid, g_ref[...], b_ref[...])
    o32_ref[...] = y
    o16_ref[...] = y.astype(o16_ref.dtype)


def _pool_mixer_ln(xb, x_res, w_in, w_group, scale, w_o, g, b, alpha, batch, seq):
    m, d = x_res.shape
    tm = _tile(seq, 256)
    s_tiles = seq // tm
    assert tm % SUBLANES == 0 and max(POOL_WINDOWS) <= POOL_HALO <= tm
    row = lambda bi, s: (bi * s_tiles + s, 0)
    fixed = lambda bi, s: (0, 0)
    resident = lambda shape, index: pl.BlockSpec(shape, index, pipeline_mode=pl.Buffered(1))
    return pl.pallas_call(
        functools.partial(_pool_mixer_kernel, alpha=alpha),
        grid=(batch, s_tiles),
        in_specs=[pl.BlockSpec((tm, d), row),
                  resident((d, d), fixed),
                  resident(w_group.shape, lambda bi, s: (0, 0, 0)),
                  pl.BlockSpec((1, d), fixed),
                  resident((d, d), fixed),
                  pl.BlockSpec((tm, d), row), pl.BlockSpec((1, d), fixed), pl.BlockSpec((1, d), fixed)],
        out_specs=[pl.BlockSpec((tm, d), row), pl.BlockSpec((tm, d), row)],
        out_shape=[jax.ShapeDtypeStruct((m, d), F32), jax.ShapeDtypeStruct((m, d), BF16)],
        scratch_shapes=[pltpu.VMEM((POOL_HALO + tm, d), F32)],
        compiler_params=_params(("parallel", "arbitrary"), 56),
        name="pool_mixer_ln",
    )(xb, w_in, w_group, scale.reshape(1, d), w_o, x_res, g.reshape(1, d), b.reshape(1, d))


def _router_kernel(x_ref, wr_ref, tri_ref, e_ref, rank_ref, gate_ref, cnt_ref, run_ref):
    @pl.when(pl.program_id(0) == 0)
    def _():
        run_ref[...] = jnp.zeros_like(run_ref)

    logits = lax.dot_general(wr_ref[...], x_ref[...], (((1,), (1,)), ((), ())),
                             preferred_element_type=F32)
    n_exp = logits.shape[0]
    eidx = lax.broadcasted_iota(jnp.int32, logits.shape, 0)
    m1 = jnp.max(logits, axis=0, keepdims=True)
    i1 = jnp.min(jnp.where(logits == m1, eidx, n_exp), axis=0, keepdims=True)
    rest = jnp.where(eidx == i1, -jnp.inf, logits)
    m2 = jnp.max(rest, axis=0, keepdims=True)
    i2 = jnp.min(jnp.where(rest == m2, eidx, n_exp), axis=0, keepdims=True)
    ex = jnp.exp(m2 - m1)
    g1 = 1.0 / (1.0 + ex)
    g2 = ex / (1.0 + ex)

    oh1 = (eidx == i1).astype(F32)
    oh2 = (eidx == i2).astype(F32)
    tri = tri_ref[...]
    c1 = jnp.dot(oh1.astype(tri.dtype), tri, preferred_element_type=F32)
    c2 = jnp.dot(oh2.astype(tri.dtype), tri, preferred_element_type=F32)
    tot1 = jnp.sum(oh1, axis=1, keepdims=True)
    tot2 = jnp.sum(oh2, axis=1, keepdims=True)
    run = run_ref[:, 0:1]
    r1 = jnp.sum(oh1 * (run + c1), axis=0, keepdims=True)
    r2 = jnp.sum(oh2 * (run + tot1 + c2), axis=0, keepdims=True)
    run = run + tot1 + tot2
    run_ref[...] = jnp.broadcast_to(run, run_ref.shape)

    e_ref[...] = jnp.concatenate([i1, i2], axis=0)
    rank_ref[...] = jnp.concatenate([r1, r2], axis=0).astype(jnp.int32)
    gate_ref[...] = jnp.concatenate([g1, g2], axis=0)
    cnt_ref[...] = jnp.broadcast_to(run, cnt_ref.shape).astype(jnp.int32)


def _router(xb, w_router_t):
    m, k = xb.shape
    n_exp = w_router_t.shape[0]
    tm = _tile(m, 512)
    tri = (jnp.arange(tm)[:, None] < jnp.arange(tm)[None, :]).astype(BF16)
    pair = lambda dt: jax.ShapeDtypeStruct((TOP_K, m), dt)
    pair_spec = pl.BlockSpec((TOP_K, tm), lambda i: (0, i))
    return pl.pallas_call(
        _router_kernel,
        grid=(m // tm,),
        in_specs=[pl.BlockSpec((tm, k), lambda i: (i, 0)),
                  pl.BlockSpec((n_exp, k), lambda i: (0, 0)),
                  pl.BlockSpec((tm, tm), lambda i: (0, 0))],
        out_specs=[pair_spec, pair_spec, pair_spec, pl.BlockSpec((n_exp, LANES), lambda i: (0, 0))],
        out_shape=[pair(jnp.int32), pair(jnp.int32), pair(F32),
                   jax.ShapeDtypeStruct((n_exp, LANES), jnp.int32)],
        scratch_shapes=[pltpu.VMEM((n_exp, LANES), F32)],
        compiler_params=_params(("arbitrary",), 32),
        name="router",
    )(xb, w_router_t, tri)


def _tile_positions(pos, tm):
    return pos.reshape(TOP_K, -1, tm).transpose(1, 0, 2)


def _dispatch_kernel(fill_ref, pos_ref, x_ref, xs_ref, stage_ref, zero_ref, sem, fill_sem):
    i = pl.program_id(0)
    tm = x_ref.shape[0]
    slot = lax.rem(i, 2)

    @pl.when(i == 0)
    def _():
        zero_ref[...] = jnp.zeros_like(zero_ref)
        fill_rows = zero_ref.shape[0]
        n_groups = fill_ref.shape[0] - 1

        def fill(start):
            return pltpu.make_async_copy(
                zero_ref, xs_ref.at[pl.ds(pl.multiple_of(start, SUBLANES), fill_rows), :], fill_sem)

        fills = [fill(fill_ref[e]) for e in range(n_groups)]
        for c in fills:
            c.start()
        for c in fills:
            c.wait()

        def fill_unused(t, carry):
            c = fill(t * fill_rows)
            c.start()
            c.wait()
            return carry
        lax.fori_loop(fill_ref[n_groups] // fill_rows, xs_ref.shape[0] // fill_rows, fill_unused, 0)

    def row_copy(s, r, k):
        return pltpu.make_async_copy(stage_ref.at[s, pl.ds(r, 1), :],
                                     xs_ref.at[pl.ds(pos_ref[k, r], 1), :], sem.at[s])

    def start(r, carry):
        for k in range(TOP_K):
            row_copy(slot, r, k).start()
        return carry

    def drain(s):
        def wait(r, carry):
            for k in range(TOP_K):
                pltpu.make_async_copy(stage_ref.at[s, pl.ds(r, 1), :],
                                      xs_ref.at[pl.ds(0, 1), :], sem.at[s]).wait()
            return carry
        lax.fori_loop(0, tm, wait, 0, unroll=ROW_COPY_UNROLL)

    stage_ref[slot] = x_ref[...]
    lax.fori_loop(0, tm, start, 0, unroll=ROW_COPY_UNROLL)

    @pl.when(i > 0)
    def _():
        drain(1 - slot)

    @pl.when(i == pl.num_programs(0) - 1)
    def _():
        drain(slot)


def _dispatch(fill_starts, pos, xp, n_slots, fill_rows):
    m, w = xp.shape
    tm = _tile(m, 512)
    grid_spec = pltpu.PrefetchScalarGridSpec(
        num_scalar_prefetch=1,
        grid=(m // tm,),
        in_specs=[pl.BlockSpec((None, TOP_K, tm), lambda i, fill: (i, 0, 0), memory_space=pltpu.SMEM),
                  pl.BlockSpec((tm, w), lambda i, fill: (i, 0))],
        out_specs=pl.BlockSpec(memory_space=pl.ANY),
        scratch_shapes=[pltpu.VMEM((2, tm, w), xp.dtype), pltpu.VMEM((fill_rows, w), xp.dtype),
                        pltpu.SemaphoreType.DMA((2,)), pltpu.SemaphoreType.DMA(())],
    )
    return pl.pallas_call(
        _dispatch_kernel,
        grid_spec=grid_spec,
        out_shape=jax.ShapeDtypeStruct((n_slots, w), xp.dtype),
        compiler_params=_params(("arbitrary",), 48, disable_bounds_checks=True),
        name="dispatch",
    )(fill_starts, _tile_positions(pos, tm), xp)


def _for_used_rows(fill_ref, out_ref, compute):
    tm = out_ref.shape[0]
    fill = fill_ref[pl.program_id(1)]

    @pl.when(fill == 2)
    def _():
        compute(slice(0, tm))

    @pl.when(fill == 1)
    def _():
        compute(slice(0, tm // 2))
        out_ref[tm // 2:, :] = jnp.zeros((tm - tm // 2, out_ref.shape[1]), out_ref.dtype)

    @pl.when(fill == 0)
    def _():
        out_ref[...] = jnp.zeros_like(out_ref)


def _expert_up_kernel(te_ref, tr_ref, tv_ref, xs_ref, w1_ref, w3_ref, h_ref):
    del te_ref, tr_ref

    def compute(rows):
        x = xs_ref[rows, :].astype(w1_ref.dtype)
        a = jnp.dot(x, w1_ref[...], preferred_element_type=F32)
        b = jnp.dot(x, w3_ref[...], preferred_element_type=F32)
        h_ref[rows, :] = (a * jax.nn.sigmoid(a) * b).astype(h_ref.dtype)

    _for_used_rows(tv_ref, h_ref, compute)


def _expert_specs():
    rows = lambda c, i, te, tr, tv: (tr[i], 0)
    weights = lambda c, i, te, tr, tv: (te[i], 0, c)
    out = lambda c, i, te, tr, tv: (i, c)
    return rows, weights, out


def _expert_up(tiles, xs, w1, w3, tm):
    n_slots = xs.shape[0]
    _, k, n = w1.shape
    tn = _tile(n, 512)
    rows, weights, out = _expert_specs()
    weight_spec = pl.BlockSpec((None, k, tn), weights)
    grid_spec = pltpu.PrefetchScalarGridSpec(
        num_scalar_prefetch=3,
        grid=(n // tn, n_slots // tm),
        in_specs=[pl.BlockSpec((tm, k), rows), weight_spec, weight_spec],
        out_specs=pl.BlockSpec((tm, tn), out),
    )
    return pl.pallas_call(
        _expert_up_kernel,
        grid_spec=grid_spec,
        out_shape=jax.ShapeDtypeStruct((n_slots, n), BF16),
        compiler_params=_params(("parallel", "arbitrary"), 48),
        name="expert_up",
    )(*tiles, xs, w1, w3)


def _expert_down_kernel(te_ref, tr_ref, tv_ref, h_ref, w2_ref, y_ref):
    del te_ref, tr_ref

    def compute(rows):
        y_ref[rows, :] = jnp.dot(h_ref[rows, :], w2_ref[...], preferred_element_type=F32)

    _for_used_rows(tv_ref, y_ref, compute)


def _expert_down(tiles, h, w2, tm):
    n_slots, k = h.shape
    n = w2.shape[2]
    tn = _tile(n, 1024)
    rows, weights, out = _expert_specs()
    grid_spec = pltpu.PrefetchScalarGridSpec(
        num_scalar_prefetch=3,
        grid=(n // tn, n_slots // tm),
        in_specs=[pl.BlockSpec((tm, k), rows),
                  pl.BlockSpec((None, k, tn), weights, pipeline_mode=pl.Buffered(1))],
        out_specs=pl.BlockSpec((tm, tn), out),
    )
    return pl.pallas_call(
        _expert_down_kernel,
        grid_spec=grid_spec,
        out_shape=jax.ShapeDtypeStruct((n_slots, n), F32),
        compiler_params=_params(("parallel", "arbitrary"), 56),
        name="expert_down",
    )(*tiles, h, w2)


def _gather_mix_ln_kernel(pos_ref, next_pos_ref, ys_ref, gate_ref, x_ref, g_ref, b_ref, wg_ref,
                          bg_ref, p_ref, wp_ref, o32_ref, o16_ref, buf_ref, sem, *, alpha):
    i = pl.program_id(0)
    tm = x_ref.shape[0]
    slot = lax.rem(i, 2)

    def issue(idx_ref, s):
        def start(r, carry):
            for k in range(TOP_K):
                pltpu.make_async_copy(ys_ref.at[pl.ds(idx_ref[k, r], 1), :],
                                      buf_ref.at[s, k, pl.ds(r, 1), :], sem.at[s]).start()
            return carry
        lax.fori_loop(0, tm, start, 0, unroll=ROW_COPY_UNROLL)

    def wait(r, carry):
        for k in range(TOP_K):
            pltpu.make_async_copy(ys_ref.at[pl.ds(0, 1), :],
                                  buf_ref.at[slot, k, pl.ds(r, 1), :], sem.at[slot]).wait()
        return carry

    @pl.when(i == 0)
    def _():
        issue(pos_ref, slot)

    @pl.when(i + 1 < pl.num_programs(0))
    def _():
        issue(next_pos_ref, 1 - slot)

    lax.fori_loop(0, tm, wait, 0, unroll=ROW_COPY_UNROLL)
    gates = gate_ref[...]
    f = gates[:, 0:1] * buf_ref[slot, 0] + gates[:, 1:2] * buf_ref[slot, 1]
    y = _layer_norm(alpha * x_ref[...] + f, g_ref[...], b_ref[...])
    y = _gated_embedding(y, wg_ref, bg_ref, p_ref, wp_ref)
    o32_ref[...] = y
    o16_ref[...] = y.astype(o16_ref.dtype)


def _gather_mix_ln(pos, ys, gates, x_res, g, b, alpha, w_gate, b_gate, pb, w_proj):
    m, d = x_res.shape
    kp = pb.shape[1]
    tm = _tile(m, 512)
    last = m // tm - 1
    row = lambda i: (i, 0)
    fixed = lambda i: (0, 0)
    pos_tiles = _tile_positions(pos, tm)
    return pl.pallas_call(
        functools.partial(_gather_mix_ln_kernel, alpha=alpha),
        grid=(m // tm,),
        in_specs=[pl.BlockSpec((None, TOP_K, tm), lambda i: (i, 0, 0), memory_space=pltpu.SMEM),
                  pl.BlockSpec((None, TOP_K, tm), lambda i: (jnp.minimum(i + 1, last), 0, 0),
                               memory_space=pltpu.SMEM),
                  pl.BlockSpec(memory_space=pl.ANY),
                  pl.BlockSpec((tm, TOP_K), row),
                  pl.BlockSpec((tm, d), row),
                  pl.BlockSpec((1, d), fixed), pl.BlockSpec((1, d), fixed)]
        + _ple_specs(tm, d, kp, row, fixed),
        out_specs=[pl.BlockSpec((tm, d), row), pl.BlockSpec((tm, d), row)],
        out_shape=[jax.ShapeDtypeStruct((m, d), F32), jax.ShapeDtypeStruct((m, d), BF16)],
        scratch_shapes=[pltpu.VMEM((2, TOP_K, tm, d), F32), pltpu.SemaphoreType.DMA((2,))],
        compiler_params=_params(("arbitrary",), 56, disable_bounds_checks=True),
        name="gather_mix_ln",
    )(pos_tiles, pos_tiles, ys, gates, x_res, g.reshape(1, d), b.reshape(1, d),
      w_gate, b_gate.reshape(1, d), pb, w_proj)


def _moe_ffn_ln(x32, xb, w_router, w1, w3, w2, g, b, alpha, ple):
    m, _ = xb.shape
    n_exp = w1.shape[0]
    tm = _tile(m, 1024)
    experts, ranks, gates, counts = _router(xb, w_router.T.astype(BF16))

    counts = counts[:, 0]
    padded = (counts + tm - 1) // tm * tm
    ends = jnp.cumsum(padded)
    starts = ends - padded
    expert_ids = jnp.arange(n_exp, dtype=jnp.int32)[:, None, None]
    pos = jnp.sum(jnp.where(experts[None] == expert_ids, starts[:, None, None], 0), axis=0) + ranks
    assert (TOP_K * m) % tm == 0
    n_slots = TOP_K * m + n_exp * tm
    tile_index = jnp.arange(n_slots // tm, dtype=jnp.int32)
    tile_row = jnp.minimum(tile_index, jnp.maximum(ends[-1] // tm - 1, 0))
    tile_expert = jnp.sum((ends[None, :] <= (tile_row * tm)[:, None]).astype(jnp.int32), axis=1)
    tile_expert = jnp.minimum(tile_expert, n_exp - 1)
    group_real_end = jnp.sum(jnp.where(tile_expert[:, None] == jnp.arange(n_exp)[None, :],
                                       (starts + counts)[None, :], 0), axis=1)
    real_rows = jnp.clip(group_real_end - tile_index * tm, 0, tm)
    tile_fill = jnp.where(tile_index * tm >= ends[-1], 0, jnp.where(real_rows > tm // 2, 2, 1))
    tiles = (tile_expert, tile_row, tile_fill.astype(jnp.int32))

    fill_starts = (starts + counts) // SUBLANES * SUBLANES
    xs = _dispatch(jnp.concatenate([fill_starts, ends[-1:]]), pos, x32, n_slots, tm)
    h = _expert_up(tiles, xs, w1, w3, tm)
    ys = _expert_down(tiles, h, w2, tm)
    return _gather_mix_ln(pos, ys, gates.T, x32, g, b, alpha, *ple)


def kernel(x, p, attn_w_qkv, attn_w_o, pool_w_in, pool_w_group, pool_scale, pool_w_o,
           ln_mix_g, ln_mix_b, ln_ffn_g, ln_ffn_b, ffn_w1, ffn_w3, ffn_w2,
           moe_router, moe_w1, moe_w3, moe_w2, ple_w_proj, ple_w_gate, ple_b_gate):
    batch, seq, d = x.shape
    depth = p.shape[0]
    att_w = N_HEADS * LANES
    assert attn_w_qkv.shape[2] == N_GROUPS * 3 * att_w
    assert depth % 2 == 0
    alpha = (2 * depth) ** 0.25
    m = batch * seq
    bf = lambda t: t.astype(BF16)

    x32 = x.reshape(m, d)
    xb = None
    for i in range(depth):
        j = i // 2
        ple = (bf(ple_w_gate[i]), ple_b_gate[i], bf(p[i].reshape(m, -1)), bf(ple_w_proj[i]))
        if i % 2 == 0:
            outs, lses, expert_w = [], [], []
            x_groups = _cast_and_regroup(x32, batch, seq)
            expert_w32 = (moe_w1[j], moe_w3[j], moe_w2[j])
            assert len(expert_w32) == N_GROUPS
            for grp, (_, dil) in enumerate(DILATED_GROUPS):
                side = expert_w32[grp]
                qkv, side_b = _qkv_rope(x_groups[grp], attn_w_qkv[j], _rope_tables(seq, dil), grp, seq,
                                        att_w, side.reshape(-1, side.shape[-1]))
                expert_w.append(side_b.reshape(side.shape))
                o_g, lse_g = _dilated_attention(qkv, grp, seq, att_w)
                outs.append(o_g)
                lses.append(lse_g)
            x32, xb = _attn_out(outs, lses, bf(attn_w_o[j]), x32, ln_mix_g[i], ln_mix_b[i],
                                alpha, batch, seq)
            h = _swiglu_up(xb, ffn_w1[j], ffn_w3[j])
            x32, xb = _matmul_res_ln_ple(h, bf(ffn_w2[j]), x32, ln_ffn_g[i], ln_ffn_b[i], alpha, *ple)
        else:
            x32, xb = _pool_mixer_ln(
                xb, x32, bf(pool_w_in[j]), bf(pool_w_group[j]), pool_scale[j].reshape(-1),
                bf(pool_w_o[j]), ln_mix_g[i], ln_mix_b[i], alpha, batch, seq)
            x32, xb = _moe_ffn_ln(x32, xb, moe_router[j], *expert_w,
                                  ln_ffn_g[i], ln_ffn_b[i], alpha, ple)
    return x32.reshape(batch, seq, d)
```

```python
import functools
import math

import jax
import jax.numpy as jnp
from jax import lax
from jax.experimental import pallas as pl
from jax.experimental.pallas import tpu as pltpu

F32 = jnp.float32
BF16 = jnp.bfloat16

N_HEADS = 16
DILATED_GROUPS = ((128, 1), (512, 4), (2048, 16))
N_GROUPS = len(DILATED_GROUPS)
ATT_BLOCK = 128
ATT_BLOCKS_PER_STEP = 4
REGROUP_CHUNK = 256
SIDE_BLOCK_ELEMS = 1024 * 1024
ROPE_THETA = 10000.0
POOL_WINDOWS = (2, 4, 8, 16)
POOL_HALO = 16
TOP_K = 2
ROW_COPY_UNROLL = 4
LN_EPS = 1e-5
NEG_INF = -1e30

LANES = 128
SUBLANES = 8
MIB = 1024 * 1024


def _tile(dim, pref):
    t = min(dim, pref)
    while dim % t:
        t //= 2
    return t


def _params(semantics, vmem_mib, **kw):
    return pltpu.CompilerParams(dimension_semantics=semantics, vmem_limit_bytes=vmem_mib * MIB, **kw)


def _layer_norm(y, g, b):
    mu = jnp.mean(y, axis=-1, keepdims=True)
    yc = y - mu
    var = jnp.mean(yc * yc, axis=-1, keepdims=True)
    return yc * lax.rsqrt(var + LN_EPS) * g + b


def _regroup_kernel(x_ref, perm_ref, xb_ref, *group_refs, dils, chunk):
    xb = x_ref[...].astype(xb_ref.dtype)
    xb_ref[...] = xb
    for which, (dil, o_ref) in enumerate(zip(dils, group_refs)):
        slab = chunk // dil
        for ch in range(x_ref.shape[0] // chunk):
            grouped = jnp.dot(perm_ref[which], xb[ch * chunk:(ch + 1) * chunk],
                              preferred_element_type=F32).astype(o_ref.dtype)
            for r in range(dil):
                o_ref[r, ch * slab:(ch + 1) * slab, :] = grouped[r * slab:(r + 1) * slab]


def _cast_and_regroup(x32, batch, seq):
    m, d = x32.shape
    dils = tuple(dil for _, dil in DILATED_GROUPS if dil > 1)
    chunk = REGROUP_CHUNK
    tm = _tile(seq, 512)
    assert tm % chunk == 0 and all(chunk % (dil * 2 * SUBLANES) == 0 for dil in dils)
    src = jnp.arange(chunk)
    perms = jnp.stack([((src[:, None] % (chunk // dil)) * dil + src[:, None] // (chunk // dil)
                        == src[None, :]).astype(BF16) for dil in dils])
    s_tiles = seq // tm
    outs = pl.pallas_call(
        functools.partial(_regroup_kernel, dils=dils, chunk=chunk),
        grid=(batch, s_tiles),
        in_specs=[pl.BlockSpec((tm, d), lambda b, s: (b * s_tiles + s, 0)),
                  pl.BlockSpec(perms.shape, lambda b, s: (0, 0, 0))],
        out_specs=[pl.BlockSpec((tm, d), lambda b, s: (b * s_tiles + s, 0))]
        + [pl.BlockSpec((None, dil, tm // dil, d), lambda b, s: (b, 0, s, 0)) for dil in dils],
        out_shape=[jax.ShapeDtypeStruct((m, d), BF16)]
        + [jax.ShapeDtypeStruct((batch, dil, seq // dil, d), BF16) for dil in dils],
        compiler_params=_params(("parallel", "parallel"), 48),
        name="cast_and_regroup",
    )(x32, perms)
    by_dil = {1: outs[0], **{dil: o.reshape(m, d) for dil, o in zip(dils, outs[1:])}}
    return [by_dil[dil] for _, dil in DILATED_GROUPS]


def _qkv_rope_kernel(x_ref, w_ref, cos_ref, sin_ref, side_ref, o_ref, side_out_ref, wb_ref):
    @pl.when(pl.program_id(1) == 0)
    def _():
        wb_ref[...] = w_ref[...].astype(wb_ref.dtype)

    side_out_ref[...] = side_ref[...].astype(side_out_ref.dtype)

    acc = jnp.dot(x_ref[...], wb_ref[...], preferred_element_type=F32)
    cos = cos_ref[...]
    sin = sin_ref[...]
    for h in range(acc.shape[1] // LANES):
        t = acc[:, h * LANES:(h + 1) * LANES]
        rot = pltpu.roll(t, LANES // 2, 1)
        o_ref[:, h * LANES:(h + 1) * LANES] = (t * cos + rot * sin).astype(o_ref.dtype)


def _qkv_rope(x_g, w, tables, group, seq, att_w, side):
    m, k = x_g.shape
    cos_tab, sin_tab = tables
    tm = _tile(seq, 1024)
    tn = _tile(att_w, 1024)
    s_tiles = seq // tm
    per_role = att_w // tn
    per_group = 3 * per_role
    row_tiles = m // tm
    table_spec = pl.BlockSpec((None, tm, LANES), lambda j, i: (j // per_role, i % s_tiles, 0))
    side_cols = _tile(side.shape[1], 2048)
    side_rows = SIDE_BLOCK_ELEMS // side_cols
    assert side.shape[0] % side_rows == 0
    col_blocks = side.shape[1] // side_cols
    side_blocks = (side.shape[0] // side_rows) * col_blocks
    assert side_blocks <= per_group * row_tiles, "not enough grid steps to convert the side matrix"

    def side_index(j, i):
        block = jnp.minimum(j * row_tiles + i, side_blocks - 1)
        return (block // col_blocks, block % col_blocks)

    side_spec = pl.BlockSpec((side_rows, side_cols), side_index)
    return pl.pallas_call(
        _qkv_rope_kernel,
        grid=(per_group, row_tiles),
        in_specs=[pl.BlockSpec((tm, k), lambda j, i: (i, 0)),
                  pl.BlockSpec((k, tn), lambda j, i: (0, group * per_group + j)),
                  table_spec, table_spec, side_spec],
        out_specs=[pl.BlockSpec((tm, tn), lambda j, i: (i, j)), side_spec],
        out_shape=[jax.ShapeDtypeStruct((m, 3 * att_w), BF16),
                   jax.ShapeDtypeStruct(side.shape, BF16)],
        scratch_shapes=[pltpu.VMEM((k, tn), BF16)],
        compiler_params=_params(("arbitrary", "arbitrary"), 60),
        name=f"qkv_rope_g{group}",
    )(x_g, w, cos_tab, sin_tab, side)


def _rope_tables(seq, dil):
    half = LANES // 2
    inv = ROPE_THETA ** (-jnp.arange(half, dtype=F32) / half)
    row = jnp.arange(seq)
    n_sub = seq // dil
    pos = (row % n_sub) * dil + row // n_sub
    ang = pos.astype(F32)[:, None] * inv[None, :]
    cos, sin = jnp.cos(ang), jnp.sin(ang)
    cos_full = jnp.concatenate([cos, cos], axis=-1)
    sin_signed = jnp.concatenate([-sin, sin], axis=-1)
    return (jnp.stack([cos_full, cos_full, jnp.ones_like(cos_full)]),
            jnp.stack([sin_signed, sin_signed, jnp.zeros_like(cos_full)]))


def _attn_kernel(q_ref, kc_ref, kp_ref, vc_ref, vp_ref, o_ref, lse_ref, *, nb, n_back, scale):
    blk = kp_ref.shape[0]
    sub = q_ref.shape[0] // blk
    first_has_prev = lax.rem(pl.program_id(0) * sub, nb) > 0
    qi = lax.broadcasted_iota(jnp.int32, (blk, 2 * blk), 0) + blk
    ki = lax.broadcasted_iota(jnp.int32, (blk, 2 * blk), 1)
    dist = qi - ki
    band = (dist >= 0) & (dist <= n_back)
    bias_inner = jnp.where(band, 0.0, NEG_INF)
    bias_first = jnp.where(band & (first_has_prev | (ki >= blk)), 0.0, NEG_INF)
    lane = lax.broadcasted_iota(jnp.int32, (blk, LANES), 1)
    for t in range(sub):
        rows = slice(t * blk, (t + 1) * blk)
        before = slice((t - 1) * blk, t * blk)
        bias = bias_first if t == 0 else bias_inner
        lse_tile = jnp.zeros((blk, LANES), F32)
        for h in range(q_ref.shape[1] // LANES):
            hs = slice(h * LANES, (h + 1) * LANES)
            k_prev = kp_ref[:, hs] if t == 0 else kc_ref[before, hs]
            v_prev = vp_ref[:, hs] if t == 0 else vc_ref[before, hs]
            k2 = jnp.concatenate([k_prev, kc_ref[rows, hs]], axis=0)
            v2 = jnp.concatenate([v_prev, vc_ref[rows, hs]], axis=0)
            s = lax.dot_general(q_ref[rows, hs], k2, (((1,), (1,)), ((), ())),
                                preferred_element_type=F32) * scale + bias
            m = jnp.max(s, axis=-1, keepdims=True)
            e = jnp.exp(s - m)
            l = jnp.sum(e, axis=-1, keepdims=True)
            p = (e * (1.0 / l)).astype(v2.dtype)
            o_ref[rows, hs] = jnp.dot(p, v2, preferred_element_type=F32)
            lse_tile = jnp.where(lane == h, m + jnp.log(l), lse_tile)
        lse_ref[rows, :] = lse_tile


def _dilated_attention(qkv, group, seq, att_w):
    m = qkv.shape[0]
    window, dil = DILATED_GROUPS[group]
    nb = seq // (dil * ATT_BLOCK)
    sub = math.gcd(nb, ATT_BLOCKS_PER_STEP)
    assert seq % (dil * ATT_BLOCK) == 0 and window // dil <= ATT_BLOCK
    step_rows = sub * ATT_BLOCK

    def spec(role, prev):
        if prev:
            return pl.BlockSpec((ATT_BLOCK, att_w), lambda r: (jnp.maximum(sub * r - 1, 0), role))
        return pl.BlockSpec((step_rows, att_w), lambda r: (r, role))

    return pl.pallas_call(
        functools.partial(_attn_kernel, nb=nb, n_back=window // dil, scale=LANES ** -0.5),
        grid=(m // step_rows,),
        in_specs=[spec(0, False), spec(1, False), spec(1, True), spec(2, False), spec(2, True)],
        out_specs=[pl.BlockSpec((step_rows, att_w), lambda r: (r, 0)),
                   pl.BlockSpec((step_rows, LANES), lambda r: (r, 0))],
        out_shape=[jax.ShapeDtypeStruct((m, att_w), F32), jax.ShapeDtypeStruct((m, LANES), F32)],
        compiler_params=_params(("arbitrary",), 32),
        name=f"dilated_attn_g{group}",
    )(qkv, qkv, qkv, qkv, qkv)


def _token_order(ref, scr_ref):
    dil, n, w = ref.shape
    heads = range(w // LANES)
    if dil == 1:
        return [ref[0, :, h * LANES:(h + 1) * LANES] for h in heads]
    for r in range(dil):
        for h in heads:
            scr_ref[h, pl.ds(r, n, stride=dil), :] = ref[r, :, h * LANES:(h + 1) * LANES]
    return [scr_ref[h] for h in heads]


def _attn_out_kernel(*refs, alpha):
    o_refs, l_refs = refs[:N_GROUPS], refs[N_GROUPS:2 * N_GROUPS]
    w_ref, x_ref, g_ref, b_ref, o32_ref, o16_ref = refs[2 * N_GROUPS:2 * N_GROUPS + 6]
    scratch = refs[2 * N_GROUPS + 6:]
    o_scr, l_scr = scratch[:N_GROUPS], scratch[N_GROUPS:]
    outs = [_token_order(o_refs[g], o_scr[g]) for g in range(N_GROUPS)]
    lses = [_token_order(l_refs[g], l_scr[g])[0] for g in range(N_GROUPS)]
    m = functools.reduce(jnp.maximum, lses)
    es = [jnp.exp(l - m) for l in lses]
    inv = 1.0 / functools.reduce(jnp.add, es)
    wts = [e * inv for e in es]
    tm = x_ref.shape[0]
    heads = []
    for h in range(len(outs[0])):
        acc = None
        for g in range(N_GROUPS):
            term = jnp.broadcast_to(wts[g][:, h:h + 1], (tm, LANES)) * outs[g][h]
            acc = term if acc is None else acc + term
        heads.append(acc.astype(w_ref.dtype))
    mixed = jnp.concatenate(heads, axis=1)
    hid = jnp.dot(mixed, w_ref[...], preferred_element_type=F32)
    y = _layer_norm(alpha * x_ref[...] + hid, g_ref[...], b_ref[...])
    o32_ref[...] = y
    o16_ref[...] = y.astype(o16_ref.dtype)


def _attn_out(outs, lses, w_o, x_res, g, b, alpha, batch, seq):
    n_groups = len(outs)
    m, att_w = outs[0].shape
    d = w_o.shape[1]
    dils = [dil for _, dil in DILATED_GROUPS]
    tm = _tile(seq, 256)
    s_tiles = seq // tm
    assert all(tm % (8 * dil) == 0 for dil in dils)

    def group_view(t, g):
        return t.reshape(batch, dils[g], seq // dils[g], t.shape[-1])

    def group_spec(g, width):
        return pl.BlockSpec((None, dils[g], tm // dils[g], width),
                            lambda i: (i // s_tiles, 0, i % s_tiles, 0))

    row = lambda i: (i, 0)
    fixed = lambda i: (0, 0)
    return pl.pallas_call(
        functools.partial(_attn_out_kernel, alpha=alpha),
        grid=(m // tm,),
        in_specs=[group_spec(g, att_w) for g in range(n_groups)]
        + [group_spec(g, LANES) for g in range(n_groups)]
        + [pl.BlockSpec((att_w, d), fixed, pipeline_mode=pl.Buffered(1)),
           pl.BlockSpec((tm, d), row), pl.BlockSpec((1, d), fixed), pl.BlockSpec((1, d), fixed)],
        out_specs=[pl.BlockSpec((tm, d), row), pl.BlockSpec((tm, d), row)],
        out_shape=[jax.ShapeDtypeStruct((m, d), F32), jax.ShapeDtypeStruct((m, d), BF16)],
        scratch_shapes=[pltpu.VMEM((att_w // LANES, tm, LANES), F32) for _ in range(n_groups)]
        + [pltpu.VMEM((1, tm, LANES), F32) for _ in range(n_groups)],
        compiler_params=_params(("parallel",), 56),
        name="attn_out_ln",
    )(*[group_view(outs[g], g) for g in range(n_groups)],
      *[group_view(lses[g], g) for g in range(n_groups)],
      w_o, x_res, g.reshape(1, d), b.reshape(1, d))


def _gated_embedding(y, wg_ref, bg_ref, p_ref, wp_ref):
    z = jnp.dot(y.astype(wg_ref.dtype), wg_ref[...], preferred_element_type=F32) + bg_ref[...]
    e = jnp.dot(p_ref[...], wp_ref[...], preferred_element_type=F32)
    return y + jax.nn.sigmoid(z) * e


def _ple_specs(tm, d, kp, row, fixed):
    return [pl.BlockSpec((d, d), fixed, pipeline_mode=pl.Buffered(1)),
            pl.BlockSpec((1, d), fixed),
            pl.BlockSpec((tm, kp), row),
            pl.BlockSpec((kp, d), fixed, pipeline_mode=pl.Buffered(1))]


def _matmul_res_ln_ple_kernel(a_ref, w_ref, x_ref, g_ref, b_ref, wg_ref, bg_ref, p_ref, wp_ref,
                              o32_ref, o16_ref, *, alpha):
    h = jnp.dot(a_ref[...], w_ref[...], preferred_element_type=F32)
    y = _layer_norm(alpha * x_ref[...] + h, g_ref[...], b_ref[...])
    y = _gated_embedding(y, wg_ref, bg_ref, p_ref, wp_ref)
    o32_ref[...] = y
    o16_ref[...] = y.astype(o16_ref.dtype)


def _matmul_res_ln_ple(a, w, x_res, g, b, alpha, w_gate, b_gate, pb, w_proj):
    m, k = a.shape
    n = w.shape[1]
    kp = pb.shape[1]
    tm = _tile(m, 256)
    row = lambda i: (i, 0)
    fixed = lambda i: (0, 0)
    return pl.pallas_call(
        functools.partial(_matmul_res_ln_ple_kernel, alpha=alpha),
        grid=(m // tm,),
        in_specs=[pl.BlockSpec((tm, k), row),
                  pl.BlockSpec((k, n), fixed, pipeline_mode=pl.Buffered(1)),
                  pl.BlockSpec((tm, n), row), pl.BlockSpec((1, n), fixed), pl.BlockSpec((1, n), fixed)]
        + _ple_specs(tm, n, kp, row, fixed),
        out_specs=[pl.BlockSpec((tm, n), row), pl.BlockSpec((tm, n), row)],
        out_shape=[jax.ShapeDtypeStruct((m, n), F32), jax.ShapeDtypeStruct((m, n), BF16)],
        compiler_params=_params(("parallel",), 60),
        name="matmul_res_ln_ple",
    )(a, w, x_res, g.reshape(1, n), b.reshape(1, n), w_gate, b_gate.reshape(1, n), pb, w_proj)


def _swiglu_up_kernel(x_ref, w1_ref, w3_ref, o_ref, w1b_ref, w3b_ref):
    @pl.when(pl.program_id(1) == 0)
    def _():
        w1b_ref[...] = w1_ref[...].astype(w1b_ref.dtype)
        w3b_ref[...] = w3_ref[...].astype(w3b_ref.dtype)

    x = x_ref[...]
    a = jnp.dot(x, w1b_ref[...], preferred_element_type=F32)
    b = jnp.dot(x, w3b_ref[...], preferred_element_type=F32)
    o_ref[...] = (a * jax.nn.sigmoid(a) * b).astype(o_ref.dtype)


def _swiglu_up(xb, w1, w3):
    m, k = xb.shape
    n = w1.shape[1]
    tm, tn = _tile(m, 1024), _tile(n, 512)
    return pl.pallas_call(
        _swiglu_up_kernel,
        grid=(n // tn, m // tm),
        in_specs=[pl.BlockSpec((tm, k), lambda j, i: (i, 0)),
                  pl.BlockSpec((k, tn), lambda j, i: (0, j)),
                  pl.BlockSpec((k, tn), lambda j, i: (0, j))],
        out_specs=pl.BlockSpec((tm, tn), lambda j, i: (i, j)),
        out_shape=jax.ShapeDtypeStruct((m, n), BF16),
        scratch_shapes=[pltpu.VMEM((k, tn), BF16), pltpu.VMEM((k, tn), BF16)],
        compiler_params=_params(("parallel", "arbitrary"), 56),
        name="swiglu_up",
    )(xb, w1, w3)


def _pool_mixer_kernel(xb_ref, win_ref, wg_ref, sc_ref, wo_ref, x_ref, g_ref, b_ref,
                       o32_ref, o16_ref, buf_ref, *, alpha):
    s = pl.program_id(1)
    tm = xb_ref.shape[0]
    pd = wg_ref.shape[1]

    @pl.when(s == 0)
    def _():
        buf_ref[0:POOL_HALO, :] = jnp.zeros((POOL_HALO, buf_ref.shape[1]), F32)

    @pl.when(s > 0)
    def _():
        buf_ref[0:POOL_HALO, :] = buf_ref[tm:tm + POOL_HALO, :]

    buf_ref[POOL_HALO:POOL_HALO + tm, :] = jnp.dot(xb_ref[...], win_ref[...], preferred_element_type=F32)
    t = s * tm + lax.broadcasted_iota(jnp.int32, (tm, 1), 0)
    pieces = []
    for g, w in enumerate(POOL_WINDOWS):
        cols = slice(g * pd, (g + 1) * pd)
        u = buf_ref[POOL_HALO:POOL_HALO + tm, cols]
        tot = u
        for i in range(1, w):
            tot = tot + buf_ref[POOL_HALO - i:POOL_HALO - i + tm, cols]
        cnt = jnp.minimum(t + 1, w).astype(F32)
        mixed = (tot / cnt - u).astype(wg_ref.dtype)
        y = jnp.dot(mixed, wg_ref[g], preferred_element_type=F32) * sc_ref[:, cols]
        pieces.append(y.astype(wo_ref.dtype))
    hid = jnp.dot(jnp.concatenate(pieces, axis=1), wo_ref[...], preferred_element_type=F32)
    y = _layer_norm(alpha * x_ref[...] + hid, g_ref[...], b_ref[...])
    o32_ref[...] = y
    o16_ref[...] = y.astype(o16_ref.dtype)


def _pool_mixer_ln(xb, x_res, w_in, w_group, scale, w_o, g, b, alpha, batch, seq):
    m, d = x_res.shape
    tm = _tile(seq, 256)
    s_tiles = seq // tm
    assert tm % SUBLANES == 0 and max(POOL_WINDOWS) <= POOL_HALO <= tm
    row = lambda bi, s: (bi * s_tiles + s, 0)
    fixed = lambda bi, s: (0, 0)
    resident = lambda shape, index: pl.BlockSpec(shape, index, pipeline_mode=pl.Buffered(1))
    return pl.pallas_call(
        functools.partial(_pool_mixer_kernel, alpha=alpha),
        grid=(batch, s_tiles),
        in_specs=[pl.BlockSpec((tm, d), row),
                  resident((d, d), fixed),
                  resident(w_group.shape, lambda bi, s: (0, 0, 0)),
                  pl.BlockSpec((1, d), fixed),
                  resident((d, d), fixed),
                  pl.BlockSpec((tm, d), row), pl.BlockSpec((1, d), fixed), pl.BlockSpec((1, d), fixed)],
        out_specs=[pl.BlockSpec((tm, d), row), pl.BlockSpec((tm, d), row)],
        out_shape=[jax.ShapeDtypeStruct((m, d), F32), jax.ShapeDtypeStruct((m, d), BF16)],
        scratch_shapes=[pltpu.VMEM((POOL_HALO + tm, d), F32)],
        compiler_params=_params(("parallel", "arbitrary"), 56),
        name="pool_mixer_ln",
    )(xb, w_in, w_group, scale.reshape(1, d), w_o, x_res, g.reshape(1, d), b.reshape(1, d))


def _router_kernel(x_ref, wr_ref, tri_ref, e_ref, rank_ref, gate_ref, cnt_ref, run_ref):
    @pl.when(pl.program_id(0) == 0)
    def _():
        run_ref[...] = jnp.zeros_like(run_ref)

    logits = lax.dot_general(wr_ref[...], x_ref[...], (((1,), (1,)), ((), ())),
                             preferred_element_type=F32)
    n_exp = logits.shape[0]
    eidx = lax.broadcasted_iota(jnp.int32, logits.shape, 0)
    m1 = jnp.max(logits, axis=0, keepdims=True)
    i1 = jnp.min(jnp.where(logits == m1, eidx, n_exp), axis=0, keepdims=True)
    rest = jnp.where(eidx == i1, -jnp.inf, logits)
    m2 = jnp.max(rest, axis=0, keepdims=True)
    i2 = jnp.min(jnp.where(rest == m2, eidx, n_exp), axis=0, keepdims=True)
    ex = jnp.exp(m2 - m1)
    g1 = 1.0 / (1.0 + ex)
    g2 = ex / (1.0 + ex)

    oh1 = (eidx == i1).astype(F32)
    oh2 = (eidx == i2).astype(F32)
    tri = tri_ref[...]
    c1 = jnp.dot(oh1.astype(tri.dtype), tri, preferred_element_type=F32)
    c2 = jnp.dot(oh2.astype(tri.dtype), tri, preferred_element_type=F32)
    tot1 = jnp.sum(oh1, axis=1, keepdims=True)
    tot2 = jnp.sum(oh2, axis=1, keepdims=True)
    run = run_ref[:, 0:1]
    r1 = jnp.sum(oh1 * (run + c1), axis=0, keepdims=True)
    r2 = jnp.sum(oh2 * (run + tot1 + c2), axis=0, keepdims=True)
    run = run + tot1 + tot2
    run_ref[...] = jnp.broadcast_to(run, run_ref.shape)

    e_ref[...] = jnp.concatenate([i1, i2], axis=0)
    rank_ref[...] = jnp.concatenate([r1, r2], axis=0).astype(jnp.int32)
    gate_ref[...] = jnp.concatenate([g1, g2], axis=0)
    cnt_ref[...] = jnp.broadcast_to(run, cnt_ref.shape).astype(jnp.int32)


def _router(xb, w_router_t):
    m, k = xb.shape
    n_exp = w_router_t.shape[0]
    tm = _tile(m, 512)
    tri = (jnp.arange(tm)[:, None] < jnp.arange(tm)[None, :]).astype(BF16)
    pair = lambda dt: jax.ShapeDtypeStruct((TOP_K, m), dt)
    pair_spec = pl.BlockSpec((TOP_K, tm), lambda i: (0, i))
    return pl.pallas_call(
        _router_kernel,
        grid=(m // tm,),
        in_specs=[pl.BlockSpec((tm, k), lambda i: (i, 0)),
                  pl.BlockSpec((n_exp, k), lambda i: (0, 0)),
                  pl.BlockSpec((tm, tm), lambda i: (0, 0))],
        out_specs=[pair_spec, pair_spec, pair_spec, pl.BlockSpec((n_exp, LANES), lambda i: (0, 0))],
        out_shape=[pair(jnp.int32), pair(jnp.int32), pair(F32),
                   jax.ShapeDtypeStruct((n_exp, LANES), jnp.int32)],
        scratch_shapes=[pltpu.VMEM((n_exp, LANES), F32)],
        compiler_params=_params(("arbitrary",), 32),
        name="router",
    )(xb, w_router_t, tri)


def _tile_positions(pos, tm):
    return pos.reshape(TOP_K, -1, tm).transpose(1, 0, 2)


def _dispatch_kernel(fill_ref, pos_ref, x_ref, xs_ref, stage_ref, zero_ref, sem, fill_sem):
    i = pl.program_id(0)
    tm = x_ref.shape[0]
    slot = lax.rem(i, 2)

    @pl.when(i == 0)
    def _():
        zero_ref[...] = jnp.zeros_like(zero_ref)
        fill_rows = zero_ref.shape[0]
        n_groups = fill_ref.shape[0] - 1

        def fill(start):
            return pltpu.make_async_copy(
                zero_ref, xs_ref.at[pl.ds(pl.multiple_of(start, SUBLANES), fill_rows), :], fill_sem)

        fills = [fill(fill_ref[e]) for e in range(n_groups)]
        for c in fills:
            c.start()
        for c in fills:
            c.wait()

        def fill_unused(t, carry):
            c = fill(t * fill_rows)
            c.start()
            c.wait()
            return carry
        lax.fori_loop(fill_ref[n_groups] // fill_rows, xs_ref.shape[0] // fill_rows, fill_unused, 0)

    def row_copy(s, r, k):
        return pltpu.make_async_copy(stage_ref.at[s, pl.ds(r, 1), :],
                                     xs_ref.at[pl.ds(pos_ref[k, r], 1), :], sem.at[s])

    def start(r, carry):
        for k in range(TOP_K):
            row_copy(slot, r, k).start()
        return carry

    def drain(s):
        def wait(r, carry):
            for k in range(TOP_K):
                pltpu.make_async_copy(stage_ref.at[s, pl.ds(r, 1), :],
                                      xs_ref.at[pl.ds(0, 1), :], sem.at[s]).wait()
            return carry
        lax.fori_loop(0, tm, wait, 0, unroll=ROW_COPY_UNROLL)

    stage_ref[slot] = x_ref[...]
    lax.fori_loop(0, tm, start, 0, unroll=ROW_COPY_UNROLL)

    @pl.when(i > 0)
    def _():
        drain(1 - slot)

    @pl.when(i == pl.num_programs(0) - 1)
    def _():
        drain(slot)


def _dispatch(fill_starts, pos, xp, n_slots, fill_rows):
    m, w = xp.shape
    tm = _tile(m, 512)
    grid_spec = pltpu.PrefetchScalarGridSpec(
        num_scalar_prefetch=1,
        grid=(m // tm,),
        in_specs=[pl.BlockSpec((None, TOP_K, tm), lambda i, fill: (i, 0, 0), memory_space=pltpu.SMEM),
                  pl.BlockSpec((tm, w), lambda i, fill: (i, 0))],
        out_specs=pl.BlockSpec(memory_space=pl.ANY),
        scratch_shapes=[pltpu.VMEM((2, tm, w), xp.dtype), pltpu.VMEM((fill_rows, w), xp.dtype),
                        pltpu.SemaphoreType.DMA((2,)), pltpu.SemaphoreType.DMA(())],
    )
    return pl.pallas_call(
        _dispatch_kernel,
        grid_spec=grid_spec,
        out_shape=jax.ShapeDtypeStruct((n_slots, w), xp.dtype),
        compiler_params=_params(("arbitrary",), 48, disable_bounds_checks=True),
        name="dispatch",
    )(fill_starts, _tile_positions(pos, tm), xp)


def _for_used_rows(fill_ref, out_ref, compute, tile_axis=1):
    tm = out_ref.shape[0]
    fill = fill_ref[pl.program_id(tile_axis)]

    @pl.when(fill == 2)
    def _():
        compute(slice(0, tm))

    @pl.when(fill == 1)
    def _():
        compute(slice(0, tm // 2))
        out_ref[tm // 2:, :] = jnp.zeros((tm - tm // 2, out_ref.shape[1]), out_ref.dtype)

    @pl.when(fill == 0)
    def _():
        out_ref[...] = jnp.zeros_like(out_ref)


def _expert_up_kernel(te_ref, tr_ref, tv_ref, xs_ref, w1_ref, w3_ref, h_ref, xb_ref):
    del te_ref, tr_ref

    @pl.when(pl.program_id(1) == 0)
    def _():
        xb_ref[...] = xs_ref[...].astype(xb_ref.dtype)

    def compute(rows):
        x = xb_ref[rows, :]
        a = jnp.dot(x, w1_ref[...], preferred_element_type=F32)
        b = jnp.dot(x, w3_ref[...], preferred_element_type=F32)
        h_ref[rows, :] = (a * jax.nn.sigmoid(a) * b).astype(h_ref.dtype)

    _for_used_rows(tv_ref, h_ref, compute, tile_axis=0)


def _expert_specs():
    rows = lambda c, i, te, tr, tv: (tr[i], 0)
    weights = lambda c, i, te, tr, tv: (te[i], 0, c)
    out = lambda c, i, te, tr, tv: (i, c)
    return rows, weights, out


def _expert_up(tiles, xs, w1, w3, tm):
    n_slots = xs.shape[0]
    _, k, n = w1.shape
    tn = _tile(n, 512)
    last_chunk = n // tn - 1
    rows = lambda i, c, te, tr, tv: (tr[i], 0)
    weights = lambda i, c, te, tr, tv: (te[i], 0, jnp.where(tv[i] > 0, c, last_chunk))
    weight_spec = pl.BlockSpec((None, k, tn), weights)
    grid_spec = pltpu.PrefetchScalarGridSpec(
        num_scalar_prefetch=3,
        grid=(n_slots // tm, n // tn),
        in_specs=[pl.BlockSpec((tm, k), rows), weight_spec, weight_spec],
        out_specs=pl.BlockSpec((tm, tn), lambda i, c, te, tr, tv: (i, c)),
        scratch_shapes=[pltpu.VMEM((tm, k), BF16)],
    )
    return pl.pallas_call(
        _expert_up_kernel,
        grid_spec=grid_spec,
        out_shape=jax.ShapeDtypeStruct((n_slots, n), BF16),
        compiler_params=_params(("arbitrary", "arbitrary"), 48),
        name="expert_up",
    )(*tiles, xs, w1, w3)


def _expert_down_kernel(te_ref, tr_ref, tv_ref, h_ref, w2_ref, y_ref):
    del te_ref, tr_ref

    def compute(rows):
        y_ref[rows, :] = jnp.dot(h_ref[rows, :], w2_ref[...], preferred_element_type=F32)

    _for_used_rows(tv_ref, y_ref, compute)


def _expert_down(tiles, h, w2, tm):
    n_slots, k = h.shape
    n = w2.shape[2]
    tn = _tile(n, 1024)
    rows, weights, out = _expert_specs()
    grid_spec = pltpu.PrefetchScalarGridSpec(
        num_scalar_prefetch=3,
        grid=(n // tn, n_slots // tm),
        in_specs=[pl.BlockSpec((tm, k), rows),
                  pl.BlockSpec((None, k, tn), weights, pipeline_mode=pl.Buffered(1))],
        out_specs=pl.BlockSpec((tm, tn), out),
    )
    return pl.pallas_call(
        _expert_down_kernel,
        grid_spec=grid_spec,
        out_shape=jax.ShapeDtypeStruct((n_slots, n), F32),
        compiler_params=_params(("parallel", "arbitrary"), 56),
        name="expert_down",
    )(*tiles, h, w2)


def _gather_mix_ln_kernel(pos_ref, next_pos_ref, ys_ref, gate_ref, x_ref, g_ref, b_ref, wg_ref,
                          bg_ref, p_ref, wp_ref, o32_ref, o16_ref, buf_ref, sem, *, alpha):
    i = pl.program_id(0)
    tm = x_ref.shape[0]
    slot = lax.rem(i, 2)

    def issue(idx_ref, s):
        def start(r, carry):
            for k in range(TOP_K):
                pltpu.make_async_copy(ys_ref.at[pl.ds(idx_ref[k, r], 1), :],
                                      buf_ref.at[s, k, pl.ds(r, 1), :], sem.at[s]).start()
            return carry
        lax.fori_loop(0, tm, start, 0, unroll=ROW_COPY_UNROLL)

    def wait(r, carry):
        for k in range(TOP_K):
            pltpu.make_async_copy(ys_ref.at[pl.ds(0, 1), :],
                                  buf_ref.at[slot, k, pl.ds(r, 1), :], sem.at[slot]).wait()
        return carry

    @pl.when(i == 0)
    def _():
        issue(pos_ref, slot)

    @pl.when(i + 1 < pl.num_programs(0))
    def _():
        issue(next_pos_ref, 1 - slot)

    lax.fori_loop(0, tm, wait, 0, unroll=ROW_COPY_UNROLL)
    gates = gate_ref[...]
    f = gates[:, 0:1] * buf_ref[slot, 0] + gates[:, 1:2] * buf_ref[slot, 1]
    y = _layer_norm(alpha * x_ref[...] + f, g_ref[...], b_ref[...])
    y = _gated_embedding(y, wg_ref, bg_ref, p_ref, wp_ref)
    o32_ref[...] = y
    o16_ref[...] = y.astype(o16_ref.dtype)


def _gather_mix_ln(pos, ys, gates, x_res, g, b, alpha, w_gate, b_gate, pb, w_proj):
    m, d = x_res.shape
    kp = pb.shape[1]
    tm = _tile(m, 512)
    last = m // tm - 1
    row = lambda i: (i, 0)
    fixed = lambda i: (0, 0)
    pos_tiles = _tile_positions(pos, tm)
    return pl.pallas_call(
        functools.partial(_gather_mix_ln_kernel, alpha=alpha),
        grid=(m // tm,),
        in_specs=[pl.BlockSpec((None, TOP_K, tm), lambda i: (i, 0, 0), memory_space=pltpu.SMEM),
                  pl.BlockSpec((None, TOP_K, tm), lambda i: (jnp.minimum(i + 1, last), 0, 0),
                               memory_space=pltpu.SMEM),
                  pl.BlockSpec(memory_space=pl.ANY),
                  pl.BlockSpec((tm, TOP_K), row),
                  pl.BlockSpec((tm, d), row),
                  pl.BlockSpec((1, d), fixed), pl.BlockSpec((1, d), fixed)]
        + _ple_specs(tm, d, kp, row, fixed),
        out_specs=[pl.BlockSpec((tm, d), row), pl.BlockSpec((tm, d), row)],
        out_shape=[jax.ShapeDtypeStruct((m, d), F32), jax.ShapeDtypeStruct((m, d), BF16)],
        scratch_shapes=[pltpu.VMEM((2, TOP_K, tm, d), F32), pltpu.SemaphoreType.DMA((2,))],
        compiler_params=_params(("arbitrary",), 56, disable_bounds_checks=True),
        name="gather_mix_ln",
    )(pos_tiles, pos_tiles, ys, gates, x_res, g.reshape(1, d), b.reshape(1, d),
      w_gate, b_gate.reshape(1, d), pb, w_proj)


def _moe_ffn_ln(x32, xb, w_router, w1, w3, w2, g, b, alpha, ple):
    m, _ = xb.shape
    n_exp = w1.shape[0]
    tm = _tile(m, 1024)
    experts, ranks, gates, counts = _router(xb, w_router.T.astype(BF16))

    counts = counts[:, 0]
    padded = (counts + tm - 1) // tm * tm
    ends = jnp.cumsum(padded)
    starts = ends - padded
    expert_ids = jnp.arange(n_exp, dtype=jnp.int32)[:, None, None]
    pos = jnp.sum(jnp.where(experts[None] == expert_ids, starts[:, None, None], 0), axis=0) + ranks
    assert (TOP_K * m) % tm == 0
    n_slots = TOP_K * m + n_exp * tm
    tile_index = jnp.arange(n_slots // tm, dtype=jnp.int32)
    tile_row = jnp.minimum(tile_index, jnp.maximum(ends[-1] // tm - 1, 0))
    tile_expert = jnp.sum((ends[None, :] <= (tile_row * tm)[:, None]).astype(jnp.int32), axis=1)
    tile_expert = jnp.minimum(tile_expert, n_exp - 1)
    group_real_end = jnp.sum(jnp.where(tile_expert[:, None] == jnp.arange(n_exp)[None, :],
                                       (starts + counts)[None, :], 0), axis=1)
    real_rows = jnp.clip(group_real_end - tile_index * tm, 0, tm)
    tile_fill = jnp.where(tile_index * tm >= ends[-1], 0, jnp.where(real_rows > tm // 2, 2, 1))
    tiles = (tile_expert, tile_row, tile_fill.astype(jnp.int32))

    fill_starts = (starts + counts) // SUBLANES * SUBLANES
    xs = _dispatch(jnp.concatenate([fill_starts, ends[-1:]]), pos, x32, n_slots, tm)
    h = _expert_up(tiles, xs, w1, w3, tm)
    ys = _expert_down(tiles, h, w2, tm)
    return _gather_mix_ln(pos, ys, gates.T, x32, g, b, alpha, *ple)


def kernel(x, p, attn_w_qkv, attn_w_o, pool_w_in, pool_w_group, pool_scale, pool_w_o,
           ln_mix_g, ln_mix_b, ln_ffn_g, ln_ffn_b, ffn_w1, ffn_w3, ffn_w2,
           moe_router, moe_w1, moe_w3, moe_w2, ple_w_proj, ple_w_gate, ple_b_gate):
    batch, seq, d = x.shape
    depth = p.shape[0]
    att_w = N_HEADS * LANES
    assert attn_w_qkv.shape[2] == N_GROUPS * 3 * att_w
    assert depth % 2 == 0
    alpha = (2 * depth) ** 0.25
    m = batch * seq
    bf = lambda t: t.astype(BF16)

    x32 = x.reshape(m, d)
    xb = None
    for i in range(depth):
        j = i // 2
        ple = (bf(ple_w_gate[i]), ple_b_gate[i], bf(p[i].reshape(m, -1)), bf(ple_w_proj[i]))
        if i % 2 == 0:
            outs, lses, expert_w = [], [], []
            x_groups = _cast_and_regroup(x32, batch, seq)
            expert_w32 = (moe_w1[j], moe_w3[j], moe_w2[j])
            assert len(expert_w32) == N_GROUPS
            for grp, (_, dil) in enumerate(DILATED_GROUPS):
                side = expert_w32[grp]
                qkv, side_b = _qkv_rope(x_groups[grp], attn_w_qkv[j], _rope_tables(seq, dil), grp, seq,
                                        att_w, side.reshape(-1, side.shape[-1]))
                expert_w.append(side_b.reshape(side.shape))
                o_g, lse_g = _dilated_attention(qkv, grp, seq, att_w)
                outs.append(o_g)
                lses.append(lse_g)
            x32, xb = _attn_out(outs, lses, bf(attn_w_o[j]), x32, ln_mix_g[i], ln_mix_b[i],
                                alpha, batch, seq)
            h = _swiglu_up(xb, ffn_w1[j], ffn_w3[j])
            x32, xb = _matmul_res_ln_ple(h, bf(ffn_w2[j]), x32, ln_ffn_g[i], ln_ffn_b[i], alpha, *ple)
        else:
            x32, xb = _pool_mixer_ln(
                xb, x32, bf(pool_w_in[j]), bf(pool_w_group[j]), pool_scale[j].reshape(-1),
                bf(pool_w_o[j]), ln_mix_g[i], ln_mix_b[i], alpha, batch, seq)
            x32, xb = _moe_ffn_ln(x32, xb, moe_router[j], *expert_w,
                                  ln_ffn_g[i], ln_ffn_b[i], alpha, ple)
    return x32.reshape(batch, seq, d)
```

```python
import functools
import math

import jax
import jax.numpy as jnp
from jax import lax
from jax.experimental import pallas as pl
from jax.experimental.pallas import tpu as pltpu

F32 = jnp.float32
BF16 = jnp.bfloat16

N_HEADS = 16
DILATED_GROUPS = ((128, 1), (512, 4), (2048, 16))
N_GROUPS = len(DILATED_GROUPS)
ATT_BLOCK = 128
ATT_BLOCKS_PER_STEP = 4
REGROUP_CHUNK = 256
SIDE_BLOCK_ELEMS = 1024 * 1024
ROPE_THETA = 10000.0
POOL_WINDOWS = (2, 4, 8, 16)
POOL_HALO = 16
TOP_K = 2
ROW_COPY_UNROLL = 4
LN_EPS = 1e-5
NEG_INF = -1e30

LANES = 128
SUBLANES = 8
MIB = 1024 * 1024


def _tile(dim, pref):
    t = min(dim, pref)
    while dim % t:
        t //= 2
    return t


def _params(semantics, vmem_mib, **kw):
    return pltpu.CompilerParams(dimension_semantics=semantics, vmem_limit_bytes=vmem_mib * MIB, **kw)


def _layer_norm(y, g, b):
    mu = jnp.mean(y, axis=-1, keepdims=True)
    yc = y - mu
    var = jnp.mean(yc * yc, axis=-1, keepdims=True)
    return yc * lax.rsqrt(var + LN_EPS) * g + b


def _regroup_kernel(x_ref, perm_ref, xb_ref, *group_refs, dils, chunk):
    xb = x_ref[...].astype(xb_ref.dtype)
    xb_ref[...] = xb
    for which, (dil, o_ref) in enumerate(zip(dils, group_refs)):
        slab = chunk // dil
        for ch in range(x_ref.shape[0] // chunk):
            grouped = jnp.dot(perm_ref[which], xb[ch * chunk:(ch + 1) * chunk],
                              preferred_element_type=F32).astype(o_ref.dtype)
            for r in range(dil):
                o_ref[r, ch * slab:(ch + 1) * slab, :] = grouped[r * slab:(r + 1) * slab]


def _cast_and_regroup(x32, batch, seq):
    m, d = x32.shape
    dils = tuple(dil for _, dil in DILATED_GROUPS if dil > 1)
    chunk = REGROUP_CHUNK
    tm = _tile(seq, 512)
    assert tm % chunk == 0 and all(chunk % (dil * 2 * SUBLANES) == 0 for dil in dils)
    src = jnp.arange(chunk)
    perms = jnp.stack([((src[:, None] % (chunk // dil)) * dil + src[:, None] // (chunk // dil)
                        == src[None, :]).astype(BF16) for dil in dils])
    s_tiles = seq // tm
    outs = pl.pallas_call(
        functools.partial(_regroup_kernel, dils=dils, chunk=chunk),
        grid=(batch, s_tiles),
        in_specs=[pl.BlockSpec((tm, d), lambda b, s: (b * s_tiles + s, 0)),
                  pl.BlockSpec(perms.shape, lambda b, s: (0, 0, 0))],
        out_specs=[pl.BlockSpec((tm, d), lambda b, s: (b * s_tiles + s, 0))]
        + [pl.BlockSpec((None, dil, tm // dil, d), lambda b, s: (b, 0, s, 0)) for dil in dils],
        out_shape=[jax.ShapeDtypeStruct((m, d), BF16)]
        + [jax.ShapeDtypeStruct((batch, dil, seq // dil, d), BF16) for dil in dils],
        compiler_params=_params(("parallel", "parallel"), 48),
        name="cast_and_regroup",
    )(x32, perms)
    by_dil = {1: outs[0], **{dil: o.reshape(m, d) for dil, o in zip(dils, outs[1:])}}
    return [by_dil[dil] for _, dil in DILATED_GROUPS]


def _qkv_rope_kernel(x_ref, w_ref, cos_ref, sin_ref, side_ref, o_ref, side_out_ref, wb_ref):
    @pl.when(pl.program_id(1) == 0)
    def _():
        wb_ref[...] = w_ref[...].astype(wb_ref.dtype)

    side_out_ref[...] = side_ref[...].astype(side_out_ref.dtype)

    acc = jnp.dot(x_ref[...], wb_ref[...], preferred_element_type=F32)
    cos = cos_ref[...]
    sin = sin_ref[...]
    for h in range(acc.shape[1] // LANES):
        t = acc[:, h * LANES:(h + 1) * LANES]
        rot = pltpu.roll(t, LANES // 2, 1)
        o_ref[:, h * LANES:(h + 1) * LANES] = (t * cos + rot * sin).astype(o_ref.dtype)


def _qkv_rope(x_g, w, tables, group, seq, att_w, side):
    m, k = x_g.shape
    cos_tab, sin_tab = tables
    tm = _tile(seq, 1024)
    tn = _tile(att_w, 1024)
    s_tiles = seq // tm
    per_role = att_w // tn
    per_group = 3 * per_role
    row_tiles = m // tm
    table_spec = pl.BlockSpec((None, tm, LANES), lambda j, i: (j // per_role, i % s_tiles, 0))
    side_cols = _tile(side.shape[1], 2048)
    side_rows = SIDE_BLOCK_ELEMS // side_cols
    assert side.shape[0] % side_rows == 0
    col_blocks = side.shape[1] // side_cols
    side_blocks = (side.shape[0] // side_rows) * col_blocks
    assert side_blocks <= per_group * row_tiles, "not enough grid steps to convert the side matrix"

    def side_index(j, i):
        block = jnp.minimum(j * row_tiles + i, side_blocks - 1)
        return (block // col_blocks, block % col_blocks)

    side_spec = pl.BlockSpec((side_rows, side_cols), side_index)
    return pl.pallas_call(
        _qkv_rope_kernel,
        grid=(per_group, row_tiles),
        in_specs=[pl.BlockSpec((tm, k), lambda j, i: (i, 0)),
                  pl.BlockSpec((k, tn), lambda j, i: (0, group * per_group + j)),
                  table_spec, table_spec, side_spec],
        out_specs=[pl.BlockSpec((tm, tn), lambda j, i: (i, j)), side_spec],
        out_shape=[jax.ShapeDtypeStruct((m, 3 * att_w), BF16),
                   jax.ShapeDtypeStruct(side.shape, BF16)],
        scratch_shapes=[pltpu.VMEM((k, tn), BF16)],
        compiler_params=_params(("arbitrary", "arbitrary"), 60),
        name=f"qkv_rope_g{group}",
    )(x_g, w, cos_tab, sin_tab, side)


def _rope_tables(seq, dil):
    half = LANES // 2
    inv = ROPE_THETA ** (-jnp.arange(half, dtype=F32) / half)
    row = jnp.arange(seq)
    n_sub = seq // dil
    pos = (row % n_sub) * dil + row // n_sub
    ang = pos.astype(F32)[:, None] * inv[None, :]
    cos, sin = jnp.cos(ang), jnp.sin(ang)
    cos_full = jnp.concatenate([cos, cos], axis=-1)
    sin_signed = jnp.concatenate([-sin, sin], axis=-1)
    return (jnp.stack([cos_full, cos_full, jnp.ones_like(cos_full)]),
            jnp.stack([sin_signed, sin_signed, jnp.zeros_like(cos_full)]))


def _attn_kernel(q_ref, kc_ref, kp_ref, vc_ref, vp_ref, o_ref, lse_ref, *, nb, n_back, scale):
    blk = kp_ref.shape[0]
    sub = q_ref.shape[0] // blk
    first_has_prev = lax.rem(pl.program_id(0) * sub, nb) > 0
    qi = lax.broadcasted_iota(jnp.int32, (blk, 2 * blk), 0) + blk
    ki = lax.broadcasted_iota(jnp.int32, (blk, 2 * blk), 1)
    dist = qi - ki
    band = (dist >= 0) & (dist <= n_back)
    bias_inner = jnp.where(band, 0.0, NEG_INF)
    bias_first = jnp.where(band & (first_has_prev | (ki >= blk)), 0.0, NEG_INF)
    lane = lax.broadcasted_iota(jnp.int32, (blk, LANES), 1)
    for t in range(sub):
        rows = slice(t * blk, (t + 1) * blk)
        before = slice((t - 1) * blk, t * blk)
        bias = bias_first if t == 0 else bias_inner
        lse_tile = jnp.zeros((blk, LANES), F32)
        for h in range(q_ref.shape[1] // LANES):
            hs = slice(h * LANES, (h + 1) * LANES)
            k_prev = kp_ref[:, hs] if t == 0 else kc_ref[before, hs]
            v_prev = vp_ref[:, hs] if t == 0 else vc_ref[before, hs]
            k2 = jnp.concatenate([k_prev, kc_ref[rows, hs]], axis=0)
            v2 = jnp.concatenate([v_prev, vc_ref[rows, hs]], axis=0)
            s = lax.dot_general(q_ref[rows, hs], k2, (((1,), (1,)), ((), ())),
                                preferred_element_type=F32) * scale + bias
            m = jnp.max(s, axis=-1, keepdims=True)
            e = jnp.exp(s - m)
            l = jnp.sum(e, axis=-1, keepdims=True)
            p = (e * (1.0 / l)).astype(v2.dtype)
            o_ref[rows, hs] = jnp.dot(p, v2, preferred_element_type=F32)
            lse_tile = jnp.where(lane == h, m + jnp.log(l), lse_tile)
        lse_ref[rows, :] = lse_tile


def _dilated_attention(qkv, group, seq, att_w):
    m = qkv.shape[0]
    window, dil = DILATED_GROUPS[group]
    nb = seq // (dil * ATT_BLOCK)
    sub = math.gcd(nb, ATT_BLOCKS_PER_STEP)
    assert seq % (dil * ATT_BLOCK) == 0 and window // dil <= ATT_BLOCK
    step_rows = sub * ATT_BLOCK

    def spec(role, prev):
        if prev:
            return pl.BlockSpec((ATT_BLOCK, att_w), lambda r: (jnp.maximum(sub * r - 1, 0), role))
        return pl.BlockSpec((step_rows, att_w), lambda r: (r, role))

    return pl.pallas_call(
        functools.partial(_attn_kernel, nb=nb, n_back=window // dil, scale=LANES ** -0.5),
        grid=(m // step_rows,),
        in_specs=[spec(0, False), spec(1, False), spec(1, True), spec(2, False), spec(2, True)],
        out_specs=[pl.BlockSpec((step_rows, att_w), lambda r: (r, 0)),
                   pl.BlockSpec((step_rows, LANES), lambda r: (r, 0))],
        out_shape=[jax.ShapeDtypeStruct((m, att_w), F32), jax.ShapeDtypeStruct((m, LANES), F32)],
        compiler_params=_params(("arbitrary",), 32),
        name=f"dilated_attn_g{group}",
    )(qkv, qkv, qkv, qkv, qkv)


def _token_order(ref, scr_ref):
    dil, n, w = ref.shape
    heads = range(w // LANES)
    if dil == 1:
        return [ref[0, :, h * LANES:(h + 1) * LANES] for h in heads]
    for r in range(dil):
        for h in heads:
            scr_ref[h, pl.ds(r, n, stride=dil), :] = ref[r, :, h * LANES:(h + 1) * LANES]
    return [scr_ref[h] for h in heads]


def _attn_out_kernel(*refs, alpha):
    o_refs, l_refs = refs[:N_GROUPS], refs[N_GROUPS:2 * N_GROUPS]
    w_ref, x_ref, g_ref, b_ref, o32_ref, o16_ref = refs[2 * N_GROUPS:2 * N_GROUPS + 6]
    scratch = refs[2 * N_GROUPS + 6:]
    o_scr, l_scr = scratch[:N_GROUPS], scratch[N_GROUPS:]
    outs = [_token_order(o_refs[g], o_scr[g]) for g in range(N_GROUPS)]
    lses = [_token_order(l_refs[g], l_scr[g])[0] for g in range(N_GROUPS)]
    m = functools.reduce(jnp.maximum, lses)
    es = [jnp.exp(l - m) for l in lses]
    inv = 1.0 / functools.reduce(jnp.add, es)
    wts = [e * inv for e in es]
    tm = x_ref.shape[0]
    heads = []
    for h in range(len(outs[0])):
        acc = None
        for g in range(N_GROUPS):
            term = jnp.broadcast_to(wts[g][:, h:h + 1], (tm, LANES)) * outs[g][h]
            acc = term if acc is None else acc + term
        heads.append(acc.astype(w_ref.dtype))
    mixed = jnp.concatenate(heads, axis=1)
    hid = jnp.dot(mixed, w_ref[...], preferred_element_type=F32)
    y = _layer_norm(alpha * x_ref[...] + hid, g_ref[...], b_ref[...])
    o32_ref[...] = y
    o16_ref[...] = y.astype(o16_ref.dtype)


def _attn_out(outs, lses, w_o, x_res, g, b, alpha, batch, seq):
    n_groups = len(outs)
    m, att_w = outs[0].shape
    d = w_o.shape[1]
    dils = [dil for _, dil in DILATED_GROUPS]
    tm = _tile(seq, 256)
    s_tiles = seq // tm
    assert all(tm % (8 * dil) == 0 for dil in dils)

    def group_view(t, g):
        return t.reshape(batch, dils[g], seq // dils[g], t.shape[-1])

    def group_spec(g, width):
        return pl.BlockSpec((None, dils[g], tm // dils[g], width),
                            lambda i: (i // s_tiles, 0, i % s_tiles, 0))

    row = lambda i: (i, 0)
    fixed = lambda i: (0, 0)
    return pl.pallas_call(
        functools.partial(_attn_out_kernel, alpha=alpha),
        grid=(m // tm,),
        in_specs=[group_spec(g, att_w) for g in range(n_groups)]
        + [group_spec(g, LANES) for g in range(n_groups)]
        + [pl.BlockSpec((att_w, d), fixed, pipeline_mode=pl.Buffered(1)),
           pl.BlockSpec((tm, d), row), pl.BlockSpec((1, d), fixed), pl.BlockSpec((1, d), fixed)],
        out_specs=[pl.BlockSpec((tm, d), row), pl.BlockSpec((tm, d), row)],
        out_shape=[jax.ShapeDtypeStruct((m, d), F32), jax.ShapeDtypeStruct((m, d), BF16)],
        scratch_shapes=[pltpu.VMEM((att_w // LANES, tm, LANES), F32) for _ in range(n_groups)]
        + [pltpu.VMEM((1, tm, LANES), F32) for _ in range(n_groups)],
        compiler_params=_params(("parallel",), 56),
        name="attn_out_ln",
    )(*[group_view(outs[g], g) for g in range(n_groups)],
      *[group_view(lses[g], g) for g in range(n_groups)],
      w_o, x_res, g.reshape(1, d), b.reshape(1, d))


def _gated_embedding(y, wg_ref, bg_ref, p_ref, wp_ref):
    z = jnp.dot(y.astype(wg_ref.dtype), wg_ref[...], preferred_element_type=F32) + bg_ref[...]
    e = jnp.dot(p_ref[...], wp_ref[...], preferred_element_type=F32)
    return y + jax.nn.sigmoid(z) * e


def _ple_specs(tm, d, kp, row, fixed):
    return [pl.BlockSpec((d, d), fixed, pipeline_mode=pl.Buffered(1)),
            pl.BlockSpec((1, d), fixed),
            pl.BlockSpec((tm, kp), row),
            pl.BlockSpec((kp, d), fixed, pipeline_mode=pl.Buffered(1))]


def _matmul_res_ln_ple_kernel(a_ref, w_ref, x_ref, g_ref, b_ref, wg_ref, bg_ref, p_ref, wp_ref,
                              o32_ref, o16_ref, *, alpha):
    h = jnp.dot(a_ref[...], w_ref[...], preferred_element_type=F32)
    y = _layer_norm(alpha * x_ref[...] + h, g_ref[...], b_ref[...])
    y = _gated_embedding(y, wg_ref, bg_ref, p_ref, wp_ref)
    o32_ref[...] = y
    o16_ref[...] = y.astype(o16_ref.dtype)


def _matmul_res_ln_ple(a, w, x_res, g, b, alpha, w_gate, b_gate, pb, w_proj):
    m, k = a.shape
    n = w.shape[1]
    kp = pb.shape[1]
    tm = _tile(m, 256)
    row = lambda i: (i, 0)
    fixed = lambda i: (0, 0)
    return pl.pallas_call(
        functools.partial(_matmul_res_ln_ple_kernel, alpha=alpha),
        grid=(m // tm,),
        in_specs=[pl.BlockSpec((tm, k), row),
                  pl.BlockSpec((k, n), fixed, pipeline_mode=pl.Buffered(1)),
                  pl.BlockSpec((tm, n), row), pl.BlockSpec((1, n), fixed), pl.BlockSpec((1, n), fixed)]
        + _ple_specs(tm, n, kp, row, fixed),
        out_specs=[pl.BlockSpec((tm, n), row), pl.BlockSpec((tm, n), row)],
        out_shape=[jax.ShapeDtypeStruct((m, n), F32), jax.ShapeDtypeStruct((m, n), BF16)],
        compiler_params=_params(("parallel",), 60),
        name="matmul_res_ln_ple",
    )(a, w, x_res, g.reshape(1, n), b.reshape(1, n), w_gate, b_gate.reshape(1, n), pb, w_proj)


def _swiglu_up_kernel(x_ref, w1_ref, w3_ref, o_ref, w1b_ref, w3b_ref):
    @pl.when(pl.program_id(1) == 0)
    def _():
        w1b_ref[...] = w1_ref[...].astype(w1b_ref.dtype)
        w3b_ref[...] = w3_ref[...].astype(w3b_ref.dtype)

    x = x_ref[...]
    a = jnp.dot(x, w1b_ref[...], preferred_element_type=F32)
    b = jnp.dot(x, w3b_ref[...], preferred_element_type=F32)
    o_ref[...] = (a * jax.nn.sigmoid(a) * b).astype(o_ref.dtype)


def _swiglu_up(xb, w1, w3):
    m, k = xb.shape
    n = w1.shape[1]
    tm, tn = _tile(m, 1024), _tile(n, 512)
    return pl.pallas_call(
        _swiglu_up_kernel,
        grid=(n // tn, m // tm),
        in_specs=[pl.BlockSpec((tm, k), lambda j, i: (i, 0)),
                  pl.BlockSpec((k, tn), lambda j, i: (0, j)),
                  pl.BlockSpec((k, tn), lambda j, i: (0, j))],
        out_specs=pl.BlockSpec((tm, tn), lambda j, i: (i, j)),
        out_shape=jax.ShapeDtypeStruct((m, n), BF16),
        scratch_shapes=[pltpu.VMEM((k, tn), BF16), pltpu.VMEM((k, tn), BF16)],
        compiler_params=_params(("parallel", "arbitrary"), 56),
        name="swiglu_up",
    )(xb, w1, w3)


def _pool_mixer_kernel(xb_ref, win_ref, wg_ref, sc_ref, wo_ref, x_ref, g_ref, b_ref,
                       o32_ref, o16_ref, buf_ref, *, alpha):
    s = pl.program_id(1)
    tm = xb_ref.shape[0]
    pd = wg_ref.shape[1]

    @pl.when(s == 0)
    def _():
        buf_ref[0:POOL_HALO, :] = jnp.zeros((POOL_HALO, buf_ref.shape[1]), F32)

    @pl.when(s > 0)
    def _():
        buf_ref[0:POOL_HALO, :] = buf_ref[tm:tm + POOL_HALO, :]

    buf_ref[POOL_HALO:POOL_HALO + tm, :] = jnp.dot(xb_ref[...], win_ref[...], preferred_element_type=F32)
    t = s * tm + lax.broadcasted_iota(jnp.int32, (tm, 1), 0)
    pieces = []
    for g, w in enumerate(POOL_WINDOWS):
        cols = slice(g * pd, (g + 1) * pd)
        u = buf_ref[POOL_HALO:POOL_HALO + tm, cols]
        tot = u
        for i in range(1, w):
            tot = tot + buf_ref[POOL_HALO - i:POOL_HALO - i + tm, cols]
        cnt = jnp.minimum(t + 1, w).astype(F32)
        mixed = (tot / cnt - u).astype(wg_ref.dtype)
        y = jnp.dot(mixed, wg_ref[g], preferred_element_type=F32) * sc_ref[:, cols]
        pieces.append(y.astype(wo_ref.dtype))
    hid = jnp.dot(jnp.concatenate(pieces, axis=1), wo_ref[...], preferred_element_type=F32)
    y = _layer_norm(alpha * x_ref[...] + hid, g_ref[...], b_ref[...])
    o32_ref[...] = y
    o16_ref[...] = y.astype(o16_ref.dtype)


def _pool_mixer_ln(xb, x_res, w_in, w_group, scale, w_o, g, b, alpha, batch, seq):
    m, d = x_res.shape
    tm = _tile(seq, 256)
    s_tiles = seq // tm
    assert tm % SUBLANES == 0 and max(POOL_WINDOWS) <= POOL_HALO <= tm
    row = lambda bi, s: (bi * s_tiles + s, 0)
    fixed = lambda bi, s: (0, 0)
    resident = lambda shape, index: pl.BlockSpec(shape, index, pipeline_mode=pl.Buffered(1))
    return pl.pallas_call(
        functools.partial(_pool_mixer_kernel, alpha=alpha),
        grid=(batch, s_tiles),
        in_specs=[pl.BlockSpec((tm, d), row),
                  resident((d, d), fixed),
                  resident(w_group.shape, lambda bi, s: (0, 0, 0)),
                  pl.BlockSpec((1, d), fixed),
                  resident((d, d), fixed),
                  pl.BlockSpec((tm, d), row), pl.BlockSpec((1, d), fixed), pl.BlockSpec((1, d), fixed)],
        out_specs=[pl.BlockSpec((tm, d), row), pl.BlockSpec((tm, d), row)],
        out_shape=[jax.ShapeDtypeStruct((m, d), F32), jax.ShapeDtypeStruct((m, d), BF16)],
        scratch_shapes=[pltpu.VMEM((POOL_HALO + tm, d), F32)],
        compiler_params=_params(("parallel", "arbitrary"), 56),
        name="pool_mixer_ln",
    )(xb, w_in, w_group, scale.reshape(1, d), w_o, x_res, g.reshape(1, d), b.reshape(1, d))


def _router_kernel(x_ref, wr_ref, tri_ref, e_ref, rank_ref, gate_ref, cnt_ref, run_ref):
    @pl.when(pl.program_id(0) == 0)
    def _():
        run_ref[...] = jnp.zeros_like(run_ref)

    logits = lax.dot_general(wr_ref[...], x_ref[...], (((1,), (1,)), ((), ())),
                             preferred_element_type=F32)
    n_exp = logits.shape[0]
    eidx = lax.broadcasted_iota(jnp.int32, logits.shape, 0)
    m1 = jnp.max(logits, axis=0, keepdims=True)
    i1 = jnp.min(jnp.where(logits == m1, eidx, n_exp), axis=0, keepdims=True)
    rest = jnp.where(eidx == i1, -jnp.inf, logits)
    m2 = jnp.max(rest, axis=0, keepdims=True)
    i2 = jnp.min(jnp.where(rest == m2, eidx, n_exp), axis=0, keepdims=True)
    ex = jnp.exp(m2 - m1)
    g1 = 1.0 / (1.0 + ex)
    g2 = ex / (1.0 + ex)

    oh1 = (eidx == i1).astype(F32)
    oh2 = (eidx == i2).astype(F32)
    tri = tri_ref[...]
    c1 = jnp.dot(oh1.astype(tri.dtype), tri, preferred_element_type=F32)
    c2 = jnp.dot(oh2.astype(tri.dtype), tri, preferred_element_type=F32)
    tot1 = jnp.sum(oh1, axis=1, keepdims=True)
    tot2 = jnp.sum(oh2, axis=1, keepdims=True)
    run = run_ref[:, 0:1]
    r1 = jnp.sum(oh1 * (run + c1), axis=0, keepdims=True)
    r2 = jnp.sum(oh2 * (run + tot1 + c2), axis=0, keepdims=True)
    run = run + tot1 + tot2
    run_ref[...] = jnp.broadcast_to(run, run_ref.shape)

    e_ref[...] = jnp.concatenate([i1, i2], axis=0)
    rank_ref[...] = jnp.concatenate([r1, r2], axis=0).astype(jnp.int32)
    gate_ref[...] = jnp.concatenate([g1, g2], axis=0)
    cnt_ref[...] = jnp.broadcast_to(run, cnt_ref.shape).astype(jnp.int32)


def _router(xb, w_router_t):
    m, k = xb.shape
    n_exp = w_router_t.shape[0]
    tm = _tile(m, 512)
    tri = (jnp.arange(tm)[:, None] < jnp.arange(tm)[None, :]).astype(BF16)
    pair = lambda dt: jax.ShapeDtypeStruct((TOP_K, m), dt)
    pair_spec = pl.BlockSpec((TOP_K, tm), lambda i: (0, i))
    return pl.pallas_call(
        _router_kernel,
        grid=(m // tm,),
        in_specs=[pl.BlockSpec((tm, k), lambda i: (i, 0)),
                  pl.BlockSpec((n_exp, k), lambda i: (0, 0)),
                  pl.BlockSpec((tm, tm), lambda i: (0, 0))],
        out_specs=[pair_spec, pair_spec, pair_spec, pl.BlockSpec((n_exp, LANES), lambda i: (0, 0))],
        out_shape=[pair(jnp.int32), pair(jnp.int32), pair(F32),
                   jax.ShapeDtypeStruct((n_exp, LANES), jnp.int32)],
        scratch_shapes=[pltpu.VMEM((n_exp, LANES), F32)],
        compiler_params=_params(("arbitrary",), 32),
        name="router",
    )(xb, w_router_t, tri)


def _tile_positions(pos, tm):
    return pos.reshape(TOP_K, -1, tm).transpose(1, 0, 2)


def _dispatch_kernel(fill_ref, pos_ref, x_ref, xs_ref, stage_ref, zero_ref, sem, fill_sem):
    i = pl.program_id(0)
    tm = x_ref.shape[0]
    slot = lax.rem(i, 2)

    @pl.when(i == 0)
    def _():
        zero_ref[...] = jnp.zeros_like(zero_ref)
        fill_rows = zero_ref.shape[0]
        n_groups = fill_ref.shape[0] - 1

        def fill(start):
            return pltpu.make_async_copy(
                zero_ref, xs_ref.at[pl.ds(pl.multiple_of(start, SUBLANES), fill_rows), :], fill_sem)

        fills = [fill(fill_ref[e]) for e in range(n_groups)]
        for c in fills:
            c.start()
        for c in fills:
            c.wait()

        def fill_unused(t, carry):
            c = fill(t * fill_rows)
            c.start()
            c.wait()
            return carry
        lax.fori_loop(fill_ref[n_groups] // fill_rows, xs_ref.shape[0] // fill_rows, fill_unused, 0)

    def row_copy(s, r, k):
        return pltpu.make_async_copy(stage_ref.at[s, pl.ds(r, 1), :],
                                     xs_ref.at[pl.ds(pos_ref[k, r], 1), :], sem.at[s])

    def start(r, carry):
        for k in range(TOP_K):
            row_copy(slot, r, k).start(priority=k % 2)
        return carry

    def drain(s):
        def wait(r, carry):
            for k in range(TOP_K):
                pltpu.make_async_copy(stage_ref.at[s, pl.ds(r, 1), :],
                                      xs_ref.at[pl.ds(0, 1), :], sem.at[s]).wait()
            return carry
        lax.fori_loop(0, tm, wait, 0, unroll=ROW_COPY_UNROLL)

    stage_ref[slot] = x_ref[...]
    lax.fori_loop(0, tm, start, 0, unroll=ROW_COPY_UNROLL)

    @pl.when(i > 0)
    def _():
        drain(1 - slot)

    @pl.when(i == pl.num_programs(0) - 1)
    def _():
        drain(slot)


def _dispatch(fill_starts, pos, xp, n_slots, fill_rows):
    m, w = xp.shape
    tm = _tile(m, 512)
    grid_spec = pltpu.PrefetchScalarGridSpec(
        num_scalar_prefetch=1,
        grid=(m // tm,),
        in_specs=[pl.BlockSpec((None, TOP_K, tm), lambda i, fill: (i, 0, 0), memory_space=pltpu.SMEM),
                  pl.BlockSpec((tm, w), lambda i, fill: (i, 0))],
        out_specs=pl.BlockSpec(memory_space=pl.ANY),
        scratch_shapes=[pltpu.VMEM((2, tm, w), xp.dtype), pltpu.VMEM((fill_rows, w), xp.dtype),
                        pltpu.SemaphoreType.DMA((2,)), pltpu.SemaphoreType.DMA(())],
    )
    return pl.pallas_call(
        _dispatch_kernel,
        grid_spec=grid_spec,
        out_shape=jax.ShapeDtypeStruct((n_slots, w), xp.dtype),
        compiler_params=_params(("arbitrary",), 48, disable_bounds_checks=True),
        name="dispatch",
    )(fill_starts, _tile_positions(pos, tm), xp)


def _for_used_rows(fill_ref, out_ref, compute, tile_axis=1):
    tm = out_ref.shape[0]
    fill = fill_ref[pl.program_id(tile_axis)]

    @pl.when(fill == 2)
    def _():
        compute(slice(0, tm))

    @pl.when(fill == 1)
    def _():
        compute(slice(0, tm // 2))
        out_ref[tm // 2:, :] = jnp.zeros((tm - tm // 2, out_ref.shape[1]), out_ref.dtype)

    @pl.when(fill == 0)
    def _():
        out_ref[...] = jnp.zeros_like(out_ref)


def _expert_up_kernel(te_ref, tr_ref, tv_ref, xs_ref, w1_ref, w3_ref, h_ref, xb_ref):
    del te_ref, tr_ref

    @pl.when(pl.program_id(1) == 0)
    def _():
        xb_ref[...] = xs_ref[...].astype(xb_ref.dtype)

    def compute(rows):
        x = xb_ref[rows, :]
        a = jnp.dot(x, w1_ref[...], preferred_element_type=F32)
        b = jnp.dot(x, w3_ref[...], preferred_element_type=F32)
        h_ref[rows, :] = (a * jax.nn.sigmoid(a) * b).astype(h_ref.dtype)

    _for_used_rows(tv_ref, h_ref, compute, tile_axis=0)


def _expert_specs():
    rows = lambda c, i, te, tr, tv: (tr[i], 0)
    weights = lambda c, i, te, tr, tv: (te[i], 0, c)
    out = lambda c, i, te, tr, tv: (i, c)
    return rows, weights, out


def _expert_up(tiles, xs, w1, w3, tm):
    n_slots = xs.shape[0]
    _, k, n = w1.shape
    tn = _tile(n, 512)
    last_chunk = n // tn - 1
    rows = lambda i, c, te, tr, tv: (tr[i], 0)
    weights = lambda i, c, te, tr, tv: (te[i], 0, jnp.where(tv[i] > 0, c, last_chunk))
    weight_spec = pl.BlockSpec((None, k, tn), weights)
    grid_spec = pltpu.PrefetchScalarGridSpec(
        num_scalar_prefetch=3,
        grid=(n_slots // tm, n // tn),
        in_specs=[pl.BlockSpec((tm, k), rows), weight_spec, weight_spec],
        out_specs=pl.BlockSpec((tm, tn), lambda i, c, te, tr, tv: (i, c)),
        scratch_shapes=[pltpu.VMEM((tm, k), BF16)],
    )
    return pl.pallas_call(
        _expert_up_kernel,
        grid_spec=grid_spec,
        out_shape=jax.ShapeDtypeStruct((n_slots, n), BF16),
        compiler_params=_params(("arbitrary", "arbitrary"), 48),
        name="expert_up",
    )(*tiles, xs, w1, w3)


def _expert_down_kernel(te_ref, tr_ref, tv_ref, h_ref, w2_ref, y_ref):
    del te_ref, tr_ref

    def compute(rows):
        y_ref[rows, :] = jnp.dot(h_ref[rows, :], w2_ref[...], preferred_element_type=F32)

    _for_used_rows(tv_ref, y_ref, compute)


def _expert_down(tiles, h, w2, tm):
    n_slots, k = h.shape
    n = w2.shape[2]
    tn = _tile(n, 1024)
    rows, weights, out = _expert_specs()
    grid_spec = pltpu.PrefetchScalarGridSpec(
        num_scalar_prefetch=3,
        grid=(n // tn, n_slots // tm),
        in_specs=[pl.BlockSpec((tm, k), rows),
                  pl.BlockSpec((None, k, tn), weights, pipeline_mode=pl.Buffered(1))],
        out_specs=pl.BlockSpec((tm, tn), out),
    )
    return pl.pallas_call(
        _expert_down_kernel,
        grid_spec=grid_spec,
        out_shape=jax.ShapeDtypeStruct((n_slots, n), F32),
        compiler_params=_params(("parallel", "arbitrary"), 56),
        name="expert_down",
    )(*tiles, h, w2)


def _gather_mix_ln_kernel(pos_ref, next_pos_ref, ys_ref, gate_ref, x_ref, g_ref, b_ref, wg_ref,
                          bg_ref, p_ref, wp_ref, o32_ref, o16_ref, buf_ref, sem, *, alpha):
    i = pl.program_id(0)
    tm = x_ref.shape[0]
    slot = lax.rem(i, 2)

    def issue(idx_ref, s):
        def start(r, carry):
            for k in range(TOP_K):
                pltpu.make_async_copy(ys_ref.at[pl.ds(idx_ref[k, r], 1), :],
                                      buf_ref.at[s, k, pl.ds(r, 1), :], sem.at[s]).start(priority=k % 2)
            return carry
        lax.fori_loop(0, tm, start, 0, unroll=ROW_COPY_UNROLL)

    def wait(r, carry):
        for k in range(TOP_K):
            pltpu.make_async_copy(ys_ref.at[pl.ds(0, 1), :],
                                  buf_ref.at[slot, k, pl.ds(r, 1), :], sem.at[slot]).wait()
        return carry

    @pl.when(i == 0)
    def _():
        issue(pos_ref, slot)

    @pl.when(i + 1 < pl.num_programs(0))
    def _():
        issue(next_pos_ref, 1 - slot)

    lax.fori_loop(0, tm, wait, 0, unroll=ROW_COPY_UNROLL)
    gates = gate_ref[...]
    f = gates[:, 0:1] * buf_ref[slot, 0] + gates[:, 1:2] * buf_ref[slot, 1]
    y = _layer_norm(alpha * x_ref[...] + f, g_ref[...], b_ref[...])
    y = _gated_embedding(y, wg_ref, bg_ref, p_ref, wp_ref)
    o32_ref[...] = y
    o16_ref[...] = y.astype(o16_ref.dtype)


def _gather_mix_ln(pos, ys, gates, x_res, g, b, alpha, w_gate, b_gate, pb, w_proj):
    m, d = x_res.shape
    kp = pb.shape[1]
    tm = _tile(m, 512)
    last = m // tm - 1
    row = lambda i: (i, 0)
    fixed = lambda i: (0, 0)
    pos_tiles = _tile_positions(pos, tm)
    return pl.pallas_call(
        functools.partial(_gather_mix_ln_kernel, alpha=alpha),
        grid=(m // tm,),
        in_specs=[pl.BlockSpec((None, TOP_K, tm), lambda i: (i, 0, 0), memory_space=pltpu.SMEM),
                  pl.BlockSpec((None, TOP_K, tm), lambda i: (jnp.minimum(i + 1, last), 0, 0),
                               memory_space=pltpu.SMEM),
                  pl.BlockSpec(memory_space=pl.ANY),
                  pl.BlockSpec((tm, TOP_K), row),
                  pl.BlockSpec((tm, d), row),
                  pl.BlockSpec((1, d), fixed), pl.BlockSpec((1, d), fixed)]
        + _ple_specs(tm, d, kp, row, fixed),
        out_specs=[pl.BlockSpec((tm, d), row), pl.BlockSpec((tm, d), row)],
        out_shape=[jax.ShapeDtypeStruct((m, d), F32), jax.ShapeDtypeStruct((m, d), BF16)],
        scratch_shapes=[pltpu.VMEM((2, TOP_K, tm, d), F32), pltpu.SemaphoreType.DMA((2,))],
        compiler_params=_params(("arbitrary",), 56, disable_bounds_checks=True),
        name="gather_mix_ln",
    )(pos_tiles, pos_tiles, ys, gates, x_res, g.reshape(1, d), b.reshape(1, d),
      w_gate, b_gate.reshape(1, d), pb, w_proj)


def _moe_ffn_ln(x32, xb, w_router, w1, w3, w2, g, b, alpha, ple):
    m, _ = xb.shape
    n_exp = w1.shape[0]
    tm = _tile(m, 1024)
    experts, ranks, gates, counts = _router(xb, w_router.T.astype(BF16))

    counts = counts[:, 0]
    padded = (counts + tm - 1) // tm * tm
    ends = jnp.cumsum(padded)
    starts = ends - padded
    expert_ids = jnp.arange(n_exp, dtype=jnp.int32)[:, None, None]
    pos = jnp.sum(jnp.where(experts[None] == expert_ids, starts[:, None, None], 0), axis=0) + ranks
    assert (TOP_K * m) % tm == 0
    n_slots = TOP_K * m + n_exp * tm
    tile_index = jnp.arange(n_slots // tm, dtype=jnp.int32)
    tile_row = jnp.minimum(tile_index, jnp.maximum(ends[-1] // tm - 1, 0))
    tile_expert = jnp.sum((ends[None, :] <= (tile_row * tm)[:, None]).astype(jnp.int32), axis=1)
    tile_expert = jnp.minimum(tile_expert, n_exp - 1)
    group_real_end = jnp.sum(jnp.where(tile_expert[:, None] == jnp.arange(n_exp)[None, :],
                                       (starts + counts)[None, :], 0), axis=1)
    real_rows = jnp.clip(group_real_end - tile_index * tm, 0, tm)
    tile_fill = jnp.where(tile_index * tm >= ends[-1], 0, jnp.where(real_rows > tm // 2, 2, 1))
    tiles = (tile_expert, tile_row, tile_fill.astype(jnp.int32))

    fill_starts = (starts + counts) // SUBLANES * SUBLANES
    xs = _dispatch(jnp.concatenate([fill_starts, ends[-1:]]), pos, x32, n_slots, tm)
    h = _expert_up(tiles, xs, w1, w3, tm)
    ys = _expert_down(tiles, h, w2, tm)
    return _gather_mix_ln(pos, ys, gates.T, x32, g, b, alpha, *ple)


def kernel(x, p, attn_w_qkv, attn_w_o, pool_w_in, pool_w_group, pool_scale, pool_w_o,
           ln_mix_g, ln_mix_b, ln_ffn_g, ln_ffn_b, ffn_w1, ffn_w3, ffn_w2,
           moe_router, moe_w1, moe_w3, moe_w2, ple_w_proj, ple_w_gate, ple_b_gate):
    batch, seq, d = x.shape
    depth = p.shape[0]
    att_w = N_HEADS * LANES
    assert attn_w_qkv.shape[2] == N_GROUPS * 3 * att_w
    assert depth % 2 == 0
    alpha = (2 * depth) ** 0.25
    m = batch * seq
    bf = lambda t: t.astype(BF16)

    x32 = x.reshape(m, d)
    xb = None
    for i in range(depth):
        j = i // 2
        ple = (bf(ple_w_gate[i]), ple_b_gate[i], bf(p[i].reshape(m, -1)), bf(ple_w_proj[i]))
        if i % 2 == 0:
            outs, lses, expert_w = [], [], []
            x_groups = _cast_and_regroup(x32, batch, seq)
            expert_w32 = (moe_w1[j], moe_w3[j], moe_w2[j])
            assert len(expert_w32) == N_GROUPS
            for grp, (_, dil) in enumerate(DILATED_GROUPS):
                side = expert_w32[grp]
                qkv, side_b = _qkv_rope(x_groups[grp], attn_w_qkv[j], _rope_tables(seq, dil), grp, seq,
                                        att_w, side.reshape(-1, side.shape[-1]))
                expert_w.append(side_b.reshape(side.shape))
                o_g, lse_g = _dilated_attention(qkv, grp, seq, att_w)
                outs.append(o_g)
                lses.append(lse_g)
            x32, xb = _attn_out(outs, lses, bf(attn_w_o[j]), x32, ln_mix_g[i], ln_mix_b[i],
                                alpha, batch, seq)
            h = _swiglu_up(xb, ffn_w1[j], ffn_w3[j])
            x32, xb = _matmul_res_ln_ple(h, bf(ffn_w2[j]), x32, ln_ffn_g[i], ln_ffn_b[i], alpha, *ple)
        else:
            x32, xb = _pool_mixer_ln(
                xb, x32, bf(pool_w_in[j]), bf(pool_w_group[j]), pool_scale[j].reshape(-1),
                bf(pool_w_o[j]), ln_mix_g[i], ln_mix_b[i], alpha, batch, seq)
            x32, xb = _moe_ffn_ln(x32, xb, moe_router[j], *expert_w,
                                  ln_ffn_g[i], ln_ffn_b[i], alpha, ple)
    return x32.reshape(batch, seq, d)
```
